```python
import jax, jax.numpy as jnp
from jax import lax
import numpy as np

D_MODEL = 1024
BATCH = 2
SEQ = 16384
DEPTH = 2

D_PLE = 256
HEAD_DIM = 64
N_HEADS_SGU = 4
N_HEADS_RET = 6
N_HEADS_FOX = 6
W_SGU = N_HEADS_SGU * HEAD_DIM
W_RET = N_HEADS_RET * HEAD_DIM
W_FOX = N_HEADS_FOX * HEAD_DIM
D_MIX = W_SGU + W_RET + W_FOX
CHUNK = 128
D_FF = 4 * D_MODEL
CONV_WIDTH = 3
ROPE_BASE = 10000.0
RMS_EPS = 1e-6
IN_SPLIT_SIZES = (W_SGU, W_SGU, W_RET, W_RET, W_RET, W_RET, W_FOX, W_FOX, W_FOX, N_HEADS_FOX)
N_IN = sum(IN_SPLIT_SIZES)
IN_SPLIT_POINTS = tuple(int(v) for v in np.cumsum(IN_SPLIT_SIZES)[:-1])

kernel_name = "hymba_style_sgu_retention_fox_block"


def rmsnorm(x, g):
    xf = x.astype(jnp.float32)
    y = xf * lax.rsqrt(jnp.mean(xf * xf, axis=-1, keepdims=True) + RMS_EPS)
    return (y * g.astype(jnp.float32)).astype(x.dtype)


def rotary(t, positions):
    half = t.shape[-1] // 2
    inv_freq = ROPE_BASE ** (-jnp.arange(half, dtype=jnp.float32) / half)
    ang = positions.astype(jnp.float32)[..., None] * inv_freq
    cos = jnp.cos(ang)[:, :, None, :]
    sin = jnp.sin(ang)[:, :, None, :]
    t1, t2 = t[..., :half], t[..., half:]
    return jnp.concatenate([t1 * cos - t2 * sin, t1 * sin + t2 * cos], axis=-1)


def spatial_gating(u, v, v_gain, w_s, b_s):
    B, S, H, D = v.shape
    nc = S // CHUNK
    v = rmsnorm(v, v_gain).reshape(B, nc, CHUNK, H, D)
    causal = jnp.tril(jnp.ones((CHUNK, CHUNK), dtype=bool))
    w = jnp.where(causal[None], w_s, jnp.zeros_like(w_s))
    s = jnp.einsum('hts,bnshd->bnthd', w, v) + b_s.T[None, None, :, :, None]
    return u * s.reshape(B, S, H, D)


def retention_chunkwise(q, k, v):
    B, S, H, D = q.shape
    nc = S // CHUNK
    log_g = jnp.log(1.0 - 2.0 ** (-5.0 - jnp.arange(H, dtype=jnp.float32)))
    idx = jnp.arange(CHUNK, dtype=jnp.float32)
    diff = idx[:, None] - idx[None, :]
    decay = jnp.where(diff >= 0, jnp.exp(log_g[:, None, None] * jnp.maximum(diff, 0.0)), 0.0)
    q_decay = jnp.exp(log_g[:, None] * (idx + 1.0))[None, :, :, None]
    k_decay = jnp.exp(log_g[:, None] * (CHUNK - 1.0 - idx))[None, :, :, None]
    chunk_decay = jnp.exp(log_g * CHUNK)[None, :, None, None]

    def to_chunks(t):
        return t.reshape(B, nc, CHUNK, H, D).transpose(1, 0, 3, 2, 4)

    def step(R, inp):
        q_i, k_i, v_i = inp
        inner = jnp.einsum('bhtd,bhsd->bhts', q_i, k_i) * decay
        o = jnp.einsum('bhts,bhse->bhte', inner, v_i)
        o = o + jnp.einsum('bhtd,bhde->bhte', q_i, R) * q_decay
        R = R * chunk_decay + jnp.einsum('bhsd,bhse->bhde', k_i * k_decay, v_i)
        return R, o

    R0 = jnp.zeros((B, H, D, D), dtype=jnp.float32)
    _, o = lax.scan(step, R0, (to_chunks(q), to_chunks(k), to_chunks(v)))
    return o.transpose(1, 0, 3, 2, 4).reshape(B, S, H, D)


def forgetting_attention(q, k, v, log_f):
    B, S, H, D = q.shape
    nb = S // CHUNK
    scale = D ** -0.5
    c = jnp.cumsum(log_f, axis=1)
    cT = c.transpose(0, 2, 1)
    qb = q.reshape(B, nb, CHUNK, H, D).transpose(1, 0, 2, 3, 4)
    cb = cT.reshape(B, H, nb, CHUNK).transpose(2, 0, 1, 3)
    starts = jnp.arange(nb, dtype=jnp.int32) * CHUNK
    key_pos = jnp.arange(S, dtype=jnp.int32)

    def one_block(args):
        q_i, c_i, start = args
        s = jnp.einsum('bqhd,bkhd->bhqk', q_i, k).astype(jnp.float32) * scale
        s = s + c_i[..., None] - cT[:, :, None, :]
        q_pos = start + jnp.arange(CHUNK, dtype=jnp.int32)
        mask = key_pos[None, :] <= q_pos[:, None]
        s = jnp.where(mask[None, None], s, -jnp.inf)
        prob = jax.nn.softmax(s, axis=-1)
        return jnp.einsum('bhqk,bkhd->bqhd', prob.astype(v.dtype), v)

    out = lax.map(one_block, (qb, cb, starts))
    return out.transpose(1, 0, 2, 3, 4).reshape(B, S, H, D)


def conv_gated_mlp(h, w_gate, w_up, conv_w, conv_b, w_down):
    S = h.shape[1]
    g = h @ w_gate
    gp = jnp.pad(g, ((0, 0), (CONV_WIDTH - 1, 0), (0, 0)))
    conv = conv_b + gp[:, 0:S] * conv_w[0]
    for j in range(1, CONV_WIDTH):
        conv = conv + gp[:, j:j + S] * conv_w[j]
    act = jax.nn.gelu(conv, approximate=True) * (h @ w_up)
    return act @ w_down


def setup_inputs(seed: int = 0) -> dict:
    key = jax.random.key(seed)
    ks = jax.random.split(key, 24)
    f32 = jnp.float32

    def nrm(k, shape, scale):
        return jax.random.normal(k, shape, f32) * scale

    def gain(k, shape):
        return 1.0 + 0.05 * jax.random.normal(k, shape, f32)

    return {
        "x": nrm(ks[0], (BATCH, SEQ, D_MODEL), 1.0),
        "p": nrm(ks[1], (DEPTH, BATCH, SEQ, D_PLE), 1.0),
        "positions": jnp.broadcast_to(jnp.arange(SEQ, dtype=jnp.int32)[None, :], (BATCH, SEQ)),
        "mix_pre_g": gain(ks[2], (DEPTH, D_MODEL)),
        "w_in": nrm(ks[3], (DEPTH, D_MODEL, N_IN), D_MODEL ** -0.5),
        "sgu_v_g": gain(ks[4], (DEPTH, N_HEADS_SGU, HEAD_DIM)),
        "sgu_w": nrm(ks[5], (DEPTH, N_HEADS_SGU, CHUNK, CHUNK), CHUNK ** -0.5),
        "sgu_b": 1.0 + 0.01 * jax.random.normal(ks[6], (DEPTH, N_HEADS_SGU, CHUNK), f32),
        "fox_b_f": 3.0 + 0.5 * jax.random.normal(ks[7], (DEPTH, N_HEADS_FOX), f32),
        "w_o": nrm(ks[8], (DEPTH, D_MIX, D_MODEL), D_MIX ** -0.5),
        "mix_post_g": gain(ks[9], (DEPTH, D_MODEL)),
        "ffn_pre_g": gain(ks[10], (DEPTH, D_MODEL)),
        "w_gate": nrm(ks[11], (DEPTH, D_MODEL, D_FF), D_MODEL ** -0.5),
        "w_up": nrm(ks[12], (DEPTH, D_MODEL, D_FF), D_MODEL ** -0.5),
        "conv_w": nrm(ks[13], (DEPTH, CONV_WIDTH, D_FF), CONV_WIDTH ** -0.5),
        "conv_b": nrm(ks[14], (DEPTH, D_FF), 0.01),
        "w_down": nrm(ks[15], (DEPTH, D_FF, D_MODEL), D_FF ** -0.5),
        "ffn_post_g": gain(ks[16], (DEPTH, D_MODEL)),
        "ple_pre_g": gain(ks[17], (DEPTH, D_MODEL)),
        "w_ple_gate": nrm(ks[18], (DEPTH, D_MODEL, D_MODEL), D_MODEL ** -0.5),
        "w_ple_proj": nrm(ks[19], (DEPTH, D_PLE, D_MODEL), D_PLE ** -0.5),
        "ple_post_g": gain(ks[20], (DEPTH, D_MODEL)),
    }


def reference(x, p, positions, mix_pre_g, w_in, sgu_v_g, sgu_w, sgu_b, fox_b_f, w_o,
              mix_post_g, ffn_pre_g, w_gate, w_up, conv_w, conv_b, w_down, ffn_post_g,
              ple_pre_g, w_ple_gate, w_ple_proj, ple_post_g):
    B, S, _ = x.shape
    dt = x.dtype
    for i in range(DEPTH):
        h = rmsnorm(x, mix_pre_g[i])
        z = h @ w_in[i]
        (a_u, a_v, r_q, r_k, r_v, r_g,
         f_q, f_k, f_v, f_f) = jnp.split(z, IN_SPLIT_POINTS, axis=-1)

        a_u = jax.nn.gelu(a_u, approximate=True).reshape(B, S, N_HEADS_SGU, HEAD_DIM)
        a_v = jax.nn.gelu(a_v, approximate=True).reshape(B, S, N_HEADS_SGU, HEAD_DIM)
        out_a = spatial_gating(a_u, a_v, sgu_v_g[i], sgu_w[i], sgu_b[i]).reshape(B, S, W_SGU)

        rq = rotary(r_q.astype(jnp.float32).reshape(B, S, N_HEADS_RET, HEAD_DIM), positions)
        rk = rotary(r_k.astype(jnp.float32).reshape(B, S, N_HEADS_RET, HEAD_DIM), positions) * (HEAD_DIM ** -0.5)
        rv = r_v.astype(jnp.float32).reshape(B, S, N_HEADS_RET, HEAD_DIM)
        ro = retention_chunkwise(rq, rk, rv)
        ro = ro * lax.rsqrt(jnp.mean(ro * ro, axis=-1, keepdims=True) + RMS_EPS)
        out_b = (jax.nn.silu(r_g.astype(jnp.float32)) * ro.reshape(B, S, W_RET)).astype(dt)

        log_f = jax.nn.log_sigmoid(f_f.astype(jnp.float32) + fox_b_f[i].astype(jnp.float32))
        out_c = forgetting_attention(
            f_q.reshape(B, S, N_HEADS_FOX, HEAD_DIM),
            f_k.reshape(B, S, N_HEADS_FOX, HEAD_DIM),
            f_v.reshape(B, S, N_HEADS_FOX, HEAD_DIM),
            log_f).reshape(B, S, W_FOX).astype(dt)

        mix = jnp.concatenate([out_a.astype(dt), out_b, out_c], axis=-1) @ w_o[i]
        x = x + rmsnorm(mix, mix_post_g[i])

        h = rmsnorm(x, ffn_pre_g[i])
        f = conv_gated_mlp(h, w_gate[i], w_up[i], conv_w[i], conv_b[i], w_down[i])
        x = x + rmsnorm(f, ffn_post_g[i])

        gate = jax.nn.sigmoid(rmsnorm(x, ple_pre_g[i]) @ w_ple_gate[i])
        e = p[i] @ w_ple_proj[i]
        x = x + rmsnorm(e * gate, ple_post_g[i])
    return x
```

```python
import functools

import jax
import jax.numpy as jnp
from jax import lax
from jax.experimental import pallas as pl
from jax.experimental.pallas import tpu as pltpu

D_MODEL = 1024
D_PLE = 256
HEAD_DIM = 64
N_HEADS_SGU = 4
N_HEADS_RET = 6
N_HEADS_FOX = 6
W_SGU = N_HEADS_SGU * HEAD_DIM
W_RET = N_HEADS_RET * HEAD_DIM
W_FOX = N_HEADS_FOX * HEAD_DIM
CHUNK = 128
D_FF = 4 * D_MODEL
ROPE_BASE = 10000.0
RMS_EPS = 1e-6
LANES = 128
N_PAIRS = W_RET // LANES
N_IN_PAD = 2 * W_SGU + 4 * W_RET + 3 * W_FOX + LANES
MASK_VALUE = -1e30
VMEM_LIMIT = 56 * 1024 * 1024

f32 = jnp.float32
bf16 = jnp.bfloat16


def _params(*sem):
    return pltpu.CompilerParams(dimension_semantics=sem, vmem_limit_bytes=VMEM_LIMIT)


def _resident(shape):
    zeros = (0,) * len(shape)
    return pl.BlockSpec(shape, lambda *_: zeros, pipeline_mode=pl.Buffered(1))


def _rms(xf, g):
    y = xf * lax.rsqrt(jnp.mean(xf * xf, axis=-1, keepdims=True) + RMS_EPS)
    return y * g


def _dot(a, b):
    return jnp.dot(a, b, preferred_element_type=f32)


def _dot_nt(a, b):
    return lax.dot_general(a, b, (((1,), (1,)), ((), ())), preferred_element_type=f32)


def _dot_tn(a, b):
    return lax.dot_general(a, b, (((0,), (0,)), ((), ())), preferred_element_type=f32)


def _split_bf16(x, terms):
    parts = []
    r = x
    for _ in range(terms):
        p = r.astype(bf16)
        parts.append(p)
        r = r - p.astype(f32)
    return parts


def _group_mean(y, width):
    r = lax.broadcasted_iota(jnp.int32, (width, width), 0) // HEAD_DIM
    c = lax.broadcasted_iota(jnp.int32, (width, width), 1) // HEAD_DIM
    gm = jnp.where(r == c, 1.0 / HEAD_DIM, 0.0).astype(bf16)
    return sum(_dot(p, gm) for p in _split_bf16(y, 2))


def _gelu(x):
    return jax.nn.gelu(x, approximate=True)


def _rope_kernel(pos_ref, invf_ref, cos_ref, sin_ref, nsin_ref):
    ang = pos_ref[...].astype(f32) * invf_ref[...]
    s = jnp.sin(ang)
    cos_ref[...] = jnp.cos(ang)
    sin_ref[...] = s
    nsin_ref[...] = -s


def _rope_tables(positions):
    t = positions.size
    half = HEAD_DIM // 2
    per_row = LANES // half
    rows = t // per_row
    pos = jnp.repeat(positions.reshape(rows, per_row), half, axis=1)
    inv_freq = ROPE_BASE ** (-jnp.arange(half, dtype=f32) / half)
    invf = jnp.tile(inv_freq, per_row)[None, :]
    tr = min(rows, 1024)
    spec = pl.BlockSpec((tr, LANES), lambda i: (i, 0))
    cos, sin, nsin = pl.pallas_call(
        _rope_kernel,
        grid=(rows // tr,),
        in_specs=[spec, _resident((1, LANES))],
        out_specs=[spec, spec, spec],
        out_shape=[jax.ShapeDtypeStruct((rows, LANES), f32)] * 3,
        compiler_params=_params("parallel"),
        name="rope_tables",
    )(pos, invf)
    cos, sin, nsin = (a.reshape(t, half) for a in (cos, sin, nsin))
    return jnp.tile(cos, (1, per_row)), jnp.concatenate([nsin, sin] * (per_row // 2), axis=1)


def _proj_in_kernel(x_ref, g_ref, w_ref, cos_ref, sin_ref, sguw_ref, sgub_ref, vg_ref, fb_ref,
                    oa_ref, rq_ref, rk_ref, rv_ref, rg_ref, fq_ref, fk_ref, fv_ref, ct_ref,
                    carry_ref, *, tm, blocks_per_seq):
    i = pl.program_id(0)
    n_chunks = tm // CHUNK
    h = _rms(x_ref[...], g_ref[...]).astype(bf16)

    def proj(lo, width):
        return _dot(h, w_ref[:, lo:lo + width])

    sq_row = lax.broadcasted_iota(jnp.int32, (CHUNK, CHUNK), 0)
    sq_col = lax.broadcasted_iota(jnp.int32, (CHUNK, CHUNK), 1)
    tril = sq_row >= sq_col
    first_head = sq_col < HEAD_DIM

    u = _gelu(proj(0, W_SGU))
    v = _gelu(proj(W_SGU, W_SGU))
    v = v * lax.rsqrt(_group_mean(v * v, W_SGU) + RMS_EPS) * vg_ref[...]
    for p in range(W_SGU // LANES):
        cols = slice(p * LANES, (p + 1) * LANES)
        w_pair = jnp.concatenate(
            [jnp.where(tril, sguw_ref[2 * p], 0.0), jnp.where(tril, sguw_ref[2 * p + 1], 0.0)],
            axis=0).astype(bf16)
        for c in range(n_chunks):
            rows = slice(c * CHUNK, (c + 1) * CHUNK)
            r = _dot(w_pair, v[rows, cols].astype(bf16))
            s = jnp.where(first_head, r[:CHUNK], r[CHUNK:]) + sgub_ref[:, cols]
            oa_ref[rows, cols] = (u[rows, cols] * s).astype(bf16)

    cosv = cos_ref[...]
    sinv = sin_ref[...]
    lane = lax.broadcasted_iota(jnp.int32, (tm, LANES), 1)
    first_half = (lane % HEAD_DIM) < (HEAD_DIM // 2)

    def rotary(z):
        outs = []
        for s in range(N_PAIRS):
            t = z[:, s * LANES:(s + 1) * LANES]
            partner = jnp.where(first_half, pltpu.roll(t, LANES - HEAD_DIM // 2, 1),
                                pltpu.roll(t, HEAD_DIM // 2, 1))
            outs.append(t * cosv + partner * sinv)
        return jnp.concatenate(outs, axis=1)

    base = 2 * W_SGU
    rq_ref[...] = rotary(proj(base, W_RET)).astype(bf16)
    rk_ref[...] = (rotary(proj(base + W_RET, W_RET)) * (HEAD_DIM ** -0.5)).astype(bf16)
    rv_ref[...] = proj(base + 2 * W_RET, W_RET).astype(bf16)
    rg_ref[...] = proj(base + 3 * W_RET, W_RET).astype(bf16)

    base += 4 * W_RET
    fq_ref[...] = (proj(base, W_FOX) * (HEAD_DIM ** -0.5)).astype(bf16)
    fk_ref[...] = proj(base + W_FOX, W_FOX).astype(bf16)
    fv_ref[...] = proj(base + 2 * W_FOX, W_FOX).astype(bf16)

    ff = proj(base + 3 * W_FOX, LANES) + fb_ref[...]
    logf = -(jnp.maximum(-ff, 0.0) + jnp.log1p(jnp.exp(-jnp.abs(ff))))

    @pl.when(i % blocks_per_seq == 0)
    def _():
        carry_ref[...] = jnp.zeros_like(carry_ref)

    ltri = jnp.where(tril, 1.0, 0.0).astype(bf16)
    carry = carry_ref[...]
    for c in range(n_chunks):
        within = sum(_dot(ltri, part) for part in _split_bf16(logf[c * CHUNK:(c + 1) * CHUNK, :], 3))
        cum = within + carry
        carry = cum[CHUNK - 1:CHUNK, :]
        ct_ref[:, c * CHUNK:(c + 1) * CHUNK] = cum.T[0:8, :]
    carry_ref[...] = carry


def _proj_in(x2d, g, w_pad, cos4, sin4, sgu_w, sgu_b_full, sgu_vg, fox_b, *, seq, tm):
    t = x2d.shape[0]
    row = lambda width: pl.BlockSpec((tm, width), lambda i: (i, 0))
    out_widths = [W_SGU] + [W_RET] * 4 + [W_FOX] * 3
    kern = functools.partial(_proj_in_kernel, tm=tm, blocks_per_seq=seq // tm)
    return pl.pallas_call(
        kern,
        grid=(t // tm,),
        in_specs=[row(D_MODEL), _resident((1, D_MODEL)), _resident((D_MODEL, N_IN_PAD)),
                  row(LANES), row(LANES),
                  _resident((N_HEADS_SGU, CHUNK, CHUNK)), _resident((CHUNK, W_SGU)),
                  _resident((1, W_SGU)), _resident((1, LANES))],
        out_specs=[row(w) for w in out_widths] + [pl.BlockSpec((8, tm), lambda i: (0, i))],
        out_shape=[jax.ShapeDtypeStruct((t, w), bf16) for w in out_widths]
        + [jax.ShapeDtypeStruct((8, t), f32)],
        scratch_shapes=[pltpu.VMEM((1, LANES), f32)],
        compiler_params=_params("arbitrary"),
        name="proj_in",
    )(x2d, g, w_pad, cos4, sin4, sgu_w, sgu_b_full, sgu_vg, fox_b)


def _retention_kernel(q_ref, k_ref, v_ref, g_ref, dec_ref, qd_ref, kd_ref, cd_ref, o_ref, state_ref,
                      *, tm, blocks_per_seq):
    i = pl.program_id(0)

    @pl.when(i % blocks_per_seq == 0)
    def _():
        state_ref[...] = jnp.zeros_like(state_ref)

    row = lax.broadcasted_iota(jnp.int32, (CHUNK, LANES), 0)
    lane = lax.broadcasted_iota(jnp.int32, (CHUNK, LANES), 1)
    first_head = lane < HEAD_DIM
    same_head = (row < HEAD_DIM) == first_head
    zero = jnp.zeros((CHUNK, LANES), bf16)
    for c in range(tm // CHUNK):
        rows = slice(c * CHUNK, (c + 1) * CHUNK)
        for p in range(N_PAIRS):
            cols = slice(p * LANES, (p + 1) * LANES)
            q = q_ref[rows, cols]
            k = k_ref[rows, cols]
            v = v_ref[rows, cols]
            s0 = _dot_nt(jnp.where(first_head, q, zero), k) * dec_ref[2 * p]
            s1 = _dot_nt(jnp.where(first_head, zero, q), k) * dec_ref[2 * p + 1]
            o0 = _dot(s0.astype(bf16), v)
            o1 = _dot(s1.astype(bf16), v)
            state = state_ref[p]
            o = jnp.where(first_head, o0, o1) + _dot(q, state.astype(bf16)) * qd_ref[p]
            k_dec = (k.astype(f32) * kd_ref[p]).astype(bf16)
            state_ref[p] = state * cd_ref[p] + jnp.where(same_head, _dot_tn(k_dec, v), 0.0)
            ro = o * lax.rsqrt(_group_mean(o * o, LANES) + RMS_EPS)
            gate = g_ref[rows, cols].astype(f32)
            o_ref[rows, cols] = (gate * jax.nn.sigmoid(gate) * ro).astype(bf16)


def _retention_tables():
    log_g = jnp.log(1.0 - 2.0 ** (-5.0 - jnp.arange(N_HEADS_RET, dtype=f32)))
    idx = jnp.arange(CHUNK, dtype=f32)
    diff = idx[:, None] - idx[None, :]
    decay = jnp.where(diff >= 0, jnp.exp(log_g[:, None, None] * jnp.maximum(diff, 0.0)), 0.0)
    lane_g = jnp.repeat(log_g, HEAD_DIM).reshape(N_PAIRS, 1, LANES)
    q_decay = jnp.exp(lane_g * (idx + 1.0)[None, :, None])
    k_decay = jnp.exp(lane_g * (CHUNK - 1.0 - idx)[None, :, None])
    chunk_decay = jnp.broadcast_to(jnp.exp(lane_g * CHUNK).reshape(N_PAIRS, LANES, 1), (N_PAIRS, LANES, LANES))
    return decay, q_decay, k_decay, chunk_decay


def _retention(rq, rk, rv, rg, *, seq, tm):
    t = rq.shape[0]
    decay, q_decay, k_decay, chunk_decay = _retention_tables()
    row = pl.BlockSpec((tm, W_RET), lambda i: (i, 0))
    kern = functools.partial(_retention_kernel, tm=tm, blocks_per_seq=seq // tm)
    return pl.pallas_call(
        kern,
        grid=(t // tm,),
        in_specs=[row, row, row, row, _resident(decay.shape), _resident(q_decay.shape),
                  _resident(k_decay.shape), _resident(chunk_decay.shape)],
        out_specs=row,
        out_shape=jax.ShapeDtypeStruct((t, W_RET), bf16),
        scratch_shapes=[pltpu.VMEM((N_PAIRS, LANES, LANES), f32)],
        compiler_params=_params("arbitrary"),
        name="retention",
    )(rq, rk, rv, rg, decay, q_decay, k_decay, chunk_decay)


def _fox_kernel(q_ref, k_ref, v_ref, c_ref, o_ref, m_ref, l_ref, acc_ref, *, tb):
    i = pl.program_id(2)
    q = q_ref[...]
    lane = lax.broadcasted_iota(jnp.int32, (tb, LANES), 1)
    first_head = lane < HEAD_DIM
    zero = jnp.zeros((tb, LANES), bf16)
    q_heads = (jnp.where(first_head, q, zero), jnp.where(first_head, zero, q))
    m_ref[...] = jnp.full_like(m_ref, MASK_VALUE)
    l_ref[...] = jnp.zeros_like(l_ref)
    acc_ref[...] = jnp.zeros_like(acc_ref)

    def step(j, diagonal):
        off = pl.multiple_of(j * tb, tb)
        k = k_ref[pl.ds(off, tb), :]
        v = v_ref[pl.ds(off, tb), :]
        for a in range(2):
            s = _dot_nt(q_heads[a], k) - c_ref[a:a + 1, pl.ds(off, tb)]
            if diagonal:
                r = lax.broadcasted_iota(jnp.int32, (tb, tb), 0)
                cidx = lax.broadcasted_iota(jnp.int32, (tb, tb), 1)
                s = jnp.where(cidx <= r, s, MASK_VALUE)
            m_old = m_ref[a]
            m_new = jnp.maximum(m_old, jnp.max(s, axis=1, keepdims=True))
            p = jnp.exp(s - m_new)
            alpha = jnp.exp(m_old - m_new)
            l_ref[a] = alpha * l_ref[a] + jnp.sum(p, axis=1, keepdims=True)
            acc_ref[a] = alpha * acc_ref[a] + _dot(p.astype(bf16), v)
            m_ref[a] = m_new

    def body(j, carry):
        step(j, False)
        return carry

    lax.fori_loop(0, i, body, 0)
    step(i, True)
    o = jnp.where(first_head, acc_ref[0] / l_ref[0], acc_ref[1] / l_ref[1])
    o_ref[...] = o.astype(bf16)


def _fox(fq, fk, fv, c_pairs, *, batch, seq, tb):
    t = fq.shape[0]
    nq = seq // tb
    kv_spec = pl.BlockSpec((seq, LANES), lambda b, p, i: (b, p))
    q_spec = pl.BlockSpec((tb, LANES), lambda b, p, i: (b * nq + i, p))
    return pl.pallas_call(
        functools.partial(_fox_kernel, tb=tb),
        grid=(batch, N_PAIRS, nq),
        in_specs=[q_spec, kv_spec, kv_spec,
                  pl.BlockSpec((None, None, 2, seq), lambda b, p, i: (b, p, 0, 0))],
        out_specs=q_spec,
        out_shape=jax.ShapeDtypeStruct((t, W_FOX), bf16),
        scratch_shapes=[pltpu.VMEM((2, tb, 1), f32), pltpu.VMEM((2, tb, 1), f32),
                        pltpu.VMEM((2, tb, LANES), f32)],
        compiler_params=_params("parallel", "parallel", "arbitrary"),
        name="fox_attention",
    )(fq, fk, fv, c_pairs)


def _out_proj_kernel(a_ref, b_ref, c_ref, x_ref, w_ref, g_ref, o_ref):
    cat = jnp.concatenate([a_ref[...], b_ref[...], c_ref[...]], axis=1)
    o_ref[...] = x_ref[...] + _rms(_dot(cat, w_ref[...]), g_ref[...])


def _out_proj(out_a, out_b, out_c, x2d, w_o, g, *, tm):
    t = x2d.shape[0]
    row = lambda width: pl.BlockSpec((tm, width), lambda i: (i, 0))
    return pl.pallas_call(
        _out_proj_kernel,
        grid=(t // tm,),
        in_specs=[row(W_SGU), row(W_RET), row(W_FOX), row(D_MODEL),
                  _resident((D_MODEL, D_MODEL)), _resident((1, D_MODEL))],
        out_specs=row(D_MODEL),
        out_shape=jax.ShapeDtypeStruct((t, D_MODEL), f32),
        compiler_params=_params("parallel"),
        name="out_proj",
    )(out_a, out_b, out_c, x2d, w_o, g)


def _ffn_kernel(x_ref, gpre_ref, wg_ref, wu_ref, cw_ref, cb_ref, wd_ref, gpost_ref, o_ref, halo_ref,
                *, tm, fc, blocks_per_seq):
    i = pl.program_id(0)

    @pl.when(i % blocks_per_seq == 0)
    def _():
        halo_ref[...] = jnp.zeros_like(halo_ref)

    xf = x_ref[...]
    h = _rms(xf, gpre_ref[...]).astype(bf16)
    row8 = lax.broadcasted_iota(jnp.int32, (8, fc), 0)
    acc = jnp.zeros((tm, D_MODEL), f32)
    for c in range(D_FF // fc):
        cols = slice(c * fc, (c + 1) * fc)
        g = _dot(h, wg_ref[:, cols])
        up = _dot(h, wu_ref[:, cols])
        prev = halo_ref[:, cols]
        halo_ref[:, cols] = g[tm - 8:, :]
        g1 = pltpu.roll(g, 1, 0)
        g2 = pltpu.roll(g, 2, 0)
        g1 = jnp.concatenate([jnp.where(row8 < 1, pltpu.roll(prev, 1, 0), g1[:8]), g1[8:]], axis=0)
        g2 = jnp.concatenate([jnp.where(row8 < 2, pltpu.roll(prev, 2, 0), g2[:8]), g2[8:]], axis=0)
        conv = cb_ref[:, cols] + g2 * cw_ref[0:1, cols]
        conv = conv + g1 * cw_ref[1:2, cols]
        conv = conv + g * cw_ref[2:3, cols]
        act = _gelu(conv) * up
        acc = acc + _dot(act.astype(bf16), wd_ref[cols, :])
    o_ref[...] = xf + _rms(acc, gpost_ref[...])


def _ffn(x2d, gpre, w_gate, w_up, conv_w, conv_b, w_down, gpost, *, seq, tm, fc):
    t = x2d.shape[0]
    row = pl.BlockSpec((tm, D_MODEL), lambda i: (i, 0))
    kern = functools.partial(_ffn_kernel, tm=tm, fc=fc, blocks_per_seq=seq // tm)
    return pl.pallas_call(
        kern,
        grid=(t // tm,),
        in_specs=[row, _resident((1, D_MODEL)), _resident((D_MODEL, D_FF)), _resident((D_MODEL, D_FF)),
                  _resident(conv_w.shape), _resident((1, D_FF)), _resident((D_FF, D_MODEL)),
                  _resident((1, D_MODEL))],
        out_specs=row,
        out_shape=jax.ShapeDtypeStruct((t, D_MODEL), f32),
        scratch_shapes=[pltpu.VMEM((8, D_FF), f32)],
        compiler_params=_params("arbitrary"),
        name="ffn",
    )(x2d, gpre, w_gate, w_up, conv_w, conv_b, w_down, gpost)


def _ple_kernel(x_ref, p_ref, gpre_ref, wgate_ref, wproj_ref, gpost_ref, o_ref):
    xf = x_ref[...]
    gate = jax.nn.sigmoid(_dot(_rms(xf, gpre_ref[...]).astype(bf16), wgate_ref[...]))
    e = _dot(p_ref[...].astype(bf16), wproj_ref[...])
    o_ref[...] = xf + _rms(e * gate, gpost_ref[...])


def _ple(x2d, p2d, gpre, w_gate, w_proj, gpost, *, tm):
    t = x2d.shape[0]
    row = lambda width: pl.BlockSpec((tm, width), lambda i: (i, 0))
    return pl.pallas_call(
        _ple_kernel,
        grid=(t // tm,),
        in_specs=[row(D_MODEL), row(D_PLE), _resident((1, D_MODEL)), _resident((D_MODEL, D_MODEL)),
                  _resident((D_PLE, D_MODEL)), _resident((1, D_MODEL))],
        out_specs=row(D_MODEL),
        out_shape=jax.ShapeDtypeStruct((t, D_MODEL), f32),
        compiler_params=_params("parallel"),
        name="ple",
    )(x2d, p2d, gpre, w_gate, w_proj, gpost)


def _largest_tile(seq, cap):
    tile = cap
    while seq % tile:
        tile //= 2
    return tile


def kernel(x, p, positions, mix_pre_g, w_in, sgu_v_g, sgu_w, sgu_b, fox_b_f, w_o, mix_post_g, ffn_pre_g,
           w_gate, w_up, conv_w, conv_b, w_down, ffn_post_g, ple_pre_g, w_ple_gate, w_ple_proj, ple_post_g):
    batch, seq, _ = x.shape
    depth = w_in.shape[0]
    t = batch * seq
    assert seq % CHUNK == 0
    tm = _largest_tile(seq, 512)
    tb = _largest_tile(seq, 512)
    fc = 1024

    cos4, sin4 = _rope_tables(positions)
    x2d = x.reshape(t, D_MODEL)
    for i in range(depth):
        w_pad = jnp.pad(w_in[i].astype(bf16), ((0, 0), (0, N_IN_PAD - w_in.shape[2])))
        sgu_b_full = jnp.repeat(sgu_b[i].T, HEAD_DIM, axis=1)
        fox_b = jnp.pad(fox_b_f[i], (0, LANES - N_HEADS_FOX))[None, :]
        out_a, rq, rk, rv, rg, fq, fk, fv, c_t = _proj_in(
            x2d, mix_pre_g[i][None, :], w_pad, cos4, sin4, sgu_w[i], sgu_b_full,
            sgu_v_g[i].reshape(1, W_SGU), fox_b, seq=seq, tm=tm)
        out_b = _retention(rq, rk, rv, rg, seq=seq, tm=tm)
        c_pairs = c_t[:N_HEADS_FOX].reshape(N_PAIRS, 2, batch, seq).transpose(2, 0, 1, 3)
        out_c = _fox(fq, fk, fv, c_pairs, batch=batch, seq=seq, tb=tb)
        x2d = _out_proj(out_a, out_b, out_c, x2d, w_o[i].astype(bf16), mix_post_g[i][None, :], tm=tm)
        x2d = _ffn(x2d, ffn_pre_g[i][None, :], w_gate[i].astype(bf16), w_up[i].astype(bf16), conv_w[i],
                   conv_b[i][None, :], w_down[i].astype(bf16), ffn_post_g[i][None, :], seq=seq, tm=tm, fc=fc)
        x2d = _ple(x2d, p[i].reshape(t, D_PLE), ple_pre_g[i][None, :], w_ple_gate[i].astype(bf16),
                   w_ple_proj[i].astype(bf16), ple_post_g[i][None, :], tm=tm)
    return x2d.reshape(batch, seq, D_MODEL)
```

```python
import functools

import numpy as np
import jax
import jax.numpy as jnp
from jax import lax
from jax.experimental import pallas as pl
from jax.experimental.pallas import tpu as pltpu

D_MODEL = 1024
D_PLE = 256
HEAD_DIM = 64
N_HEADS_SGU = 4
N_HEADS_RET = 6
N_HEADS_FOX = 6
W_SGU = N_HEADS_SGU * HEAD_DIM
W_RET = N_HEADS_RET * HEAD_DIM
W_FOX = N_HEADS_FOX * HEAD_DIM
CHUNK = 128
D_FF = 4 * D_MODEL
ROPE_BASE = 10000.0
RMS_EPS = 1e-6
LANES = 128
N_PAIRS = W_RET // LANES
N_IN_PAD = 2 * W_SGU + 4 * W_RET + 3 * W_FOX + LANES
MASK_VALUE = -1e30
LOG2E = 1.4426950408889634
C_TERMS = 3
VMEM_LIMIT = 56 * 1024 * 1024

f32 = jnp.float32
bf16 = jnp.bfloat16


def _params(*sem):
    return pltpu.CompilerParams(dimension_semantics=sem, vmem_limit_bytes=VMEM_LIMIT)


def _resident(shape):
    zeros = (0,) * len(shape)
    return pl.BlockSpec(shape, lambda *_: zeros, pipeline_mode=pl.Buffered(1))


def _rms(xf, g):
    y = xf * lax.rsqrt(jnp.mean(xf * xf, axis=-1, keepdims=True) + RMS_EPS)
    return y * g


def _dot(a, b):
    return jnp.dot(a, b, preferred_element_type=f32)


def _dot_nt(a, b):
    return lax.dot_general(a, b, (((1,), (1,)), ((), ())), preferred_element_type=f32)


def _dot_tn(a, b):
    return lax.dot_general(a, b, (((0,), (0,)), ((), ())), preferred_element_type=f32)


def _split_bf16(x, terms):
    parts = []
    r = x
    for _ in range(terms):
        p = r.astype(bf16)
        parts.append(p)
        r = r - p.astype(f32)
    return parts


def _group_mean(y, width):
    r = lax.broadcasted_iota(jnp.int32, (width, width), 0) // HEAD_DIM
    c = lax.broadcasted_iota(jnp.int32, (width, width), 1) // HEAD_DIM
    gm = jnp.where(r == c, 1.0 / HEAD_DIM, 0.0).astype(bf16)
    return sum(_dot(p, gm) for p in _split_bf16(y, 2))


def _gelu(x):
    return jax.nn.gelu(x, approximate=True)


def _rope_kernel(pos_ref, invf_ref, cos_ref, sin_ref, nsin_ref):
    ang = pos_ref[...].astype(f32) * invf_ref[...]
    s = jnp.sin(ang)
    cos_ref[...] = jnp.cos(ang)
    sin_ref[...] = s
    nsin_ref[...] = -s


def _rope_tables(positions):
    t = positions.size
    half = HEAD_DIM // 2
    per_row = LANES // half
    rows = t // per_row
    pos = jnp.repeat(positions.reshape(rows, per_row), half, axis=1)
    inv_freq = ROPE_BASE ** (-jnp.arange(half, dtype=f32) / half)
    invf = jnp.tile(inv_freq, per_row)[None, :]
    tr = min(rows, 1024)
    spec = pl.BlockSpec((tr, LANES), lambda i: (i, 0))
    cos, sin, nsin = pl.pallas_call(
        _rope_kernel,
        grid=(rows // tr,),
        in_specs=[spec, _resident((1, LANES))],
        out_specs=[spec, spec, spec],
        out_shape=[jax.ShapeDtypeStruct((rows, LANES), f32)] * 3,
        compiler_params=_params("parallel"),
        name="rope_tables",
    )(pos, invf)
    cos, sin, nsin = (a.reshape(t, half) for a in (cos, sin, nsin))
    return jnp.tile(cos, (1, per_row)), jnp.concatenate([nsin, sin] * (per_row // 2), axis=1)


def _proj_in_kernel(x_ref, g_ref, w_ref, cos_ref, sin_ref, sguw_ref, sgub_ref, vg_ref, fb_ref, place_ref,
                    oa_ref, rq_ref, rk_ref, rv_ref, rg_ref, fq_ref, fk_ref, fvt_ref,
                    carry_ref, fv_ref, *, tm, blocks_per_seq):
    i = pl.program_id(0)
    n_chunks = tm // CHUNK
    h = _rms(x_ref[...], g_ref[...]).astype(bf16)

    def proj(lo, width):
        return _dot(h, w_ref[:, lo:lo + width])

    sq_row = lax.broadcasted_iota(jnp.int32, (CHUNK, CHUNK), 0)
    sq_col = lax.broadcasted_iota(jnp.int32, (CHUNK, CHUNK), 1)
    tril = sq_row >= sq_col
    first_head = sq_col < HEAD_DIM

    u = _gelu(proj(0, W_SGU))
    v = _gelu(proj(W_SGU, W_SGU))
    v = v * lax.rsqrt(_group_mean(v * v, W_SGU) + RMS_EPS) * vg_ref[...]
    for p in range(W_SGU // LANES):
        cols = slice(p * LANES, (p + 1) * LANES)
        w_pair = jnp.concatenate(
            [jnp.where(tril, sguw_ref[2 * p], 0.0), jnp.where(tril, sguw_ref[2 * p + 1], 0.0)],
            axis=0).astype(bf16)
        for c in range(n_chunks):
            rows = slice(c * CHUNK, (c + 1) * CHUNK)
            r = _dot(w_pair, v[rows, cols].astype(bf16))
            s = jnp.where(first_head, r[:CHUNK], r[CHUNK:]) + sgub_ref[:, cols]
            oa_ref[rows, cols] = (u[rows, cols] * s).astype(bf16)

    cosv = cos_ref[...]
    sinv = sin_ref[...]
    lane = lax.broadcasted_iota(jnp.int32, (tm, LANES), 1)
    first_half = (lane % HEAD_DIM) < (HEAD_DIM // 2)

    def rotary(z):
        outs = []
        for s in range(N_PAIRS):
            t = z[:, s * LANES:(s + 1) * LANES]
            partner = jnp.where(first_half, pltpu.roll(t, LANES - HEAD_DIM // 2, 1),
                                pltpu.roll(t, HEAD_DIM // 2, 1))
            outs.append(t * cosv + partner * sinv)
        return jnp.concatenate(outs, axis=1)

    base = 2 * W_SGU
    rq_ref[...] = rotary(proj(base, W_RET)).astype(bf16)
    rk_ref[...] = (rotary(proj(base + W_RET, W_RET)) * (HEAD_DIM ** -0.5)).astype(bf16)
    rv_ref[...] = proj(base + 2 * W_RET, W_RET).astype(bf16)
    rg_ref[...] = proj(base + 3 * W_RET, W_RET).astype(bf16)

    base += 4 * W_RET
    fq_ref[...] = (proj(base, W_FOX) * (LOG2E * HEAD_DIM ** -0.5)).astype(bf16)
    fk = proj(base + W_FOX, W_FOX).astype(bf16)
    for p in range(N_PAIRS):
        fk_ref[:, 2 * p * LANES:(2 * p + 1) * LANES] = fk[:, p * LANES:(p + 1) * LANES]
    fv_ref[...] = proj(base + 2 * W_FOX, W_FOX)
    fvt_ref[...] = fv_ref[...].T.astype(bf16)

    ff = proj(base + 3 * W_FOX, LANES) + fb_ref[...]
    logf = -(jnp.maximum(-ff, 0.0) + jnp.log1p(jnp.exp(-jnp.abs(ff))))

    @pl.when(i % blocks_per_seq == 0)
    def _():
        carry_ref[...] = jnp.zeros_like(carry_ref)

    ltri = jnp.where(tril, 1.0, 0.0).astype(bf16)
    carry = carry_ref[...]
    for c in range(n_chunks):
        within = sum(_dot(ltri, part) for part in _split_bf16(logf[c * CHUNK:(c + 1) * CHUNK, :], 3))
        cum = within + carry
        carry = cum[CHUNK - 1:CHUNK, :]
        terms = jnp.concatenate(_split_bf16(cum * LOG2E, C_TERMS), axis=1)
        placed = _dot(terms, place_ref[...]).astype(bf16)
        for p in range(N_PAIRS):
            fk_ref[c * CHUNK:(c + 1) * CHUNK, (2 * p + 1) * LANES:(2 * p + 2) * LANES] = (
                placed[:, p * LANES:(p + 1) * LANES])
    carry_ref[...] = carry


def _placement_matrix():
    place = np.zeros((C_TERMS * LANES, N_PAIRS * LANES), np.float32)
    for h in range(N_HEADS_FOX):
        for term in range(C_TERMS):
            place[term * LANES + h, (h // 2) * LANES + C_TERMS * (h % 2) + term] = 1.0
    return jnp.asarray(place, bf16)


def _proj_in(x2d, g, w_pad, cos4, sin4, sgu_w, sgu_b_full, sgu_vg, fox_b, *, seq, tm):
    t = x2d.shape[0]
    row = lambda width: pl.BlockSpec((tm, width), lambda i: (i, 0))
    out_widths = [W_SGU] + [W_RET] * 4 + [W_FOX, 2 * W_FOX]
    place = _placement_matrix()
    kern = functools.partial(_proj_in_kernel, tm=tm, blocks_per_seq=seq // tm)
    return pl.pallas_call(
        kern,
        grid=(t // tm,),
        in_specs=[row(D_MODEL), _resident((1, D_MODEL)), _resident((D_MODEL, N_IN_PAD)),
                  row(LANES), row(LANES),
                  _resident((N_HEADS_SGU, CHUNK, CHUNK)), _resident((CHUNK, W_SGU)),
                  _resident((1, W_SGU)), _resident((1, LANES)), _resident(place.shape)],
        out_specs=[row(w) for w in out_widths] + [pl.BlockSpec((W_FOX, tm), lambda i: (0, i))],
        out_shape=[jax.ShapeDtypeStruct((t, w), bf16) for w in out_widths]
        + [jax.ShapeDtypeStruct((W_FOX, t), bf16)],
        scratch_shapes=[pltpu.VMEM((1, LANES), f32), pltpu.VMEM((tm, W_FOX), f32)],
        compiler_params=_params("arbitrary"),
        name="proj_in",
    )(x2d, g, w_pad, cos4, sin4, sgu_w, sgu_b_full, sgu_vg, fox_b, place)


def _retention_kernel(q_ref, k_ref, v_ref, g_ref, dec_ref, qd_ref, kd_ref, cd_ref, o_ref, state_ref,
                      *, tm, blocks_per_seq):
    i = pl.program_id(0)

    @pl.when(i % blocks_per_seq == 0)
    def _():
        state_ref[...] = jnp.zeros_like(state_ref)

    row = lax.broadcasted_iota(jnp.int32, (CHUNK, LANES), 0)
    lane = lax.broadcasted_iota(jnp.int32, (CHUNK, LANES), 1)
    first_head = lane < HEAD_DIM
    same_head = (row < HEAD_DIM) == first_head
    zero = jnp.zeros((CHUNK, LANES), bf16)
    for c in range(tm // CHUNK):
        rows = slice(c * CHUNK, (c + 1) * CHUNK)
        for p in range(N_PAIRS):
            cols = slice(p * LANES, (p + 1) * LANES)
            q = q_ref[rows, cols]
            k = k_ref[rows, cols]
            v = v_ref[rows, cols]
            s0 = _dot_nt(jnp.where(first_head, q, zero), k) * dec_ref[2 * p]
            s1 = _dot_nt(jnp.where(first_head, zero, q), k) * dec_ref[2 * p + 1]
            o0 = _dot(s0.astype(bf16), v)
            o1 = _dot(s1.astype(bf16), v)
            state = state_ref[p]
            o = jnp.where(first_head, o0, o1) + _dot(q, state.astype(bf16)) * qd_ref[p]
            k_dec = (k.astype(f32) * kd_ref[p]).astype(bf16)
            state_ref[p] = state * cd_ref[p] + jnp.where(same_head, _dot_tn(k_dec, v), 0.0)
            ro = o * lax.rsqrt(_group_mean(o * o, LANES) + RMS_EPS)
            gate = g_ref[rows, cols].astype(f32)
            o_ref[rows, cols] = (gate * jax.nn.sigmoid(gate) * ro).astype(bf16)


def _retention_tables():
    log_g = jnp.log(1.0 - 2.0 ** (-5.0 - jnp.arange(N_HEADS_RET, dtype=f32)))
    idx = jnp.arange(CHUNK, dtype=f32)
    diff = idx[:, None] - idx[None, :]
    decay = jnp.where(diff >= 0, jnp.exp(log_g[:, None, None] * jnp.maximum(diff, 0.0)), 0.0)
    lane_g = jnp.repeat(log_g, HEAD_DIM).reshape(N_PAIRS, 1, LANES)
    q_decay = jnp.exp(lane_g * (idx + 1.0)[None, :, None])
    k_decay = jnp.exp(lane_g * (CHUNK - 1.0 - idx)[None, :, None])
    chunk_decay = jnp.broadcast_to(jnp.exp(lane_g * CHUNK).reshape(N_PAIRS, LANES, 1), (N_PAIRS, LANES, LANES))
    return decay, q_decay, k_decay, chunk_decay


def _retention(rq, rk, rv, rg, *, seq, tm):
    t = rq.shape[0]
    decay, q_decay, k_decay, chunk_decay = _retention_tables()
    row = pl.BlockSpec((tm, W_RET), lambda i: (i, 0))
    kern = functools.partial(_retention_kernel, tm=tm, blocks_per_seq=seq // tm)
    return pl.pallas_call(
        kern,
        grid=(t // tm,),
        in_specs=[row, row, row, row, _resident(decay.shape), _resident(q_decay.shape),
                  _resident(k_decay.shape), _resident(chunk_decay.shape)],
        out_specs=row,
        out_shape=jax.ShapeDtypeStruct((t, W_RET), bf16),
        scratch_shapes=[pltpu.VMEM((N_PAIRS, LANES, LANES), f32)],
        compiler_params=_params("arbitrary"),
        name="retention",
    )(rq, rk, rv, rg, decay, q_decay, k_decay, chunk_decay)


SUM_ROWS = 16


def _fox_kernel(q_ref, k_ref, vt_ref, o_ref, s_ref, m_ref, acc_ref, *, tq, tk):
    i = pl.program_id(2)
    q = q_ref[...]
    lane = lax.broadcasted_iota(jnp.int32, (tq, LANES), 1)
    first_head = lane < HEAD_DIM
    zero = jnp.zeros((tq, LANES), bf16)
    q_aug = []
    for a in range(2):
        qa = jnp.where(first_head, q, zero) if a == 0 else jnp.where(first_head, zero, q)
        pick = jnp.where((lane >= C_TERMS * a) & (lane < C_TERMS * (a + 1)), -1.0, 0.0).astype(bf16)
        q_aug.append(jnp.concatenate([qa, pick], axis=1))
    m_ref[...] = jnp.full_like(m_ref, MASK_VALUE)
    acc_ref[...] = jnp.zeros_like(acc_ref)
    ones_rows = jnp.ones((SUM_ROWS, tk), bf16)

    def scores(j, slot):
        off = pl.multiple_of(j * tk, tk)
        k = k_ref[pl.ds(off, tk), :]
        for a in range(2):
            s_ref[slot, a] = _dot_nt(k, q_aug[a])

    def consume(j, slot, masked):
        off = pl.multiple_of(j * tk, tk)
        for a in range(2):
            s = s_ref[slot, a]
            if masked:
                kpos = off + lax.broadcasted_iota(jnp.int32, (tk, tq), 0)
                qpos = i * tq + lax.broadcasted_iota(jnp.int32, (tk, tq), 1)
                s = jnp.where(kpos <= qpos, s, MASK_VALUE)
            m_old = m_ref[a]
            m_new = jnp.maximum(m_old, jnp.max(s, axis=0, keepdims=True))
            p = jnp.exp2(s - m_new)
            alpha = jnp.exp2(m_old - m_new)
            vt = jnp.concatenate([vt_ref[a * HEAD_DIM:(a + 1) * HEAD_DIM, pl.ds(off, tk)], ones_rows], axis=0)
            acc_ref[a] = alpha * acc_ref[a] + _dot(vt, p.astype(bf16))
            m_ref[a] = m_new

    scores(0, 0)

    def body(jj, carry):
        scores(2 * jj + 1, 1)
        consume(2 * jj, 0, False)
        scores(2 * jj + 2, 0)
        consume(2 * jj + 1, 1, False)
        return carry

    lax.fori_loop(0, i, body, 0)
    scores(2 * i + 1, 1)
    consume(2 * i, 0, True)
    consume(2 * i + 1, 1, True)
    o_t = jnp.concatenate([acc_ref[a, :HEAD_DIM] / acc_ref[a, HEAD_DIM:HEAD_DIM + 1] for a in range(2)], axis=0)
    o_ref[...] = o_t.T.astype(bf16)


def _fox(fq, fk_aug, fv_t, *, batch, seq, tq):
    t = fq.shape[0]
    nq = seq // tq
    tk = tq // 2
    q_spec = pl.BlockSpec((tq, LANES), lambda b, p, i: (b * nq + i, p))
    return pl.pallas_call(
        functools.partial(_fox_kernel, tq=tq, tk=tk),
        grid=(batch, N_PAIRS, nq),
        in_specs=[q_spec,
                  pl.BlockSpec((seq, 2 * LANES), lambda b, p, i: (b, p)),
                  pl.BlockSpec((LANES, seq), lambda b, p, i: (p, b))],
        out_specs=q_spec,
        out_shape=jax.ShapeDtypeStruct((t, W_FOX), bf16),
        scratch_shapes=[pltpu.VMEM((2, 2, tk, tq), f32), pltpu.VMEM((2, 1, tq), f32),
                        pltpu.VMEM((2, HEAD_DIM + SUM_ROWS, tq), f32)],
        compiler_params=_params("parallel", "parallel", "arbitrary"),
        name="fox_attention",
    )(fq, fk_aug, fv_t)


def _out_proj_kernel(a_ref, b_ref, c_ref, x_ref, w_ref, g_ref, o_ref):
    cat = jnp.concatenate([a_ref[...], b_ref[...], c_ref[...]], axis=1)
    o_ref[...] = x_ref[...] + _rms(_dot(cat, w_ref[...]), g_ref[...])


def _out_proj(out_a, out_b, out_c, x2d, w_o, g, *, tm):
    t = x2d.shape[0]
    row = lambda width: pl.BlockSpec((tm, width), lambda i: (i, 0))
    return pl.pallas_call(
        _out_proj_kernel,
        grid=(t // tm,),
        in_specs=[row(W_SGU), row(W_RET), row(W_FOX), row(D_MODEL),
                  _resident((D_MODEL, D_MODEL)), _resident((1, D_MODEL))],
        out_specs=row(D_MODEL),
        out_shape=jax.ShapeDtypeStruct((t, D_MODEL), f32),
        compiler_params=_params("parallel"),
        name="out_proj",
    )(out_a, out_b, out_c, x2d, w_o, g)


def _ffn_kernel(x_ref, gpre_ref, wg_ref, wu_ref, cw_ref, cb_ref, wd_ref, gpost_ref, o_ref, halo_ref,
                *, tm, fc, blocks_per_seq):
    i = pl.program_id(0)

    @pl.when(i % blocks_per_seq == 0)
    def _():
        halo_ref[...] = jnp.zeros_like(halo_ref)

    xf = x_ref[...]
    h = _rms(xf, gpre_ref[...]).astype(bf16)
    row8 = lax.broadcasted_iota(jnp.int32, (8, fc), 0)
    acc = jnp.zeros((tm, D_MODEL), f32)
    for c in range(D_FF // fc):
        cols = slice(c * fc, (c + 1) * fc)
        g = _dot(h, wg_ref[:, cols])
        up = _dot(h, wu_ref[:, cols])
        prev = halo_ref[:, cols]
        halo_ref[:, cols] = g[tm - 8:, :]
        g1 = pltpu.roll(g, 1, 0)
        g2 = pltpu.roll(g, 2, 0)
        g1 = jnp.concatenate([jnp.where(row8 < 1, pltpu.roll(prev, 1, 0), g1[:8]), g1[8:]], axis=0)
        g2 = jnp.concatenate([jnp.where(row8 < 2, pltpu.roll(prev, 2, 0), g2[:8]), g2[8:]], axis=0)
        conv = cb_ref[:, cols] + g2 * cw_ref[0:1, cols]
        conv = conv + g1 * cw_ref[1:2, cols]
        conv = conv + g * cw_ref[2:3, cols]
        act = _gelu(conv) * up
        acc = acc + _dot(act.astype(bf16), wd_ref[cols, :])
    o_ref[...] = xf + _rms(acc, gpost_ref[...])


def _ffn(x2d, gpre, w_gate, w_up, conv_w, conv_b, w_down, gpost, *, seq, tm, fc):
    t = x2d.shape[0]
    row = pl.BlockSpec((tm, D_MODEL), lambda i: (i, 0))
    kern = functools.partial(_ffn_kernel, tm=tm, fc=fc, blocks_per_seq=seq // tm)
    return pl.pallas_call(
        kern,
        grid=(t // tm,),
        in_specs=[row, _resident((1, D_MODEL)), _resident((D_MODEL, D_FF)), _resident((D_MODEL, D_FF)),
                  _resident(conv_w.shape), _resident((1, D_FF)), _resident((D_FF, D_MODEL)),
                  _resident((1, D_MODEL))],
        out_specs=row,
        out_shape=jax.ShapeDtypeStruct((t, D_MODEL), f32),
        scratch_shapes=[pltpu.VMEM((8, D_FF), f32)],
        compiler_params=_params("arbitrary"),
        name="ffn",
    )(x2d, gpre, w_gate, w_up, conv_w, conv_b, w_down, gpost)


def _ple_kernel(x_ref, p_ref, gpre_ref, wgate_ref, wproj_ref, gpost_ref, o_ref):
    xf = x_ref[...]
    gate = jax.nn.sigmoid(_dot(_rms(xf, gpre_ref[...]).astype(bf16), wgate_ref[...]))
    e = _dot(p_ref[...].astype(bf16), wproj_ref[...])
    o_ref[...] = xf + _rms(e * gate, gpost_ref[...])


def _ple(x2d, p2d, gpre, w_gate, w_proj, gpost, *, tm):
    t = x2d.shape[0]
    row = lambda width: pl.BlockSpec((tm, width), lambda i: (i, 0))
    return pl.pallas_call(
        _ple_kernel,
        grid=(t // tm,),
        in_specs=[row(D_MODEL), row(D_PLE), _resident((1, D_MODEL)), _resident((D_MODEL, D_MODEL)),
                  _resident((D_PLE, D_MODEL)), _resident((1, D_MODEL))],
        out_specs=row(D_MODEL),
        out_shape=jax.ShapeDtypeStruct((t, D_MODEL), f32),
        compiler_params=_params("parallel"),
        name="ple",
    )(x2d, p2d, gpre, w_gate, w_proj, gpost)


def _largest_tile(seq, cap):
    tile = cap
    while seq % tile:
        tile //= 2
    return tile


def kernel(x, p, positions, mix_pre_g, w_in, sgu_v_g, sgu_w, sgu_b, fox_b_f, w_o, mix_post_g, ffn_pre_g,
           w_gate, w_up, conv_w, conv_b, w_down, ffn_post_g, ple_pre_g, w_ple_gate, w_ple_proj, ple_post_g):
    batch, seq, _ = x.shape
    depth = w_in.shape[0]
    t = batch * seq
    assert seq % CHUNK == 0
    tm = _largest_tile(seq, 512)
    tq = _largest_tile(seq, 1024)
    fc = 1024

    cos4, sin4 = _rope_tables(positions)
    x2d = x.reshape(t, D_MODEL)
    for i in range(depth):
        w_pad = jnp.pad(w_in[i].astype(bf16), ((0, 0), (0, N_IN_PAD - w_in.shape[2])))
        sgu_b_full = jnp.repeat(sgu_b[i].T, HEAD_DIM, axis=1)
        fox_b = jnp.pad(fox_b_f[i], (0, LANES - N_HEADS_FOX))[None, :]
        out_a, rq, rk, rv, rg, fq, fk_aug, fv_t = _proj_in(
            x2d, mix_pre_g[i][None, :], w_pad, cos4, sin4, sgu_w[i], sgu_b_full,
            sgu_v_g[i].reshape(1, W_SGU), fox_b, seq=seq, tm=tm)
        out_b = _retention(rq, rk, rv, rg, seq=seq, tm=tm)
        out_c = _fox(fq, fk_aug, fv_t, batch=batch, seq=seq, tq=tq)
        x2d = _out_proj(out_a, out_b, out_c, x2d, w_o[i].astype(bf16), mix_post_g[i][None, :], tm=tm)
        x2d = _ffn(x2d, ffn_pre_g[i][None, :], w_gate[i].astype(bf16), w_up[i].astype(bf16), conv_w[i],
                   conv_b[i][None, :], w_down[i].astype(bf16), ffn_post_g[i][None, :], seq=seq, tm=tm, fc=fc)
        x2d = _ple(x2d, p[i].reshape(t, D_PLE), ple_pre_g[i][None, :], w_ple_gate[i].astype(bf16),
                   w_ple_proj[i].astype(bf16), ple_post_g[i][None, :], tm=tm)
    return x2d.reshape(batch, seq, D_MODEL)
```

```python
import functools

import numpy as np
import jax
import jax.numpy as jnp
from jax import lax
from jax.experimental import pallas as pl
from jax.experimental.pallas import tpu as pltpu

D_MODEL = 1024
D_PLE = 256
HEAD_DIM = 64
N_HEADS_SGU = 4
N_HEADS_RET = 6
N_HEADS_FOX = 6
W_SGU = N_HEADS_SGU * HEAD_DIM
W_RET = N_HEADS_RET * HEAD_DIM
W_FOX = N_HEADS_FOX * HEAD_DIM
CHUNK = 128
D_FF = 4 * D_MODEL
ROPE_BASE = 10000.0
RMS_EPS = 1e-6
LANES = 128
N_PAIRS = W_RET // LANES
N_IN_PAD = 2 * W_SGU + 4 * W_RET + 3 * W_FOX + LANES
MASK_VALUE = -1e30
LOG2E = 1.4426950408889634
C_TERMS = 3
VMEM_LIMIT = 56 * 1024 * 1024

f32 = jnp.float32
bf16 = jnp.bfloat16


def _params(*sem):
    return pltpu.CompilerParams(dimension_semantics=sem, vmem_limit_bytes=VMEM_LIMIT)


def _resident(shape):
    zeros = (0,) * len(shape)
    return pl.BlockSpec(shape, lambda *_: zeros, pipeline_mode=pl.Buffered(1))


def _rms(xf, g):
    y = xf * lax.rsqrt(jnp.mean(xf * xf, axis=-1, keepdims=True) + RMS_EPS)
    return y * g


def _dot(a, b):
    return jnp.dot(a, b, preferred_element_type=f32)


def _dot_nt(a, b):
    return lax.dot_general(a, b, (((1,), (1,)), ((), ())), preferred_element_type=f32)


def _dot_tn(a, b):
    return lax.dot_general(a, b, (((0,), (0,)), ((), ())), preferred_element_type=f32)


def _split_bf16(x, terms):
    parts = []
    r = x
    for _ in range(terms):
        p = r.astype(bf16)
        parts.append(p)
        r = r - p.astype(f32)
    return parts


def _group_mean(y, width):
    r = lax.broadcasted_iota(jnp.int32, (width, width), 0) // HEAD_DIM
    c = lax.broadcasted_iota(jnp.int32, (width, width), 1) // HEAD_DIM
    gm = jnp.where(r == c, 1.0 / HEAD_DIM, 0.0).astype(bf16)
    return sum(_dot(p, gm) for p in _split_bf16(y, 2))


def _gelu(x):
    return jax.nn.gelu(x, approximate=True)


def _rope_kernel(pos_ref, invf_ref, cos_ref, sin_ref, nsin_ref):
    ang = pos_ref[...].astype(f32) * invf_ref[...]
    s = jnp.sin(ang)
    cos_ref[...] = jnp.cos(ang)
    sin_ref[...] = s
    nsin_ref[...] = -s


def _rope_tables(positions):
    t = positions.size
    half = HEAD_DIM // 2
    per_row = LANES // half
    rows = t // per_row
    pos = jnp.repeat(positions.reshape(rows, per_row), half, axis=1)
    inv_freq = ROPE_BASE ** (-jnp.arange(half, dtype=f32) / half)
    invf = jnp.tile(inv_freq, per_row)[None, :]
    tr = min(rows, 1024)
    spec = pl.BlockSpec((tr, LANES), lambda i: (i, 0))
    cos, sin, nsin = pl.pallas_call(
        _rope_kernel,
        grid=(rows // tr,),
        in_specs=[spec, _resident((1, LANES))],
        out_specs=[spec, spec, spec],
        out_shape=[jax.ShapeDtypeStruct((rows, LANES), f32)] * 3,
        compiler_params=_params("parallel"),
        name="rope_tables",
    )(pos, invf)
    cos, sin, nsin = (a.reshape(t, half) for a in (cos, sin, nsin))
    return jnp.tile(cos, (1, per_row)), jnp.concatenate([nsin, sin] * (per_row // 2), axis=1)


def _proj_in_kernel(x_ref, g_ref, w_ref, cos_ref, sin_ref, sguw_ref, sgub_ref, vg_ref, fb_ref, place_ref,
                    oa_ref, rq_ref, rk_ref, rv_ref, rg_ref, fq_ref, fk_ref, fvt_ref, cend_ref, nrm_ref,
                    carry_ref, fv_ref, *, tm, blocks_per_seq):
    i = pl.program_id(0)
    n_chunks = tm // CHUNK
    h = _rms(x_ref[...], g_ref[...]).astype(bf16)

    def proj(lo, width):
        return _dot(h, w_ref[:, lo:lo + width])

    sq_row = lax.broadcasted_iota(jnp.int32, (CHUNK, CHUNK), 0)
    sq_col = lax.broadcasted_iota(jnp.int32, (CHUNK, CHUNK), 1)
    tril = sq_row >= sq_col
    first_head = sq_col < HEAD_DIM

    u = _gelu(proj(0, W_SGU))
    v = _gelu(proj(W_SGU, W_SGU))
    v = v * lax.rsqrt(_group_mean(v * v, W_SGU) + RMS_EPS) * vg_ref[...]
    for p in range(W_SGU // LANES):
        cols = slice(p * LANES, (p + 1) * LANES)
        w_pair = jnp.concatenate(
            [jnp.where(tril, sguw_ref[2 * p], 0.0), jnp.where(tril, sguw_ref[2 * p + 1], 0.0)],
            axis=0).astype(bf16)
        for c in range(n_chunks):
            rows = slice(c * CHUNK, (c + 1) * CHUNK)
            r = _dot(w_pair, v[rows, cols].astype(bf16))
            s = jnp.where(first_head, r[:CHUNK], r[CHUNK:]) + sgub_ref[:, cols]
            oa_ref[rows, cols] = (u[rows, cols] * s).astype(bf16)

    cosv = cos_ref[...]
    sinv = sin_ref[...]
    lane = lax.broadcasted_iota(jnp.int32, (tm, LANES), 1)
    first_half = (lane % HEAD_DIM) < (HEAD_DIM // 2)

    def rotary(z):
        outs = []
        for s in range(N_PAIRS):
            t = z[:, s * LANES:(s + 1) * LANES]
            partner = jnp.where(first_half, pltpu.roll(t, LANES - HEAD_DIM // 2, 1),
                                pltpu.roll(t, HEAD_DIM // 2, 1))
            outs.append(t * cosv + partner * sinv)
        return jnp.concatenate(outs, axis=1)

    base = 2 * W_SGU
    rq_ref[...] = rotary(proj(base, W_RET)).astype(bf16)
    rk_ref[...] = (rotary(proj(base + W_RET, W_RET)) * (HEAD_DIM ** -0.5)).astype(bf16)
    rv_ref[...] = proj(base + 2 * W_RET, W_RET).astype(bf16)
    rg_ref[...] = proj(base + 3 * W_RET, W_RET).astype(bf16)

    base += 4 * W_RET
    fq = (proj(base, W_FOX) * (LOG2E * HEAD_DIM ** -0.5)).astype(bf16)
    fq_ref[...] = fq
    fk = proj(base + W_FOX, W_FOX).astype(bf16)
    for p in range(N_PAIRS):
        fk_ref[:, 2 * p * LANES:(2 * p + 1) * LANES] = fk[:, p * LANES:(p + 1) * LANES]
    fv_ref[...] = proj(base + 2 * W_FOX, W_FOX)
    fvt_ref[...] = fv_ref[...].T.astype(bf16)

    gr = lax.broadcasted_iota(jnp.int32, (2 * W_FOX, LANES), 0) // HEAD_DIM
    gc = lax.broadcasted_iota(jnp.int32, (2 * W_FOX, LANES), 1)
    head_sum = jnp.where(gr == gc, 1.0, 0.0).astype(bf16)
    qk = jnp.concatenate([fq, fk], axis=1).astype(f32)
    norm2 = jnp.max(_dot((qk * qk).astype(bf16), head_sum), axis=0, keepdims=True)

    @pl.when(i % blocks_per_seq == 0)
    def _():
        nrm_ref[...] = jnp.zeros_like(nrm_ref)

    nrm_ref[...] = jnp.maximum(nrm_ref[...], norm2)

    ff = proj(base + 3 * W_FOX, LANES) + fb_ref[...]
    logf = -(jnp.maximum(-ff, 0.0) + jnp.log1p(jnp.exp(-jnp.abs(ff))))

    @pl.when(i % blocks_per_seq == 0)
    def _():
        carry_ref[...] = jnp.zeros_like(carry_ref)

    ltri = jnp.where(tril, 1.0, 0.0).astype(bf16)
    carry = carry_ref[...]
    for c in range(n_chunks):
        within = sum(_dot(ltri, part) for part in _split_bf16(logf[c * CHUNK:(c + 1) * CHUNK, :], 3))
        cum = within + carry
        carry = cum[CHUNK - 1:CHUNK, :]
        cend_ref[8 * c:8 * (c + 1), :] = jnp.broadcast_to(carry * LOG2E, (8, LANES))
        terms = jnp.concatenate(_split_bf16(cum * LOG2E, C_TERMS), axis=1)
        placed = _dot(terms, place_ref[...]).astype(bf16)
        for p in range(N_PAIRS):
            fk_ref[c * CHUNK:(c + 1) * CHUNK, (2 * p + 1) * LANES:(2 * p + 2) * LANES] = (
                placed[:, p * LANES:(p + 1) * LANES])
    carry_ref[...] = carry


def _placement_matrix():
    place = np.zeros((C_TERMS * LANES, N_PAIRS * LANES), np.float32)
    for h in range(N_HEADS_FOX):
        for term in range(C_TERMS):
            place[term * LANES + h, (h // 2) * LANES + C_TERMS * (h % 2) + term] = 1.0
    return jnp.asarray(place, bf16)


def _proj_in(x2d, g, w_pad, cos4, sin4, sgu_w, sgu_b_full, sgu_vg, fox_b, *, seq, tm):
    t = x2d.shape[0]
    row = lambda width: pl.BlockSpec((tm, width), lambda i: (i, 0))
    out_widths = [W_SGU] + [W_RET] * 4 + [W_FOX, 2 * W_FOX]
    place = _placement_matrix()
    kern = functools.partial(_proj_in_kernel, tm=tm, blocks_per_seq=seq // tm)
    return pl.pallas_call(
        kern,
        grid=(t // tm,),
        in_specs=[row(D_MODEL), _resident((1, D_MODEL)), _resident((D_MODEL, N_IN_PAD)),
                  row(LANES), row(LANES),
                  _resident((N_HEADS_SGU, CHUNK, CHUNK)), _resident((CHUNK, W_SGU)),
                  _resident((1, W_SGU)), _resident((1, LANES)), _resident(place.shape)],
        out_specs=[row(w) for w in out_widths] + [
            pl.BlockSpec((W_FOX, tm), lambda i: (0, i)),
            pl.BlockSpec((8 * tm // CHUNK, LANES), lambda i: (i, 0)),
            pl.BlockSpec((8, LANES), lambda i: (i // (seq // tm), 0))],
        out_shape=[jax.ShapeDtypeStruct((t, w), bf16) for w in out_widths] + [
            jax.ShapeDtypeStruct((W_FOX, t), bf16),
            jax.ShapeDtypeStruct((8 * t // CHUNK, LANES), f32),
            jax.ShapeDtypeStruct((8 * t // seq, LANES), f32)],
        scratch_shapes=[pltpu.VMEM((1, LANES), f32), pltpu.VMEM((tm, W_FOX), f32)],
        compiler_params=_params("arbitrary"),
        name="proj_in",
    )(x2d, g, w_pad, cos4, sin4, sgu_w, sgu_b_full, sgu_vg, fox_b, place)


def _retention_kernel(q_ref, k_ref, v_ref, g_ref, dec_ref, qd_ref, kd_ref, cd_ref, o_ref, state_ref,
                      *, tm, blocks_per_seq):
    i = pl.program_id(0)

    @pl.when(i % blocks_per_seq == 0)
    def _():
        state_ref[...] = jnp.zeros_like(state_ref)

    row = lax.broadcasted_iota(jnp.int32, (CHUNK, LANES), 0)
    lane = lax.broadcasted_iota(jnp.int32, (CHUNK, LANES), 1)
    first_head = lane < HEAD_DIM
    same_head = (row < HEAD_DIM) == first_head
    zero = jnp.zeros((CHUNK, LANES), bf16)
    for c in range(tm // CHUNK):
        rows = slice(c * CHUNK, (c + 1) * CHUNK)
        for p in range(N_PAIRS):
            cols = slice(p * LANES, (p + 1) * LANES)
            q = q_ref[rows, cols]
            k = k_ref[rows, cols]
            v = v_ref[rows, cols]
            s0 = _dot_nt(jnp.where(first_head, q, zero), k) * dec_ref[2 * p]
            s1 = _dot_nt(jnp.where(first_head, zero, q), k) * dec_ref[2 * p + 1]
            o0 = _dot(s0.astype(bf16), v)
            o1 = _dot(s1.astype(bf16), v)
            state = state_ref[p]
            o = jnp.where(first_head, o0, o1) + _dot(q, state.astype(bf16)) * qd_ref[p]
            k_dec = (k.astype(f32) * kd_ref[p]).astype(bf16)
            state_ref[p] = state * cd_ref[p] + jnp.where(same_head, _dot_tn(k_dec, v), 0.0)
            ro = o * lax.rsqrt(_group_mean(o * o, LANES) + RMS_EPS)
            gate = g_ref[rows, cols].astype(f32)
            o_ref[rows, cols] = (gate * jax.nn.sigmoid(gate) * ro).astype(bf16)


def _retention_tables():
    log_g = jnp.log(1.0 - 2.0 ** (-5.0 - jnp.arange(N_HEADS_RET, dtype=f32)))
    idx = jnp.arange(CHUNK, dtype=f32)
    diff = idx[:, None] - idx[None, :]
    decay = jnp.where(diff >= 0, jnp.exp(log_g[:, None, None] * jnp.maximum(diff, 0.0)), 0.0)
    lane_g = jnp.repeat(log_g, HEAD_DIM).reshape(N_PAIRS, 1, LANES)
    q_decay = jnp.exp(lane_g * (idx + 1.0)[None, :, None])
    k_decay = jnp.exp(lane_g * (CHUNK - 1.0 - idx)[None, :, None])
    chunk_decay = jnp.broadcast_to(jnp.exp(lane_g * CHUNK).reshape(N_PAIRS, LANES, 1), (N_PAIRS, LANES, LANES))
    return decay, q_decay, k_decay, chunk_decay


def _retention(rq, rk, rv, rg, *, seq, tm):
    t = rq.shape[0]
    decay, q_decay, k_decay, chunk_decay = _retention_tables()
    row = pl.BlockSpec((tm, W_RET), lambda i: (i, 0))
    kern = functools.partial(_retention_kernel, tm=tm, blocks_per_seq=seq // tm)
    return pl.pallas_call(
        kern,
        grid=(t // tm,),
        in_specs=[row, row, row, row, _resident(decay.shape), _resident(q_decay.shape),
                  _resident(k_decay.shape), _resident(chunk_decay.shape)],
        out_specs=row,
        out_shape=jax.ShapeDtypeStruct((t, W_RET), bf16),
        scratch_shapes=[pltpu.VMEM((N_PAIRS, LANES, LANES), f32)],
        compiler_params=_params("arbitrary"),
        name="retention",
    )(rq, rk, rv, rg, decay, q_decay, k_decay, chunk_decay)


SUM_ROWS = 16


def _fox_kernel(cend_ref, thr_ref, q_ref, k_ref, vt_ref, o_ref, s_ref, m_ref, acc_ref, *, tq, tk, n_chunks):
    b = pl.program_id(0)
    pair = pl.program_id(1)
    i = pl.program_id(2)

    def first_needed_block():
        def scan(j, first):
            needed = False
            for a in range(2):
                head = (b * N_HEADS_FOX + 2 * pair + a)
                base = head * n_chunks
                key_end = cend_ref[base + (tk // CHUNK) * (j + 1) - 1]
                query_start = cend_ref[base + jnp.maximum((tq // CHUNK) * i - 1, 0)]
                needed = jnp.logical_or(needed, key_end - query_start <= thr_ref[head])
            return jnp.where(jnp.logical_and(needed, first == 2 * i), j, first)
        return lax.fori_loop(0, 2 * i, scan, 2 * i)

    first_pair = first_needed_block() // 2
    q = q_ref[...]
    lane = lax.broadcasted_iota(jnp.int32, (tq, LANES), 1)
    first_head = lane < HEAD_DIM
    zero = jnp.zeros((tq, LANES), bf16)
    q_aug = []
    for a in range(2):
        qa = jnp.where(first_head, q, zero) if a == 0 else jnp.where(first_head, zero, q)
        pick = jnp.where((lane >= C_TERMS * a) & (lane < C_TERMS * (a + 1)), -1.0, 0.0).astype(bf16)
        q_aug.append(jnp.concatenate([qa, pick], axis=1))
    m_ref[...] = jnp.full_like(m_ref, MASK_VALUE)
    acc_ref[...] = jnp.zeros_like(acc_ref)
    ones_rows = jnp.ones((SUM_ROWS, tk), bf16)

    def scores(j, slot):
        off = pl.multiple_of(j * tk, tk)
        k = k_ref[pl.ds(off, tk), :]
        for a in range(2):
            s_ref[slot, a] = _dot_nt(k, q_aug[a])

    def consume(j, slot, masked):
        off = pl.multiple_of(j * tk, tk)
        for a in range(2):
            s = s_ref[slot, a]
            if masked:
                kpos = off + lax.broadcasted_iota(jnp.int32, (tk, tq), 0)
                qpos = i * tq + lax.broadcasted_iota(jnp.int32, (tk, tq), 1)
                s = jnp.where(kpos <= qpos, s, MASK_VALUE)
            m_old = m_ref[a]
            m_new = jnp.maximum(m_old, jnp.max(s, axis=0, keepdims=True))
            p = jnp.exp2(s - m_new)
            alpha = jnp.exp2(m_old - m_new)
            vt = jnp.concatenate([vt_ref[a * HEAD_DIM:(a + 1) * HEAD_DIM, pl.ds(off, tk)], ones_rows], axis=0)
            acc_ref[a] = alpha * acc_ref[a] + _dot(vt, p.astype(bf16))
            m_ref[a] = m_new

    scores(2 * first_pair, 0)

    def body(jj, carry):
        scores(2 * jj + 1, 1)
        consume(2 * jj, 0, False)
        scores(2 * jj + 2, 0)
        consume(2 * jj + 1, 1, False)
        return carry

    lax.fori_loop(first_pair, i, body, 0)
    scores(2 * i + 1, 1)
    consume(2 * i, 0, True)
    consume(2 * i + 1, 1, True)
    o_t = jnp.concatenate([acc_ref[a, :HEAD_DIM] / acc_ref[a, HEAD_DIM:HEAD_DIM + 1] for a in range(2)], axis=0)
    o_ref[...] = o_t.T.astype(bf16)


EXP2_UNDERFLOW = 160.0
NORM_SLACK = 1.02


def _fox(fq, fk_aug, fv_t, cend, norm2, *, batch, seq, tq):
    t = fq.shape[0]
    nq = seq // tq
    tk = tq // 2
    n_chunks = seq // CHUNK
    cend_flat = cend.reshape(batch, n_chunks, 8, LANES)[:, :, 0, :N_HEADS_FOX].transpose(0, 2, 1).reshape(-1)
    norm2 = norm2.reshape(batch, 8, LANES)[:, 0, :2 * N_HEADS_FOX].reshape(batch, 2, N_HEADS_FOX)
    thr = (2.0 * NORM_SLACK * jnp.sqrt(norm2[:, 0] * norm2[:, 1]) + EXP2_UNDERFLOW).reshape(-1)
    smem = pl.BlockSpec(memory_space=pltpu.SMEM)
    q_spec = pl.BlockSpec((tq, LANES), lambda b, p, i: (b * nq + i, p))
    return pl.pallas_call(
        functools.partial(_fox_kernel, tq=tq, tk=tk, n_chunks=n_chunks),
        grid=(batch, N_PAIRS, nq),
        in_specs=[smem, smem, q_spec,
                  pl.BlockSpec((seq, 2 * LANES), lambda b, p, i: (b, p)),
                  pl.BlockSpec((LANES, seq), lambda b, p, i: (p, b))],
        out_specs=q_spec,
        out_shape=jax.ShapeDtypeStruct((t, W_FOX), bf16),
        scratch_shapes=[pltpu.VMEM((2, 2, tk, tq), f32), pltpu.VMEM((2, 1, tq), f32),
                        pltpu.VMEM((2, HEAD_DIM + SUM_ROWS, tq), f32)],
        compiler_params=_params("parallel", "parallel", "arbitrary"),
        name="fox_attention",
    )(cend_flat, thr, fq, fk_aug, fv_t)


def _out_proj_kernel(a_ref, b_ref, c_ref, x_ref, w_ref, g_ref, o_ref):
    cat = jnp.concatenate([a_ref[...], b_ref[...], c_ref[...]], axis=1)
    o_ref[...] = x_ref[...] + _rms(_dot(cat, w_ref[...]), g_ref[...])


def _out_proj(out_a, out_b, out_c, x2d, w_o, g, *, tm):
    t = x2d.shape[0]
    row = lambda width: pl.BlockSpec((tm, width), lambda i: (i, 0))
    return pl.pallas_call(
        _out_proj_kernel,
        grid=(t // tm,),
        in_specs=[row(W_SGU), row(W_RET), row(W_FOX), row(D_MODEL),
                  _resident((D_MODEL, D_MODEL)), _resident((1, D_MODEL))],
        out_specs=row(D_MODEL),
        out_shape=jax.ShapeDtypeStruct((t, D_MODEL), f32),
        compiler_params=_params("parallel"),
        name="out_proj",
    )(out_a, out_b, out_c, x2d, w_o, g)


def _ffn_kernel(x_ref, gpre_ref, wg_ref, wu_ref, cw_ref, cb_ref, wd_ref, gpost_ref, o_ref, halo_ref,
                *, tm, fc, blocks_per_seq):
    i = pl.program_id(0)

    @pl.when(i % blocks_per_seq == 0)
    def _():
        halo_ref[...] = jnp.zeros_like(halo_ref)

    xf = x_ref[...]
    h = _rms(xf, gpre_ref[...]).astype(bf16)
    row8 = lax.broadcasted_iota(jnp.int32, (8, fc), 0)
    acc = jnp.zeros((tm, D_MODEL), f32)
    for c in range(D_FF // fc):
        cols = slice(c * fc, (c + 1) * fc)
        g = _dot(h, wg_ref[:, cols])
        up = _dot(h, wu_ref[:, cols])
        prev = halo_ref[:, cols]
        halo_ref[:, cols] = g[tm - 8:, :]
        g1 = pltpu.roll(g, 1, 0)
        g2 = pltpu.roll(g, 2, 0)
        g1 = jnp.concatenate([jnp.where(row8 < 1, pltpu.roll(prev, 1, 0), g1[:8]), g1[8:]], axis=0)
        g2 = jnp.concatenate([jnp.where(row8 < 2, pltpu.roll(prev, 2, 0), g2[:8]), g2[8:]], axis=0)
        conv = cb_ref[:, cols] + g2 * cw_ref[0:1, cols]
        conv = conv + g1 * cw_ref[1:2, cols]
        conv = conv + g * cw_ref[2:3, cols]
        act = _gelu(conv) * up
        acc = acc + _dot(act.astype(bf16), wd_ref[cols, :])
    o_ref[...] = xf + _rms(acc, gpost_ref[...])


def _ffn(x2d, gpre, w_gate, w_up, conv_w, conv_b, w_down, gpost, *, seq, tm, fc):
    t = x2d.shape[0]
    row = pl.BlockSpec((tm, D_MODEL), lambda i: (i, 0))
    kern = functools.partial(_ffn_kernel, tm=tm, fc=fc, blocks_per_seq=seq // tm)
    return pl.pallas_call(
        kern,
        grid=(t // tm,),
        in_specs=[row, _resident((1, D_MODEL)), _resident((D_MODEL, D_FF)), _resident((D_MODEL, D_FF)),
                  _resident(conv_w.shape), _resident((1, D_FF)), _resident((D_FF, D_MODEL)),
                  _resident((1, D_MODEL))],
        out_specs=row,
        out_shape=jax.ShapeDtypeStruct((t, D_MODEL), f32),
        scratch_shapes=[pltpu.VMEM((8, D_FF), f32)],
        compiler_params=_params("arbitrary"),
        name="ffn",
    )(x2d, gpre, w_gate, w_up, conv_w, conv_b, w_down, gpost)


def _ple_kernel(x_ref, p_ref, gpre_ref, wgate_ref, wproj_ref, gpost_ref, o_ref):
    xf = x_ref[...]
    gate = jax.nn.sigmoid(_dot(_rms(xf, gpre_ref[...]).astype(bf16), wgate_ref[...]))
    e = _dot(p_ref[...].astype(bf16), wproj_ref[...])
    o_ref[...] = xf + _rms(e * gate, gpost_ref[...])


def _ple(x2d, p2d, gpre, w_gate, w_proj, gpost, *, tm):
    t = x2d.shape[0]
    row = lambda width: pl.BlockSpec((tm, width), lambda i: (i, 0))
    return pl.pallas_call(
        _ple_kernel,
        grid=(t // tm,),
        in_specs=[row(D_MODEL), row(D_PLE), _resident((1, D_MODEL)), _resident((D_MODEL, D_MODEL)),
                  _resident((D_PLE, D_MODEL)), _resident((1, D_MODEL))],
        out_specs=row(D_MODEL),
        out_shape=jax.ShapeDtypeStruct((t, D_MODEL), f32),
        compiler_params=_params("parallel"),
        name="ple",
    )(x2d, p2d, gpre, w_gate, w_proj, gpost)


def _largest_tile(seq, cap):
    tile = cap
    while seq % tile:
        tile //= 2
    return tile


def kernel(x, p, positions, mix_pre_g, w_in, sgu_v_g, sgu_w, sgu_b, fox_b_f, w_o, mix_post_g, ffn_pre_g,
           w_gate, w_up, conv_w, conv_b, w_down, ffn_post_g, ple_pre_g, w_ple_gate, w_ple_proj, ple_post_g):
    batch, seq, _ = x.shape
    depth = w_in.shape[0]
    t = batch * seq
    assert seq % CHUNK == 0
    tm = _largest_tile(seq, 512)
    tq = _largest_tile(seq, 1024)
    fc = 1024

    cos4, sin4 = _rope_tables(positions)
    x2d = x.reshape(t, D_MODEL)
    for i in range(depth):
        w_pad = jnp.pad(w_in[i].astype(bf16), ((0, 0), (0, N_IN_PAD - w_in.shape[2])))
        sgu_b_full = jnp.repeat(sgu_b[i].T, HEAD_DIM, axis=1)
        fox_b = jnp.pad(fox_b_f[i], (0, LANES - N_HEADS_FOX))[None, :]
        out_a, rq, rk, rv, rg, fq, fk_aug, fv_t, cend, norm2 = _proj_in(
            x2d, mix_pre_g[i][None, :], w_pad, cos4, sin4, sgu_w[i], sgu_b_full,
            sgu_v_g[i].reshape(1, W_SGU), fox_b, seq=seq, tm=tm)
        out_b = _retention(rq, rk, rv, rg, seq=seq, tm=tm)
        out_c = _fox(fq, fk_aug, fv_t, cend, norm2, batch=batch, seq=seq, tq=tq)
        x2d = _out_proj(out_a, out_b, out_c, x2d, w_o[i].astype(bf16), mix_post_g[i][None, :], tm=tm)
        x2d = _ffn(x2d, ffn_pre_g[i][None, :], w_gate[i].astype(bf16), w_up[i].astype(bf16), conv_w[i],
                   conv_b[i][None, :], w_down[i].astype(bf16), ffn_post_g[i][None, :], seq=seq, tm=tm, fc=fc)
        x2d = _ple(x2d, p[i].reshape(t, D_PLE), ple_pre_g[i][None, :], w_ple_gate[i].astype(bf16),
                   w_ple_proj[i].astype(bf16), ple_post_g[i][None, :], tm=tm)
    return x2d.reshape(batch, seq, D_MODEL)
```

```python
import functools

import numpy as np
import jax
import jax.numpy as jnp
from jax import lax
from jax.experimental import pallas as pl
from jax.experimental.pallas import tpu as pltpu

D_MODEL = 1024
D_PLE = 256
HEAD_DIM = 64
N_HEADS_SGU = 4
N_HEADS_RET = 6
N_HEADS_FOX = 6
W_SGU = N_HEADS_SGU * HEAD_DIM
W_RET = N_HEADS_RET * HEAD_DIM
W_FOX = N_HEADS_FOX * HEAD_DIM
CHUNK = 128
D_FF = 4 * D_MODEL
ROPE_BASE = 10000.0
RMS_EPS = 1e-6
LANES = 128
N_PAIRS = W_RET // LANES
N_IN_PAD = 2 * W_SGU + 4 * W_RET + 3 * W_FOX + LANES
MASK_VALUE = -1e30
LOG2E = 1.4426950408889634
C_TERMS = 3
VMEM_LIMIT = 56 * 1024 * 1024

f32 = jnp.float32
bf16 = jnp.bfloat16


def _params(*sem):
    return pltpu.CompilerParams(dimension_semantics=sem, vmem_limit_bytes=VMEM_LIMIT)


def _resident(shape):
    zeros = (0,) * len(shape)
    return pl.BlockSpec(shape, lambda *_: zeros, pipeline_mode=pl.Buffered(1))


def _rms(xf, g):
    y = xf * lax.rsqrt(jnp.mean(xf * xf, axis=-1, keepdims=True) + RMS_EPS)
    return y * g


def _dot(a, b):
    return jnp.dot(a, b, preferred_element_type=f32)


def _dot_nt(a, b):
    return lax.dot_general(a, b, (((1,), (1,)), ((), ())), preferred_element_type=f32)


def _dot_tn(a, b):
    return lax.dot_general(a, b, (((0,), (0,)), ((), ())), preferred_element_type=f32)


def _split_bf16(x, terms):
    parts = []
    r = x
    for _ in range(terms):
        p = r.astype(bf16)
        parts.append(p)
        r = r - p.astype(f32)
    return parts


def _group_mean(y, width):
    r = lax.broadcasted_iota(jnp.int32, (width, width), 0) // HEAD_DIM
    c = lax.broadcasted_iota(jnp.int32, (width, width), 1) // HEAD_DIM
    gm = jnp.where(r == c, 1.0 / HEAD_DIM, 0.0).astype(bf16)
    return sum(_dot(p, gm) for p in _split_bf16(y, 2))


def _gelu(x):
    return jax.nn.gelu(x, approximate=True)


def _rope_kernel(pos_ref, invf_ref, cos_ref, sin_ref, nsin_ref):
    ang = pos_ref[...].astype(f32) * invf_ref[...]
    s = jnp.sin(ang)
    cos_ref[...] = jnp.cos(ang)
    sin_ref[...] = s
    nsin_ref[...] = -s


def _rope_tables(positions):
    t = positions.size
    half = HEAD_DIM // 2
    per_row = LANES // half
    rows = t // per_row
    pos = jnp.repeat(positions.reshape(rows, per_row), half, axis=1)
    inv_freq = ROPE_BASE ** (-jnp.arange(half, dtype=f32) / half)
    invf = jnp.tile(inv_freq, per_row)[None, :]
    tr = min(rows, 1024)
    spec = pl.BlockSpec((tr, LANES), lambda i: (i, 0))
    cos, sin, nsin = pl.pallas_call(
        _rope_kernel,
        grid=(rows // tr,),
        in_specs=[spec, _resident((1, LANES))],
        out_specs=[spec, spec, spec],
        out_shape=[jax.ShapeDtypeStruct((rows, LANES), f32)] * 3,
        compiler_params=_params("parallel"),
        name="rope_tables",
    )(pos, invf)
    cos, sin, nsin = (a.reshape(t, half) for a in (cos, sin, nsin))
    return jnp.tile(cos, (1, per_row)), jnp.concatenate([nsin, sin] * (per_row // 2), axis=1)


def _proj_in_kernel(x_ref, g_ref, w_ref, cos_ref, sin_ref, sguw_ref, sgub_ref, vg_ref, fb_ref, place_ref,
                    oa_ref, rq_ref, rk_ref, rv_ref, rg_ref, fq_ref, fk_ref, fvt_ref, cend_ref, nrm_ref,
                    carry_ref, fv_ref, *, tm, blocks_per_seq):
    i = pl.program_id(0)
    n_chunks = tm // CHUNK
    h = _rms(x_ref[...], g_ref[...]).astype(bf16)

    def proj(lo, width):
        return _dot(h, w_ref[:, lo:lo + width])

    sq_row = lax.broadcasted_iota(jnp.int32, (CHUNK, CHUNK), 0)
    sq_col = lax.broadcasted_iota(jnp.int32, (CHUNK, CHUNK), 1)
    tril = sq_row >= sq_col
    first_head = sq_col < HEAD_DIM

    u = _gelu(proj(0, W_SGU))
    v = _gelu(proj(W_SGU, W_SGU))
    v = v * lax.rsqrt(_group_mean(v * v, W_SGU) + RMS_EPS) * vg_ref[...]
    for p in range(W_SGU // LANES):
        cols = slice(p * LANES, (p + 1) * LANES)
        w_pair = jnp.concatenate(
            [jnp.where(tril, sguw_ref[2 * p], 0.0), jnp.where(tril, sguw_ref[2 * p + 1], 0.0)],
            axis=0).astype(bf16)
        for c in range(n_chunks):
            rows = slice(c * CHUNK, (c + 1) * CHUNK)
            r = _dot(w_pair, v[rows, cols].astype(bf16))
            s = jnp.where(first_head, r[:CHUNK], r[CHUNK:]) + sgub_ref[:, cols]
            oa_ref[rows, cols] = (u[rows, cols] * s).astype(bf16)

    cosv = cos_ref[...]
    sinv = sin_ref[...]
    lane = lax.broadcasted_iota(jnp.int32, (tm, LANES), 1)
    first_half = (lane % HEAD_DIM) < (HEAD_DIM // 2)

    def rotary(z):
        outs = []
        for s in range(N_PAIRS):
            t = z[:, s * LANES:(s + 1) * LANES]
            partner = jnp.where(first_half, pltpu.roll(t, LANES - HEAD_DIM // 2, 1),
                                pltpu.roll(t, HEAD_DIM // 2, 1))
            outs.append(t * cosv + partner * sinv)
        return jnp.concatenate(outs, axis=1)

    base = 2 * W_SGU
    rq_ref[...] = rotary(proj(base, W_RET)).astype(bf16)
    rk_ref[...] = (rotary(proj(base + W_RET, W_RET)) * (HEAD_DIM ** -0.5)).astype(bf16)
    rv_ref[...] = proj(base + 2 * W_RET, W_RET).astype(bf16)
    rg_ref[...] = proj(base + 3 * W_RET, W_RET).astype(bf16)

    base += 4 * W_RET
    fq = (proj(base, W_FOX) * (LOG2E * HEAD_DIM ** -0.5)).astype(bf16)
    fq_ref[...] = fq
    fk = proj(base + W_FOX, W_FOX).astype(bf16)
    for p in range(N_PAIRS):
        fk_ref[:, 2 * p * LANES:(2 * p + 1) * LANES] = fk[:, p * LANES:(p + 1) * LANES]
    fv_ref[...] = proj(base + 2 * W_FOX, W_FOX)
    fvt_ref[...] = fv_ref[...].T.astype(bf16)

    gr = lax.broadcasted_iota(jnp.int32, (2 * W_FOX, LANES), 0) // HEAD_DIM
    gc = lax.broadcasted_iota(jnp.int32, (2 * W_FOX, LANES), 1)
    head_sum = jnp.where(gr == gc, 1.0, 0.0).astype(bf16)
    qk = jnp.concatenate([fq, fk], axis=1).astype(f32)
    norm2 = jnp.max(_dot((qk * qk).astype(bf16), head_sum), axis=0, keepdims=True)

    @pl.when(i % blocks_per_seq == 0)
    def _():
        nrm_ref[...] = jnp.zeros_like(nrm_ref)

    nrm_ref[...] = jnp.maximum(nrm_ref[...], norm2)

    ff = proj(base + 3 * W_FOX, LANES) + fb_ref[...]
    logf = -(jnp.maximum(-ff, 0.0) + jnp.log1p(jnp.exp(-jnp.abs(ff))))

    @pl.when(i % blocks_per_seq == 0)
    def _():
        carry_ref[...] = jnp.zeros_like(carry_ref)

    ltri = jnp.where(tril, 1.0, 0.0).astype(bf16)
    carry = carry_ref[...]
    for c in range(n_chunks):
        within = sum(_dot(ltri, part) for part in _split_bf16(logf[c * CHUNK:(c + 1) * CHUNK, :], 3))
        cum = within + carry
        carry = cum[CHUNK - 1:CHUNK, :]
        cend_ref[8 * c:8 * (c + 1), :] = jnp.broadcast_to(carry * LOG2E, (8, LANES))
        terms = jnp.concatenate(_split_bf16(cum * LOG2E, C_TERMS), axis=1)
        placed = _dot(terms, place_ref[...]).astype(bf16)
        for p in range(N_PAIRS):
            fk_ref[c * CHUNK:(c + 1) * CHUNK, (2 * p + 1) * LANES:(2 * p + 2) * LANES] = (
                placed[:, p * LANES:(p + 1) * LANES])
    carry_ref[...] = carry


def _placement_matrix():
    place = np.zeros((C_TERMS * LANES, N_PAIRS * LANES), np.float32)
    for h in range(N_HEADS_FOX):
        for term in range(C_TERMS):
            place[term * LANES + h, (h // 2) * LANES + C_TERMS * (h % 2) + term] = 1.0
    return jnp.asarray(place, bf16)


def _proj_in(x2d, g, w_pad, cos4, sin4, sgu_w, sgu_b_full, sgu_vg, fox_b, *, seq, tm):
    t = x2d.shape[0]
    row = lambda width: pl.BlockSpec((tm, width), lambda i: (i, 0))
    out_widths = [W_SGU] + [W_RET] * 4 + [W_FOX, 2 * W_FOX]
    place = _placement_matrix()
    kern = functools.partial(_proj_in_kernel, tm=tm, blocks_per_seq=seq // tm)
    return pl.pallas_call(
        kern,
        grid=(t // tm,),
        in_specs=[row(D_MODEL), _resident((1, D_MODEL)), _resident((D_MODEL, N_IN_PAD)),
                  row(LANES), row(LANES),
                  _resident((N_HEADS_SGU, CHUNK, CHUNK)), _resident((CHUNK, W_SGU)),
                  _resident((1, W_SGU)), _resident((1, LANES)), _resident(place.shape)],
        out_specs=[row(w) for w in out_widths] + [
            pl.BlockSpec((W_FOX, tm), lambda i: (0, i)),
            pl.BlockSpec((8 * tm // CHUNK, LANES), lambda i: (i, 0)),
            pl.BlockSpec((8, LANES), lambda i: (i // (seq // tm), 0))],
        out_shape=[jax.ShapeDtypeStruct((t, w), bf16) for w in out_widths] + [
            jax.ShapeDtypeStruct((W_FOX, t), bf16),
            jax.ShapeDtypeStruct((8 * t // CHUNK, LANES), f32),
            jax.ShapeDtypeStruct((8 * t // seq, LANES), f32)],
        scratch_shapes=[pltpu.VMEM((1, LANES), f32), pltpu.VMEM((tm, W_FOX), f32)],
        compiler_params=_params("arbitrary"),
        name="proj_in",
    )(x2d, g, w_pad, cos4, sin4, sgu_w, sgu_b_full, sgu_vg, fox_b, place)


def _retention_kernel(q_ref, k_ref, v_ref, g_ref, dec_ref, qd_ref, kd_ref, cd_ref, o_ref, state_ref, raw_ref,
                      *, tm, blocks_per_seq):
    i = pl.program_id(0)

    @pl.when(i % blocks_per_seq == 0)
    def _():
        state_ref[...] = jnp.zeros_like(state_ref)

    row = lax.broadcasted_iota(jnp.int32, (CHUNK, LANES), 0)
    lane = lax.broadcasted_iota(jnp.int32, (CHUNK, LANES), 1)
    first_head = lane < HEAD_DIM
    same_head = (row < HEAD_DIM) == first_head
    zero = jnp.zeros((CHUNK, LANES), bf16)
    for c in range(tm // CHUNK):
        rows = slice(c * CHUNK, (c + 1) * CHUNK)
        for p in range(N_PAIRS):
            cols = slice(p * LANES, (p + 1) * LANES)
            q = q_ref[rows, cols]
            k = k_ref[rows, cols]
            v = v_ref[rows, cols]
            q_heads = jnp.concatenate([jnp.where(first_head, q, zero), jnp.where(first_head, zero, q)], axis=0)
            inner = _dot_nt(q_heads, k) * dec_ref[p]
            o_heads = _dot(inner.astype(bf16), v)
            state = state_ref[p]
            o = (jnp.where(first_head, o_heads[:CHUNK], o_heads[CHUNK:])
                 + _dot(q, state.astype(bf16)) * qd_ref[p])
            k_dec = (k.astype(f32) * kd_ref[p]).astype(bf16)
            state_ref[p] = state * cd_ref[p] + jnp.where(same_head, _dot_tn(k_dec, v), 0.0)
            raw_ref[rows, cols] = o
    o = raw_ref[...]
    ro = o * lax.rsqrt(_group_mean(o * o, W_RET) + RMS_EPS)
    gate = g_ref[...].astype(f32)
    o_ref[...] = (gate * jax.nn.sigmoid(gate) * ro).astype(bf16)


def _retention_tables():
    log_g = jnp.log(1.0 - 2.0 ** (-5.0 - jnp.arange(N_HEADS_RET, dtype=f32)))
    idx = jnp.arange(CHUNK, dtype=f32)
    diff = idx[:, None] - idx[None, :]
    decay = jnp.where(diff >= 0, jnp.exp(log_g[:, None, None] * jnp.maximum(diff, 0.0)), 0.0)
    lane_g = jnp.repeat(log_g, HEAD_DIM).reshape(N_PAIRS, 1, LANES)
    q_decay = jnp.exp(lane_g * (idx + 1.0)[None, :, None])
    k_decay = jnp.exp(lane_g * (CHUNK - 1.0 - idx)[None, :, None])
    chunk_decay = jnp.broadcast_to(jnp.exp(lane_g * CHUNK).reshape(N_PAIRS, LANES, 1), (N_PAIRS, LANES, LANES))
    pair_decay = decay.reshape(N_PAIRS, 2 * CHUNK, CHUNK)
    return pair_decay, q_decay, k_decay, chunk_decay


def _retention(rq, rk, rv, rg, *, seq, tm):
    t = rq.shape[0]
    decay, q_decay, k_decay, chunk_decay = _retention_tables()
    row = pl.BlockSpec((tm, W_RET), lambda i: (i, 0))
    kern = functools.partial(_retention_kernel, tm=tm, blocks_per_seq=seq // tm)
    return pl.pallas_call(
        kern,
        grid=(t // tm,),
        in_specs=[row, row, row, row, _resident(decay.shape), _resident(q_decay.shape),
                  _resident(k_decay.shape), _resident(chunk_decay.shape)],
        out_specs=row,
        out_shape=jax.ShapeDtypeStruct((t, W_RET), bf16),
        scratch_shapes=[pltpu.VMEM((N_PAIRS, LANES, LANES), f32), pltpu.VMEM((tm, W_RET), f32)],
        compiler_params=_params("arbitrary"),
        name="retention",
    )(rq, rk, rv, rg, decay, q_decay, k_decay, chunk_decay)


SUM_ROWS = 16


def _fox_kernel(cend_ref, thr_ref, q_ref, k_ref, vt_ref, o_ref, s_ref, m_ref, acc_ref, *, tq, tk, n_chunks):
    b = pl.program_id(0)
    pair = pl.program_id(1)
    i = pl.program_id(2)

    def first_needed_block():
        def scan(j, first):
            needed = False
            for a in range(2):
                head = (b * N_HEADS_FOX + 2 * pair + a)
                base = head * n_chunks
                key_end = cend_ref[base + (tk // CHUNK) * (j + 1) - 1]
                query_start = cend_ref[base + jnp.maximum((tq // CHUNK) * i - 1, 0)]
                needed = jnp.logical_or(needed, key_end - query_start <= thr_ref[head])
            return jnp.where(jnp.logical_and(needed, first == 2 * i), j, first)
        return lax.fori_loop(0, 2 * i, scan, 2 * i)

    first_pair = first_needed_block() // 2
    q = q_ref[...]
    lane = lax.broadcasted_iota(jnp.int32, (tq, LANES), 1)
    first_head = lane < HEAD_DIM
    zero = jnp.zeros((tq, LANES), bf16)
    q_aug = []
    for a in range(2):
        qa = jnp.where(first_head, q, zero) if a == 0 else jnp.where(first_head, zero, q)
        pick = jnp.where((lane >= C_TERMS * a) & (lane < C_TERMS * (a + 1)), -1.0, 0.0).astype(bf16)
        q_aug.append(jnp.concatenate([qa, pick], axis=1))
    m_ref[...] = jnp.full_like(m_ref, MASK_VALUE)
    acc_ref[...] = jnp.zeros_like(acc_ref)
    ones_rows = jnp.ones((SUM_ROWS, tk), bf16)

    def scores(j, slot, lo=0):
        off = pl.multiple_of(j * tk, tk)
        k = k_ref[pl.ds(off, tk), :]
        for a in range(2):
            s_ref[slot, a, :, lo:] = _dot_nt(k, q_aug[a][lo:])

    def consume(j, slot, masked, lo=0):
        off = pl.multiple_of(j * tk, tk)
        for a in range(2):
            s = s_ref[slot, a, :, lo:]
            if masked:
                kpos = off + lax.broadcasted_iota(jnp.int32, (tk, tq - lo), 0)
                qpos = i * tq + lo + lax.broadcasted_iota(jnp.int32, (tk, tq - lo), 1)
                s = jnp.where(kpos <= qpos, s, MASK_VALUE)
            m_old = m_ref[a, :, lo:]
            m_new = jnp.maximum(m_old, jnp.max(s, axis=0, keepdims=True))
            p = jnp.exp2(s - m_new)
            alpha = jnp.exp2(m_old - m_new)
            vt = jnp.concatenate([vt_ref[a * HEAD_DIM:(a + 1) * HEAD_DIM, pl.ds(off, tk)], ones_rows], axis=0)
            acc_ref[a, :, lo:] = alpha * acc_ref[a, :, lo:] + _dot(vt, p.astype(bf16))
            m_ref[a, :, lo:] = m_new

    scores(2 * first_pair, 0)

    def body(jj, carry):
        scores(2 * jj + 1, 1)
        consume(2 * jj, 0, False)
        scores(2 * jj + 2, 0)
        consume(2 * jj + 1, 1, False)
        return carry

    lax.fori_loop(first_pair, i, body, 0)
    scores(2 * i + 1, 1, lo=tk)
    consume(2 * i, 0, True)
    consume(2 * i + 1, 1, True, lo=tk)
    o_t = jnp.concatenate([acc_ref[a, :HEAD_DIM] / acc_ref[a, HEAD_DIM:HEAD_DIM + 1] for a in range(2)], axis=0)
    o_ref[...] = o_t.T.astype(bf16)


EXP2_UNDERFLOW = 160.0
NORM_SLACK = 1.02


def _fox(fq, fk_aug, fv_t, cend, norm2, *, batch, seq, tq):
    t = fq.shape[0]
    nq = seq // tq
    tk = tq // 2
    n_chunks = seq // CHUNK
    cend_flat = cend.reshape(batch, n_chunks, 8, LANES)[:, :, 0, :N_HEADS_FOX].transpose(0, 2, 1).reshape(-1)
    norm2 = norm2.reshape(batch, 8, LANES)[:, 0, :2 * N_HEADS_FOX].reshape(batch, 2, N_HEADS_FOX)
    thr = (2.0 * NORM_SLACK * jnp.sqrt(norm2[:, 0] * norm2[:, 1]) + EXP2_UNDERFLOW).reshape(-1)
    smem = pl.BlockSpec(memory_space=pltpu.SMEM)
    q_spec = pl.BlockSpec((tq, LANES), lambda b, p, i: (b * nq + i, p))
    return pl.pallas_call(
        functools.partial(_fox_kernel, tq=tq, tk=tk, n_chunks=n_chunks),
        grid=(batch, N_PAIRS, nq),
        in_specs=[smem, smem, q_spec,
                  pl.BlockSpec((seq, 2 * LANES), lambda b, p, i: (b, p)),
                  pl.BlockSpec((LANES, seq), lambda b, p, i: (p, b))],
        out_specs=q_spec,
        out_shape=jax.ShapeDtypeStruct((t, W_FOX), bf16),
        scratch_shapes=[pltpu.VMEM((2, 2, tk, tq), f32), pltpu.VMEM((2, 1, tq), f32),
                        pltpu.VMEM((2, HEAD_DIM + SUM_ROWS, tq), f32)],
        compiler_params=_params("parallel", "parallel", "arbitrary"),
        name="fox_attention",
    )(cend_flat, thr, fq, fk_aug, fv_t)


def _channel_kernel(a_ref, b_ref, c_ref, x_ref, p_ref, wo_ref, gmix_ref,
                    gpre_ref, wg_ref, wu_ref, cw_ref, cb_ref, wd_ref, gpost_ref,
                    gple_ref, wpg_ref, wpp_ref, gplepost_ref, o_ref, halo_ref,
                    *, tm, fc, blocks_per_seq):
    i = pl.program_id(0)

    @pl.when(i % blocks_per_seq == 0)
    def _():
        halo_ref[...] = jnp.zeros_like(halo_ref)

    cat = jnp.concatenate([a_ref[...], b_ref[...], c_ref[...]], axis=1)
    xf = x_ref[...] + _rms(_dot(cat, wo_ref[...]), gmix_ref[...])

    h = _rms(xf, gpre_ref[...]).astype(bf16)
    row8 = lax.broadcasted_iota(jnp.int32, (8, fc), 0)
    acc = jnp.zeros((tm, D_MODEL), f32)
    for c in range(D_FF // fc):
        cols = slice(c * fc, (c + 1) * fc)
        g = _dot(h, wg_ref[:, cols])
        up = _dot(h, wu_ref[:, cols])
        prev = halo_ref[:, cols]
        halo_ref[:, cols] = g[tm - 8:, :]
        g1 = pltpu.roll(g, 1, 0)
        g2 = pltpu.roll(g, 2, 0)
        g1 = jnp.concatenate([jnp.where(row8 < 1, pltpu.roll(prev, 1, 0), g1[:8]), g1[8:]], axis=0)
        g2 = jnp.concatenate([jnp.where(row8 < 2, pltpu.roll(prev, 2, 0), g2[:8]), g2[8:]], axis=0)
        conv = cb_ref[:, cols] + g2 * cw_ref[0:1, cols]
        conv = conv + g1 * cw_ref[1:2, cols]
        conv = conv + g * cw_ref[2:3, cols]
        act = _gelu(conv) * up
        acc = acc + _dot(act.astype(bf16), wd_ref[cols, :])
    xf = xf + _rms(acc, gpost_ref[...])

    gate = jax.nn.sigmoid(_dot(_rms(xf, gple_ref[...]).astype(bf16), wpg_ref[...]))
    e = _dot(p_ref[...].astype(bf16), wpp_ref[...])
    o_ref[...] = xf + _rms(e * gate, gplepost_ref[...])


def _channel(out_a, out_b, out_c, x2d, p2d, w_o, g_mix, g_pre, w_gate, w_up, conv_w, conv_b, w_down, g_post,
             g_ple, w_ple_gate, w_ple_proj, g_ple_post, *, seq, tm, fc):
    t = x2d.shape[0]
    row = lambda width: pl.BlockSpec((tm, width), lambda i: (i, 0))
    gain = _resident((1, D_MODEL))
    kern = functools.partial(_channel_kernel, tm=tm, fc=fc, blocks_per_seq=seq // tm)
    return pl.pallas_call(
        kern,
        grid=(t // tm,),
        in_specs=[row(W_SGU), row(W_RET), row(W_FOX), row(D_MODEL), row(D_PLE),
                  _resident((D_MODEL, D_MODEL)), gain,
                  gain, _resident((D_MODEL, D_FF)), _resident((D_MODEL, D_FF)), _resident(conv_w.shape),
                  _resident((1, D_FF)), _resident((D_FF, D_MODEL)), gain,
                  gain, _resident((D_MODEL, D_MODEL)), _resident((D_PLE, D_MODEL)), gain],
        out_specs=row(D_MODEL),
        out_shape=jax.ShapeDtypeStruct((t, D_MODEL), f32),
        scratch_shapes=[pltpu.VMEM((8, D_FF), f32)],
        compiler_params=_params("arbitrary"),
        name="channel",
    )(out_a, out_b, out_c, x2d, p2d, w_o, g_mix, g_pre, w_gate, w_up, conv_w, conv_b, w_down, g_post,
      g_ple, w_ple_gate, w_ple_proj, g_ple_post)


def _largest_tile(seq, cap):
    tile = cap
    while seq % tile:
        tile //= 2
    return tile


def kernel(x, p, positions, mix_pre_g, w_in, sgu_v_g, sgu_w, sgu_b, fox_b_f, w_o, mix_post_g, ffn_pre_g,
           w_gate, w_up, conv_w, conv_b, w_down, ffn_post_g, ple_pre_g, w_ple_gate, w_ple_proj, ple_post_g):
    batch, seq, _ = x.shape
    depth = w_in.shape[0]
    t = batch * seq
    assert seq % CHUNK == 0
    tm = _largest_tile(seq, 512)
    tq = _largest_tile(seq, 1024)
    fc = 1024

    cos4, sin4 = _rope_tables(positions)
    x2d = x.reshape(t, D_MODEL)
    for i in range(depth):
        w_pad = jnp.pad(w_in[i].astype(bf16), ((0, 0), (0, N_IN_PAD - w_in.shape[2])))
        sgu_b_full = jnp.repeat(sgu_b[i].T, HEAD_DIM, axis=1)
        fox_b = jnp.pad(fox_b_f[i], (0, LANES - N_HEADS_FOX))[None, :]
        out_a, rq, rk, rv, rg, fq, fk_aug, fv_t, cend, norm2 = _proj_in(
            x2d, mix_pre_g[i][None, :], w_pad, cos4, sin4, sgu_w[i], sgu_b_full,
            sgu_v_g[i].reshape(1, W_SGU), fox_b, seq=seq, tm=tm)
        out_b = _retention(rq, rk, rv, rg, seq=seq, tm=tm)
        out_c = _fox(fq, fk_aug, fv_t, cend, norm2, batch=batch, seq=seq, tq=tq)
        x2d = _channel(out_a, out_b, out_c, x2d, p[i].reshape(t, D_PLE),
                       w_o[i].astype(bf16), mix_post_g[i][None, :],
                       ffn_pre_g[i][None, :], w_gate[i].astype(bf16), w_up[i].astype(bf16), conv_w[i],
                       conv_b[i][None, :], w_down[i].astype(bf16), ffn_post_g[i][None, :],
                       ple_pre_g[i][None, :], w_ple_gate[i].astype(bf16), w_ple_proj[i].astype(bf16),
                       ple_post_g[i][None, :], seq=seq, tm=tm, fc=fc)
    return x2d.reshape(batch, seq, D_MODEL)
```

```python
import functools

import numpy as np
import jax
import jax.numpy as jnp
from jax import lax
from jax.experimental import pallas as pl
from jax.experimental.pallas import tpu as pltpu

D_MODEL = 1024
D_PLE = 256
HEAD_DIM = 64
N_HEADS_SGU = 4
N_HEADS_RET = 6
N_HEADS_FOX = 6
W_SGU = N_HEADS_SGU * HEAD_DIM
W_RET = N_HEADS_RET * HEAD_DIM
W_FOX = N_HEADS_FOX * HEAD_DIM
CHUNK = 128
D_FF = 4 * D_MODEL
ROPE_BASE = 10000.0
RMS_EPS = 1e-6
LANES = 128
N_PAIRS = W_RET // LANES
N_IN_PAD = 2 * W_SGU + 4 * W_RET + 3 * W_FOX + LANES
MASK_VALUE = -1e30
LOG2E = 1.4426950408889634
C_TERMS = 3
CUMSUM_ROWS = 512
VMEM_LIMIT = 56 * 1024 * 1024

f32 = jnp.float32
bf16 = jnp.bfloat16


def _params(*sem):
    return pltpu.CompilerParams(dimension_semantics=sem, vmem_limit_bytes=VMEM_LIMIT)


def _resident(shape, layer=None):
    zeros = (0,) * len(shape)
    if layer is None:
        return pl.BlockSpec(shape, lambda *_: zeros, pipeline_mode=pl.Buffered(1))
    return pl.BlockSpec((None,) + tuple(shape), lambda *_: (layer,) + zeros, pipeline_mode=pl.Buffered(1))


def _rms(xf, g):
    y = xf * lax.rsqrt(jnp.mean(xf * xf, axis=-1, keepdims=True) + RMS_EPS)
    return y * g


def _dot(a, b):
    return jnp.dot(a, b, preferred_element_type=f32)


def _dot_nt(a, b):
    return lax.dot_general(a, b, (((1,), (1,)), ((), ())), preferred_element_type=f32)


def _dot_tn(a, b):
    return lax.dot_general(a, b, (((0,), (0,)), ((), ())), preferred_element_type=f32)


def _split_bf16(x, terms):
    parts = []
    r = x
    for _ in range(terms):
        p = r.astype(bf16)
        parts.append(p)
        r = r - p.astype(f32)
    return parts


def _group_mean(y, width):
    r = lax.broadcasted_iota(jnp.int32, (width, width), 0) // HEAD_DIM
    c = lax.broadcasted_iota(jnp.int32, (width, width), 1) // HEAD_DIM
    gm = jnp.where(r == c, 1.0 / HEAD_DIM, 0.0).astype(bf16)
    return sum(_dot(p, gm) for p in _split_bf16(y, 2))


def _gelu(x):
    return jax.nn.gelu(x, approximate=True)


def _rope_kernel(pos_ref, invf_ref, cos_ref, sin_ref, nsin_ref):
    ang = pos_ref[...].astype(f32) * invf_ref[...]
    s = jnp.sin(ang)
    cos_ref[...] = jnp.cos(ang)
    sin_ref[...] = s
    nsin_ref[...] = -s


def _rope_tables(positions):
    t = positions.size
    half = HEAD_DIM // 2
    per_row = LANES // half
    rows = t // per_row
    pos = jnp.repeat(positions.reshape(rows, per_row), half, axis=1)
    inv_freq = ROPE_BASE ** (-jnp.arange(half, dtype=f32) / half)
    invf = jnp.tile(inv_freq, per_row)[None, :]
    tr = min(rows, 1024)
    spec = pl.BlockSpec((tr, LANES), lambda i: (i, 0))
    cos, sin, nsin = pl.pallas_call(
        _rope_kernel,
        grid=(rows // tr,),
        in_specs=[spec, _resident((1, LANES))],
        out_specs=[spec, spec, spec],
        out_shape=[jax.ShapeDtypeStruct((rows, LANES), f32)] * 3,
        compiler_params=_params("parallel"),
        name="rope_tables",
    )(pos, invf)
    cos, sin, nsin = (a.reshape(t, half) for a in (cos, sin, nsin))
    return jnp.tile(cos, (1, per_row)), jnp.concatenate([nsin, sin] * (per_row // 2), axis=1)


def _proj_in_kernel(x_ref, g_ref, w_ref, cos_ref, sin_ref, sguw_ref, sgub_ref, vg_ref, fb_ref, place_ref,
                    oa_ref, rq_ref, rk_ref, rv_ref, rg_ref, fq_ref, fk_ref, fvt_ref, cend_ref, nrm_ref,
                    carry_ref, fv_ref, *, tm, blocks_per_seq):
    i = pl.program_id(0)
    n_chunks = tm // CHUNK
    h = _rms(x_ref[...], g_ref[...]).astype(bf16)

    def proj(lo, width):
        return _dot(h, w_ref[:, lo:lo + width])

    z_fox_vf = proj(2 * W_SGU + 4 * W_RET + 2 * W_FOX, W_FOX + LANES)

    ff = z_fox_vf[:, W_FOX:] + fb_ref[...]
    logf = -(jnp.maximum(-ff, 0.0) + jnp.log1p(jnp.exp(-jnp.abs(ff))))

    @pl.when(i % blocks_per_seq == 0)
    def _():
        carry_ref[...] = jnp.zeros_like(carry_ref)

    sub = min(tm, CUMSUM_ROWS)
    sub_tril = (lax.broadcasted_iota(jnp.int32, (sub, sub), 0) >= lax.broadcasted_iota(jnp.int32, (sub, sub), 1))
    sub_tril = jnp.where(sub_tril, 1.0, 0.0).astype(bf16)
    carry = carry_ref[...]
    cums = []
    for r in range(tm // sub):
        within = _dot(sub_tril, jnp.concatenate(_split_bf16(logf[r * sub:(r + 1) * sub], 3), axis=1))
        cums.append(within[:, :LANES] + within[:, LANES:2 * LANES] + within[:, 2 * LANES:] + carry)
        carry = cums[-1][sub - 1:sub, :]
    carry_ref[...] = carry
    cum = jnp.concatenate(cums, axis=0) * LOG2E
    for c in range(n_chunks):
        cend_ref[8 * c:8 * (c + 1), :] = jnp.broadcast_to(cum[(c + 1) * CHUNK - 1:(c + 1) * CHUNK, :], (8, LANES))
    placed = _dot(jnp.concatenate(_split_bf16(cum, C_TERMS), axis=1), place_ref[...]).astype(bf16)
    for p in range(N_PAIRS):
        fk_ref[:, (2 * p + 1) * LANES:(2 * p + 2) * LANES] = placed[:, p * LANES:(p + 1) * LANES]

    sq_row = lax.broadcasted_iota(jnp.int32, (CHUNK, CHUNK), 0)
    sq_col = lax.broadcasted_iota(jnp.int32, (CHUNK, CHUNK), 1)
    tril = sq_row >= sq_col
    first_head = sq_col < HEAD_DIM

    z_sgu = _gelu(proj(0, 2 * W_SGU))
    u = z_sgu[:, :W_SGU]
    v = z_sgu[:, W_SGU:]
    v = v * lax.rsqrt(_group_mean(v * v, W_SGU) + RMS_EPS) * vg_ref[...]
    for p in range(W_SGU // LANES):
        cols = slice(p * LANES, (p + 1) * LANES)
        w_pair = jnp.concatenate(
            [jnp.where(tril, sguw_ref[2 * p], 0.0), jnp.where(tril, sguw_ref[2 * p + 1], 0.0)],
            axis=0).astype(bf16)
        for c in range(n_chunks):
            rows = slice(c * CHUNK, (c + 1) * CHUNK)
            r = _dot(w_pair, v[rows, cols].astype(bf16))
            s = jnp.where(first_head, r[:CHUNK], r[CHUNK:]) + sgub_ref[:, cols]
            oa_ref[rows, cols] = (u[rows, cols] * s).astype(bf16)

    cosv = cos_ref[...]
    sinv = sin_ref[...]
    lane = lax.broadcasted_iota(jnp.int32, (tm, LANES), 1)
    first_half = (lane % HEAD_DIM) < (HEAD_DIM // 2)

    def rotary(z):
        outs = []
        for s in range(N_PAIRS):
            t = z[:, s * LANES:(s + 1) * LANES]
            partner = jnp.where(first_half, pltpu.roll(t, LANES - HEAD_DIM // 2, 1),
                                pltpu.roll(t, HEAD_DIM // 2, 1))
            outs.append(t * cosv + partner * sinv)
        return jnp.concatenate(outs, axis=1)

    base = 2 * W_SGU
    z_qk = proj(base, 2 * W_RET)
    rq_ref[...] = rotary(z_qk[:, :W_RET]).astype(bf16)
    rk_ref[...] = (rotary(z_qk[:, W_RET:]) * (HEAD_DIM ** -0.5)).astype(bf16)
    z_vg = proj(base + 2 * W_RET, 2 * W_RET).astype(bf16)
    rv_ref[...] = z_vg[:, :W_RET]
    rg_ref[...] = z_vg[:, W_RET:]

    base += 4 * W_RET
    z_fox_qk = proj(base, 2 * W_FOX)
    fq = (z_fox_qk[:, :W_FOX] * (LOG2E * HEAD_DIM ** -0.5)).astype(bf16)
    fq_ref[...] = fq
    fk = z_fox_qk[:, W_FOX:].astype(bf16)
    for p in range(N_PAIRS):
        fk_ref[:, 2 * p * LANES:(2 * p + 1) * LANES] = fk[:, p * LANES:(p + 1) * LANES]
    fv_ref[...] = z_fox_vf[:, :W_FOX]
    fvt_ref[...] = fv_ref[...].T.astype(bf16)

    gr = lax.broadcasted_iota(jnp.int32, (2 * W_FOX, LANES), 0) // HEAD_DIM
    gc = lax.broadcasted_iota(jnp.int32, (2 * W_FOX, LANES), 1)
    head_sum = jnp.where(gr == gc, 1.0, 0.0).astype(bf16)
    qk = jnp.concatenate([fq, fk], axis=1).astype(f32)
    norm2 = jnp.max(_dot((qk * qk).astype(bf16), head_sum), axis=0, keepdims=True)

    @pl.when(i % blocks_per_seq == 0)
    def _():
        nrm_ref[...] = jnp.zeros_like(nrm_ref)

    nrm_ref[...] = jnp.maximum(nrm_ref[...], norm2)


def _placement_matrix():
    place = np.zeros((C_TERMS * LANES, N_PAIRS * LANES), np.float32)
    for h in range(N_HEADS_FOX):
        for term in range(C_TERMS):
            place[term * LANES + h, (h // 2) * LANES + C_TERMS * (h % 2) + term] = 1.0
    return jnp.asarray(place, bf16)


def _proj_in(x2d, g, w_pad, cos4, sin4, sgu_w, sgu_b_full, sgu_vg, fox_b, *, layer, seq, tm):
    t = x2d.shape[0]
    row = lambda width: pl.BlockSpec((tm, width), lambda i: (i, 0))
    out_widths = [W_SGU] + [W_RET] * 4 + [W_FOX, 2 * W_FOX]
    place = _placement_matrix()
    kern = functools.partial(_proj_in_kernel, tm=tm, blocks_per_seq=seq // tm)
    return pl.pallas_call(
        kern,
        grid=(t // tm,),
        in_specs=[row(D_MODEL), _resident((1, D_MODEL)), _resident((D_MODEL, N_IN_PAD), layer),
                  row(LANES), row(LANES),
                  _resident((N_HEADS_SGU, CHUNK, CHUNK)), _resident((CHUNK, W_SGU)),
                  _resident((1, W_SGU)), _resident((1, LANES)), _resident(place.shape)],
        out_specs=[row(w) for w in out_widths] + [
            pl.BlockSpec((W_FOX, tm), lambda i: (0, i)),
            pl.BlockSpec((8 * tm // CHUNK, LANES), lambda i: (i, 0)),
            pl.BlockSpec((8, LANES), lambda i: (i // (seq // tm), 0))],
        out_shape=[jax.ShapeDtypeStruct((t, w), bf16) for w in out_widths] + [
            jax.ShapeDtypeStruct((W_FOX, t), bf16),
            jax.ShapeDtypeStruct((8 * t // CHUNK, LANES), f32),
            jax.ShapeDtypeStruct((8 * t // seq, LANES), f32)],
        scratch_shapes=[pltpu.VMEM((1, LANES), f32), pltpu.VMEM((tm, W_FOX), f32)],
        compiler_params=_params("arbitrary"),
        name="proj_in",
    )(x2d, g, w_pad, cos4, sin4, sgu_w, sgu_b_full, sgu_vg, fox_b, place)


def _retention_kernel(q_ref, k_ref, v_ref, g_ref, dec_ref, qd_ref, kd_ref, cd_ref, o_ref, state_ref, raw_ref,
                      *, tm, blocks_per_seq):
    i = pl.program_id(0)

    @pl.when(i % blocks_per_seq == 0)
    def _():
        state_ref[...] = jnp.zeros_like(state_ref)

    row = lax.broadcasted_iota(jnp.int32, (CHUNK, LANES), 0)
    lane = lax.broadcasted_iota(jnp.int32, (CHUNK, LANES), 1)
    first_head = lane < HEAD_DIM
    same_head = (row < HEAD_DIM) == first_head
    zero = jnp.zeros((CHUNK, LANES), bf16)
    for c in range(tm // CHUNK):
        rows = slice(c * CHUNK, (c + 1) * CHUNK)
        for p in range(N_PAIRS):
            cols = slice(p * LANES, (p + 1) * LANES)
            q = q_ref[rows, cols]
            k = k_ref[rows, cols]
            v = v_ref[rows, cols]
            q_heads = jnp.concatenate([jnp.where(first_head, q, zero), jnp.where(first_head, zero, q)], axis=0)
            inner = _dot_nt(q_heads, k) * dec_ref[p]
            o_heads = _dot(inner.astype(bf16), v)
            state = state_ref[p]
            o = (jnp.where(first_head, o_heads[:CHUNK], o_heads[CHUNK:])
                 + _dot(q, state.astype(bf16)) * qd_ref[p])
            k_dec = (k.astype(f32) * kd_ref[p]).astype(bf16)
            state_ref[p] = state * cd_ref[p] + jnp.where(same_head, _dot_tn(k_dec, v), 0.0)
            raw_ref[rows, cols] = o
    o = raw_ref[...]
    ro = o * lax.rsqrt(_group_mean(o * o, W_RET) + RMS_EPS)
    gate = g_ref[...].astype(f32)
    o_ref[...] = (gate * jax.nn.sigmoid(gate) * ro).astype(bf16)


def _retention_tables():
    log_g = jnp.log(1.0 - 2.0 ** (-5.0 - jnp.arange(N_HEADS_RET, dtype=f32)))
    idx = jnp.arange(CHUNK, dtype=f32)
    diff = idx[:, None] - idx[None, :]
    decay = jnp.where(diff >= 0, jnp.exp(log_g[:, None, None] * jnp.maximum(diff, 0.0)), 0.0)
    lane_g = jnp.repeat(log_g, HEAD_DIM).reshape(N_PAIRS, 1, LANES)
    q_decay = jnp.exp(lane_g * (idx + 1.0)[None, :, None])
    k_decay = jnp.exp(lane_g * (CHUNK - 1.0 - idx)[None, :, None])
    chunk_decay = jnp.broadcast_to(jnp.exp(lane_g * CHUNK).reshape(N_PAIRS, LANES, 1), (N_PAIRS, LANES, LANES))
    pair_decay = decay.reshape(N_PAIRS, 2 * CHUNK, CHUNK)
    return pair_decay, q_decay, k_decay, chunk_decay


def _retention(rq, rk, rv, rg, *, seq, tm):
    t = rq.shape[0]
    decay, q_decay, k_decay, chunk_decay = _retention_tables()
    row = pl.BlockSpec((tm, W_RET), lambda i: (i, 0))
    kern = functools.partial(_retention_kernel, tm=tm, blocks_per_seq=seq // tm)
    return pl.pallas_call(
        kern,
        grid=(t // tm,),
        in_specs=[row, row, row, row, _resident(decay.shape), _resident(q_decay.shape),
                  _resident(k_decay.shape), _resident(chunk_decay.shape)],
        out_specs=row,
        out_shape=jax.ShapeDtypeStruct((t, W_RET), bf16),
        scratch_shapes=[pltpu.VMEM((N_PAIRS, LANES, LANES), f32), pltpu.VMEM((tm, W_RET), f32)],
        compiler_params=_params("arbitrary"),
        name="retention",
    )(rq, rk, rv, rg, decay, q_decay, k_decay, chunk_decay)


SUM_ROWS = 16


def _fox_kernel(cend_ref, thr_ref, q_ref, k_ref, vt_ref, o_ref, s_ref, m_ref, acc_ref, *, tq, tk, n_chunks):
    b = pl.program_id(0)
    pair = pl.program_id(1)
    i = pl.program_id(2)

    def first_needed_block():
        def scan(j, first):
            needed = False
            for a in range(2):
                head = (b * N_HEADS_FOX + 2 * pair + a)
                base = head * n_chunks
                key_end = cend_ref[base + (tk // CHUNK) * (j + 1) - 1]
                query_start = cend_ref[base + jnp.maximum((tq // CHUNK) * i - 1, 0)]
                needed = jnp.logical_or(needed, key_end - query_start <= thr_ref[head])
            return jnp.where(jnp.logical_and(needed, first == 2 * i), j, first)
        return lax.fori_loop(0, 2 * i, scan, 2 * i)

    first_pair = first_needed_block() // 2
    q = q_ref[...]
    lane = lax.broadcasted_iota(jnp.int32, (tq, LANES), 1)
    first_head = lane < HEAD_DIM
    zero = jnp.zeros((tq, LANES), bf16)
    q_aug = []
    for a in range(2):
        qa = jnp.where(first_head, q, zero) if a == 0 else jnp.where(first_head, zero, q)
        pick = jnp.where((lane >= C_TERMS * a) & (lane < C_TERMS * (a + 1)), -1.0, 0.0).astype(bf16)
        q_aug.append(jnp.concatenate([qa, pick], axis=1))
    m_ref[...] = jnp.full_like(m_ref, MASK_VALUE)
    acc_ref[...] = jnp.zeros_like(acc_ref)
    ones_rows = jnp.ones((SUM_ROWS, tk), bf16)

    def scores(j, slot, lo=0):
        off = pl.multiple_of(j * tk, tk)
        k = k_ref[pl.ds(off, tk), :]
        for a in range(2):
            s_ref[slot, a, :, lo:] = _dot_nt(k, q_aug[a][lo:])

    def consume(j, slot, masked, lo=0):
        off = pl.multiple_of(j * tk, tk)
        for a in range(2):
            s = s_ref[slot, a, :, lo:]
            if masked:
                kpos = off + lax.broadcasted_iota(jnp.int32, (tk, tq - lo), 0)
                qpos = i * tq + lo + lax.broadcasted_iota(jnp.int32, (tk, tq - lo), 1)
                s = jnp.where(kpos <= qpos, s, MASK_VALUE)
            m_old = m_ref[a, :, lo:]
            m_new = jnp.maximum(m_old, jnp.max(s, axis=0, keepdims=True))
            p = jnp.exp2(s - m_new)
            alpha = jnp.exp2(m_old - m_new)
            vt = jnp.concatenate([vt_ref[a * HEAD_DIM:(a + 1) * HEAD_DIM, pl.ds(off, tk)], ones_rows], axis=0)
            acc_ref[a, :, lo:] = alpha * acc_ref[a, :, lo:] + _dot(vt, p.astype(bf16))
            m_ref[a, :, lo:] = m_new

    scores(2 * first_pair, 0)

    def body(jj, carry):
        scores(2 * jj + 1, 1)
        consume(2 * jj, 0, False)
        scores(2 * jj + 2, 0)
        consume(2 * jj + 1, 1, False)
        return carry

    lax.fori_loop(first_pair, i, body, 0)
    scores(2 * i + 1, 1, lo=tk)
    consume(2 * i, 0, True)
    consume(2 * i + 1, 1, True, lo=tk)
    o_t = jnp.concatenate([acc_ref[a, :HEAD_DIM] / acc_ref[a, HEAD_DIM:HEAD_DIM + 1] for a in range(2)], axis=0)
    o_ref[...] = o_t.T.astype(bf16)


EXP2_UNDERFLOW = 160.0
NORM_SLACK = 1.02


def _fox(fq, fk_aug, fv_t, cend, norm2, *, batch, seq, tq):
    t = fq.shape[0]
    nq = seq // tq
    tk = tq // 2
    n_chunks = seq // CHUNK
    cend_flat = cend.reshape(batch, n_chunks, 8, LANES)[:, :, 0, :N_HEADS_FOX].transpose(0, 2, 1).reshape(-1)
    norm2 = norm2.reshape(batch, 8, LANES)[:, 0, :2 * N_HEADS_FOX].reshape(batch, 2, N_HEADS_FOX)
    thr = (2.0 * NORM_SLACK * jnp.sqrt(norm2[:, 0] * norm2[:, 1]) + EXP2_UNDERFLOW).reshape(-1)
    smem = pl.BlockSpec(memory_space=pltpu.SMEM)
    q_spec = pl.BlockSpec((tq, LANES), lambda b, p, i: (b * nq + i, p))
    return pl.pallas_call(
        functools.partial(_fox_kernel, tq=tq, tk=tk, n_chunks=n_chunks),
        grid=(batch, N_PAIRS, nq),
        in_specs=[smem, smem, q_spec,
                  pl.BlockSpec((seq, 2 * LANES), lambda b, p, i: (b, p)),
                  pl.BlockSpec((LANES, seq), lambda b, p, i: (p, b))],
        out_specs=q_spec,
        out_shape=jax.ShapeDtypeStruct((t, W_FOX), bf16),
        scratch_shapes=[pltpu.VMEM((2, 2, tk, tq), f32), pltpu.VMEM((2, 1, tq), f32),
                        pltpu.VMEM((2, HEAD_DIM + SUM_ROWS, tq), f32)],
        compiler_params=_params("parallel", "parallel", "arbitrary"),
        name="fox_attention",
    )(cend_flat, thr, fq, fk_aug, fv_t)


def _channel_kernel(a_ref, b_ref, c_ref, x_ref, p_ref, wo_ref, gmix_ref,
                    gpre_ref, wg_ref, wu_ref, cw_ref, cb_ref, wd_ref, gpost_ref,
                    gple_ref, wpg_ref, wpp_ref, gplepost_ref, o_ref, halo_ref,
                    *, tm, fc, blocks_per_seq):
    i = pl.program_id(0)

    @pl.when(i % blocks_per_seq == 0)
    def _():
        halo_ref[...] = jnp.zeros_like(halo_ref)

    cat = jnp.concatenate([a_ref[...], b_ref[...], c_ref[...]], axis=1)
    xf = x_ref[...] + _rms(_dot(cat, wo_ref[...]), gmix_ref[...])

    h = _rms(xf, gpre_ref[...]).astype(bf16)
    row8 = lax.broadcasted_iota(jnp.int32, (8, fc), 0)
    acc = jnp.zeros((tm, D_MODEL), f32)
    for c in range(D_FF // fc):
        cols = slice(c * fc, (c + 1) * fc)
        g = _dot(h, wg_ref[:, cols])
        up = _dot(h, wu_ref[:, cols])
        prev = halo_ref[:, cols]
        halo_ref[:, cols] = g[tm - 8:, :]
        g1 = pltpu.roll(g, 1, 0)
        g2 = pltpu.roll(g, 2, 0)
        g1 = jnp.concatenate([jnp.where(row8 < 1, pltpu.roll(prev, 1, 0), g1[:8]), g1[8:]], axis=0)
        g2 = jnp.concatenate([jnp.where(row8 < 2, pltpu.roll(prev, 2, 0), g2[:8]), g2[8:]], axis=0)
        conv = cb_ref[:, cols] + g2 * cw_ref[0:1, cols]
        conv = conv + g1 * cw_ref[1:2, cols]
        conv = conv + g * cw_ref[2:3, cols]
        act = _gelu(conv) * up
        acc = acc + _dot(act.astype(bf16), wd_ref[cols, :])
    xf = xf + _rms(acc, gpost_ref[...])

    gate = jax.nn.sigmoid(_dot(_rms(xf, gple_ref[...]).astype(bf16), wpg_ref[...]))
    e = _dot(p_ref[...].astype(bf16), wpp_ref[...])
    o_ref[...] = xf + _rms(e * gate, gplepost_ref[...])


def _channel(out_a, out_b, out_c, x2d, p2d, w_o, g_mix, g_pre, w_gate, w_up, conv_w, conv_b, w_down, g_post,
             g_ple, w_ple_gate, w_ple_proj, g_ple_post, *, layer, seq, tm, fc):
    t = x2d.shape[0]
    row = lambda width: pl.BlockSpec((tm, width), lambda i: (i, 0))
    gain = _resident((1, D_MODEL))
    kern = functools.partial(_channel_kernel, tm=tm, fc=fc, blocks_per_seq=seq // tm)
    return pl.pallas_call(
        kern,
        grid=(t // tm,),
        in_specs=[row(W_SGU), row(W_RET), row(W_FOX), row(D_MODEL),
                  pl.BlockSpec((tm, D_PLE), lambda i: (layer * (t // tm) + i, 0)),
                  _resident((D_MODEL, D_MODEL), layer), gain,
                  gain, _resident((D_MODEL, D_FF), layer), _resident((D_MODEL, D_FF), layer),
                  _resident(conv_w.shape), _resident((1, D_FF)), _resident((D_FF, D_MODEL), layer), gain,
                  gain, _resident((D_MODEL, D_MODEL), layer), _resident((D_PLE, D_MODEL), layer), gain],
        out_specs=row(D_MODEL),
        out_shape=jax.ShapeDtypeStruct((t, D_MODEL), f32),
        scratch_shapes=[pltpu.VMEM((8, D_FF), f32)],
        compiler_params=_params("arbitrary"),
        name="channel",
    )(out_a, out_b, out_c, x2d, p2d, w_o, g_mix, g_pre, w_gate, w_up, conv_w, conv_b, w_down, g_post,
      g_ple, w_ple_gate, w_ple_proj, g_ple_post)


def _largest_tile(seq, cap):
    tile = cap
    while seq % tile:
        tile //= 2
    return tile


def kernel(x, p, positions, mix_pre_g, w_in, sgu_v_g, sgu_w, sgu_b, fox_b_f, w_o, mix_post_g, ffn_pre_g,
           w_gate, w_up, conv_w, conv_b, w_down, ffn_post_g, ple_pre_g, w_ple_gate, w_ple_proj, ple_post_g):
    batch, seq, _ = x.shape
    depth = w_in.shape[0]
    t = batch * seq
    assert seq % CHUNK == 0
    tm = _largest_tile(seq, 512)
    tm_in = _largest_tile(seq, 1024)
    tq = _largest_tile(seq, 1024)
    fc = 1024

    cos4, sin4 = _rope_tables(positions)
    x2d = x.reshape(t, D_MODEL)
    p2d = p.reshape(depth * t, D_PLE)
    w_pad = jnp.pad(w_in.astype(bf16), ((0, 0), (0, 0), (0, N_IN_PAD - w_in.shape[2])))
    w_o, w_gate, w_up, w_down, w_ple_gate, w_ple_proj = (
        w.astype(bf16) for w in (w_o, w_gate, w_up, w_down, w_ple_gate, w_ple_proj))
    for i in range(depth):
        sgu_b_full = jnp.repeat(sgu_b[i].T, HEAD_DIM, axis=1)
        fox_b = jnp.pad(fox_b_f[i], (0, LANES - N_HEADS_FOX))[None, :]
        out_a, rq, rk, rv, rg, fq, fk_aug, fv_t, cend, norm2 = _proj_in(
            x2d, mix_pre_g[i][None, :], w_pad, cos4, sin4, sgu_w[i], sgu_b_full,
            sgu_v_g[i].reshape(1, W_SGU), fox_b, layer=i, seq=seq, tm=tm_in)
        out_b = _retention(rq, rk, rv, rg, seq=seq, tm=tm)
        out_c = _fox(fq, fk_aug, fv_t, cend, norm2, batch=batch, seq=seq, tq=tq)
        x2d = _channel(out_a, out_b, out_c, x2d, p2d, w_o, mix_post_g[i][None, :],
                       ffn_pre_g[i][None, :], w_gate, w_up, conv_w[i], conv_b[i][None, :], w_down,
                       ffn_post_g[i][None, :], ple_pre_g[i][None, :], w_ple_gate, w_ple_proj,
                       ple_post_g[i][None, :], layer=i, seq=seq, tm=tm, fc=fc)
    return x2d.reshape(batch, seq, D_MODEL)
```

```python
import functools

import numpy as np
import jax
import jax.numpy as jnp
from jax import lax
from jax.experimental import pallas as pl
from jax.experimental.pallas import tpu as pltpu

D_MODEL = 1024
D_PLE = 256
HEAD_DIM = 64
N_HEADS_SGU = 4
N_HEADS_RET = 6
N_HEADS_FOX = 6
W_SGU = N_HEADS_SGU * HEAD_DIM
W_RET = N_HEADS_RET * HEAD_DIM
W_FOX = N_HEADS_FOX * HEAD_DIM
CHUNK = 128
D_FF = 4 * D_MODEL
ROPE_BASE = 10000.0
RMS_EPS = 1e-6
LANES = 128
N_PAIRS = W_RET // LANES
N_IN_PAD = 2 * W_SGU + 4 * W_RET + 3 * W_FOX + LANES
MASK_VALUE = -1e30
LOG2E = 1.4426950408889634
C_TERMS = 3
CUMSUM_ROWS = 512
VMEM_LIMIT = 56 * 1024 * 1024

f32 = jnp.float32
bf16 = jnp.bfloat16


def _params(*sem):
    return pltpu.CompilerParams(dimension_semantics=sem, vmem_limit_bytes=VMEM_LIMIT)


def _resident(shape, layer=None):
    zeros = (0,) * len(shape)
    if layer is None:
        return pl.BlockSpec(shape, lambda *_: zeros, pipeline_mode=pl.Buffered(1))
    return pl.BlockSpec((None,) + tuple(shape), lambda *_: (layer,) + zeros, pipeline_mode=pl.Buffered(1))


def _rms(xf, g):
    y = xf * lax.rsqrt(jnp.mean(xf * xf, axis=-1, keepdims=True) + RMS_EPS)
    return y * g


def _dot(a, b):
    return jnp.dot(a, b, preferred_element_type=f32)


def _dot_nt(a, b):
    return lax.dot_general(a, b, (((1,), (1,)), ((), ())), preferred_element_type=f32)


def _dot_tn(a, b):
    return lax.dot_general(a, b, (((0,), (0,)), ((), ())), preferred_element_type=f32)


def _split_bf16(x, terms):
    parts = []
    r = x
    for _ in range(terms):
        p = r.astype(bf16)
        parts.append(p)
        r = r - p.astype(f32)
    return parts


def _group_mean(y, width):
    r = lax.broadcasted_iota(jnp.int32, (width, width), 0) // HEAD_DIM
    c = lax.broadcasted_iota(jnp.int32, (width, width), 1) // HEAD_DIM
    gm = jnp.where(r == c, 1.0 / HEAD_DIM, 0.0).astype(bf16)
    return sum(_dot(p, gm) for p in _split_bf16(y, 2))


def _gelu(x):
    return jax.nn.gelu(x, approximate=True)


ROPE_HALF = HEAD_DIM // 2
ROPE_GROUPS = LANES // ROPE_HALF


def _rope_kernel(pos_ref, invf_ref, cos_ref, sin_ref):
    ang = pos_ref[...].astype(f32) * invf_ref[...]
    c = jnp.cos(ang)
    s = jnp.sin(ang)
    lane = lax.broadcasted_iota(jnp.int32, ang.shape, 1)
    first_group = lane < ROPE_HALF
    sign = jnp.where((lane % HEAD_DIM) < ROPE_HALF, -1.0, 1.0)

    def spread(v, j):
        v = pltpu.roll(v, LANES - ROPE_HALF * j, 1) if j else v
        v = jnp.where(first_group, v, 0.0)
        v = v + pltpu.roll(v, ROPE_HALF, 1)
        return v + pltpu.roll(v, 2 * ROPE_HALF, 1)

    for j in range(ROPE_GROUPS):
        cos_ref[j] = spread(c, j)
        sin_ref[j] = spread(s, j) * sign


def _rope_tables(positions):
    t = positions.size
    rows = t // ROPE_GROUPS
    pos = jnp.repeat(positions.reshape(ROPE_GROUPS, rows).T, ROPE_HALF, axis=1)
    inv_freq = ROPE_BASE ** (-jnp.arange(ROPE_HALF, dtype=f32) / ROPE_HALF)
    invf = jnp.tile(inv_freq, ROPE_GROUPS)[None, :]
    tr = min(rows, 1024)
    out_spec = pl.BlockSpec((ROPE_GROUPS, tr, LANES), lambda i: (0, i, 0))
    cos, sin = pl.pallas_call(
        _rope_kernel,
        grid=(rows // tr,),
        in_specs=[pl.BlockSpec((tr, LANES), lambda i: (i, 0)), _resident((1, LANES))],
        out_specs=[out_spec, out_spec],
        out_shape=[jax.ShapeDtypeStruct((ROPE_GROUPS, rows, LANES), f32)] * 2,
        compiler_params=_params("parallel"),
        name="rope_tables",
    )(pos, invf)
    return cos.reshape(t, LANES), sin.reshape(t, LANES)


def _proj_in_kernel(x_ref, g_ref, w_ref, cos_ref, sin_ref, sguw_ref, sgub_ref, vg_ref, fb_ref, place_ref,
                    oa_ref, rq_ref, rk_ref, rv_ref, rg_ref, fq_ref, fk_ref, fvt_ref, cend_ref, nrm_ref,
                    carry_ref, fv_ref, *, tm, blocks_per_seq):
    i = pl.program_id(0)
    n_chunks = tm // CHUNK
    h = _rms(x_ref[...], g_ref[...]).astype(bf16)

    def proj(lo, width):
        return _dot(h, w_ref[:, lo:lo + width])

    z_fox_vf = proj(2 * W_SGU + 4 * W_RET + 2 * W_FOX, W_FOX + LANES)

    ff = z_fox_vf[:, W_FOX:] + fb_ref[...]
    logf = -(jnp.maximum(-ff, 0.0) + jnp.log1p(jnp.exp(-jnp.abs(ff))))

    @pl.when(i % blocks_per_seq == 0)
    def _():
        carry_ref[...] = jnp.zeros_like(carry_ref)

    sub = min(tm, CUMSUM_ROWS)
    sub_tril = (lax.broadcasted_iota(jnp.int32, (sub, sub), 0) >= lax.broadcasted_iota(jnp.int32, (sub, sub), 1))
    sub_tril = jnp.where(sub_tril, 1.0, 0.0).astype(bf16)
    carry = carry_ref[...]
    cums = []
    for r in range(tm // sub):
        within = _dot(sub_tril, jnp.concatenate(_split_bf16(logf[r * sub:(r + 1) * sub], 3), axis=1))
        cums.append(within[:, :LANES] + within[:, LANES:2 * LANES] + within[:, 2 * LANES:] + carry)
        carry = cums[-1][sub - 1:sub, :]
    carry_ref[...] = carry
    cum = jnp.concatenate(cums, axis=0) * LOG2E
    for c in range(n_chunks):
        cend_ref[8 * c:8 * (c + 1), :] = jnp.broadcast_to(cum[(c + 1) * CHUNK - 1:(c + 1) * CHUNK, :], (8, LANES))
    placed = _dot(jnp.concatenate(_split_bf16(cum, C_TERMS), axis=1), place_ref[...]).astype(bf16)
    for p in range(N_PAIRS):
        fk_ref[:, (2 * p + 1) * LANES:(2 * p + 2) * LANES] = placed[:, p * LANES:(p + 1) * LANES]

    sq_row = lax.broadcasted_iota(jnp.int32, (CHUNK, CHUNK), 0)
    sq_col = lax.broadcasted_iota(jnp.int32, (CHUNK, CHUNK), 1)
    tril = sq_row >= sq_col
    first_head = sq_col < HEAD_DIM

    z_sgu = _gelu(proj(0, 2 * W_SGU))
    u = z_sgu[:, :W_SGU]
    v = z_sgu[:, W_SGU:]
    v = v * lax.rsqrt(_group_mean(v * v, W_SGU) + RMS_EPS) * vg_ref[...]
    for p in range(W_SGU // LANES):
        cols = slice(p * LANES, (p + 1) * LANES)
        w_pair = jnp.concatenate(
            [jnp.where(tril, sguw_ref[2 * p], 0.0), jnp.where(tril, sguw_ref[2 * p + 1], 0.0)],
            axis=0).astype(bf16)
        for c in range(n_chunks):
            rows = slice(c * CHUNK, (c + 1) * CHUNK)
            r = _dot(w_pair, v[rows, cols].astype(bf16))
            s = jnp.where(first_head, r[:CHUNK], r[CHUNK:]) + sgub_ref[:, cols]
            oa_ref[rows, cols] = (u[rows, cols] * s).astype(bf16)

    cosv = cos_ref[...]
    sinv = sin_ref[...]
    lane = lax.broadcasted_iota(jnp.int32, (tm, LANES), 1)
    first_half = (lane % HEAD_DIM) < (HEAD_DIM // 2)

    def rotary(z):
        outs = []
        for s in range(N_PAIRS):
            t = z[:, s * LANES:(s + 1) * LANES]
            partner = jnp.where(first_half, pltpu.roll(t, LANES - HEAD_DIM // 2, 1),
                                pltpu.roll(t, HEAD_DIM // 2, 1))
            outs.append(t * cosv + partner * sinv)
        return jnp.concatenate(outs, axis=1)

    base = 2 * W_SGU
    z_qk = proj(base, 2 * W_RET)
    rq_ref[...] = rotary(z_qk[:, :W_RET]).astype(bf16)
    rk_ref[...] = (rotary(z_qk[:, W_RET:]) * (HEAD_DIM ** -0.5)).astype(bf16)
    z_vg = proj(base + 2 * W_RET, 2 * W_RET).astype(bf16)
    rv_ref[...] = z_vg[:, :W_RET]
    rg_ref[...] = z_vg[:, W_RET:]

    base += 4 * W_RET
    z_fox_qk = proj(base, 2 * W_FOX)
    fq = (z_fox_qk[:, :W_FOX] * (LOG2E * HEAD_DIM ** -0.5)).astype(bf16)
    fq_ref[...] = fq
    fk = z_fox_qk[:, W_FOX:].astype(bf16)
    for p in range(N_PAIRS):
        fk_ref[:, 2 * p * LANES:(2 * p + 1) * LANES] = fk[:, p * LANES:(p + 1) * LANES]
    fv_ref[...] = z_fox_vf[:, :W_FOX]
    fvt_ref[...] = fv_ref[...].T.astype(bf16)

    gr = lax.broadcasted_iota(jnp.int32, (2 * W_FOX, LANES), 0) // HEAD_DIM
    gc = lax.broadcasted_iota(jnp.int32, (2 * W_FOX, LANES), 1)
    head_sum = jnp.where(gr == gc, 1.0, 0.0).astype(bf16)
    qk = jnp.concatenate([fq, fk], axis=1).astype(f32)
    norm2 = jnp.max(_dot((qk * qk).astype(bf16), head_sum), axis=0, keepdims=True)

    @pl.when(i % blocks_per_seq == 0)
    def _():
        nrm_ref[...] = jnp.zeros_like(nrm_ref)

    nrm_ref[...] = jnp.maximum(nrm_ref[...], norm2)


def _placement_matrix():
    place = np.zeros((C_TERMS * LANES, N_PAIRS * LANES), np.float32)
    for h in range(N_HEADS_FOX):
        for term in range(C_TERMS):
            place[term * LANES + h, (h // 2) * LANES + C_TERMS * (h % 2) + term] = 1.0
    return jnp.asarray(place, bf16)


def _proj_in(x2d, g, w_pad, cos4, sin4, sgu_w, sgu_b_full, sgu_vg, fox_b, *, layer, seq, tm):
    t = x2d.shape[0]
    row = lambda width: pl.BlockSpec((tm, width), lambda i: (i, 0))
    out_widths = [W_SGU] + [W_RET] * 4 + [W_FOX, 2 * W_FOX]
    place = _placement_matrix()
    kern = functools.partial(_proj_in_kernel, tm=tm, blocks_per_seq=seq // tm)
    return pl.pallas_call(
        kern,
        grid=(t // tm,),
        in_specs=[row(D_MODEL), _resident((1, D_MODEL)), _resident((D_MODEL, N_IN_PAD), layer),
                  row(LANES), row(LANES),
                  _resident((N_HEADS_SGU, CHUNK, CHUNK)), _resident((CHUNK, W_SGU)),
                  _resident((1, W_SGU)), _resident((1, LANES)), _resident(place.shape)],
        out_specs=[row(w) for w in out_widths] + [
            pl.BlockSpec((W_FOX, tm), lambda i: (0, i)),
            pl.BlockSpec((8 * tm // CHUNK, LANES), lambda i: (i, 0)),
            pl.BlockSpec((8, LANES), lambda i: (i // (seq // tm), 0))],
        out_shape=[jax.ShapeDtypeStruct((t, w), bf16) for w in out_widths] + [
            jax.ShapeDtypeStruct((W_FOX, t), bf16),
            jax.ShapeDtypeStruct((8 * t // CHUNK, LANES), f32),
            jax.ShapeDtypeStruct((8 * t // seq, LANES), f32)],
        scratch_shapes=[pltpu.VMEM((1, LANES), f32), pltpu.VMEM((tm, W_FOX), f32)],
        compiler_params=_params("arbitrary"),
        name="proj_in",
    )(x2d, g, w_pad, cos4, sin4, sgu_w, sgu_b_full, sgu_vg, fox_b, place)


def _retention_kernel(q_ref, k_ref, v_ref, g_ref, dec_ref, qd_ref, kd_ref, cd_ref, o_ref, state_ref, raw_ref,
                      *, tm, blocks_per_seq):
    i = pl.program_id(0)

    @pl.when(i % blocks_per_seq == 0)
    def _():
        state_ref[...] = jnp.zeros_like(state_ref)

    row = lax.broadcasted_iota(jnp.int32, (CHUNK, LANES), 0)
    lane = lax.broadcasted_iota(jnp.int32, (CHUNK, LANES), 1)
    first_head = lane < HEAD_DIM
    same_head = (row < HEAD_DIM) == first_head
    zero = jnp.zeros((CHUNK, LANES), bf16)
    for c in range(tm // CHUNK):
        rows = slice(c * CHUNK, (c + 1) * CHUNK)
        for p in range(N_PAIRS):
            cols = slice(p * LANES, (p + 1) * LANES)
            q = q_ref[rows, cols]
            k = k_ref[rows, cols]
            v = v_ref[rows, cols]
            q_heads = jnp.concatenate([jnp.where(first_head, q, zero), jnp.where(first_head, zero, q)], axis=0)
            inner = _dot_nt(q_heads, k) * dec_ref[p]
            o_heads = _dot(inner.astype(bf16), v)
            state = state_ref[p]
            o = (jnp.where(first_head, o_heads[:CHUNK], o_heads[CHUNK:])
                 + _dot(q, state.astype(bf16)) * qd_ref[p])
            k_dec = (k.astype(f32) * kd_ref[p]).astype(bf16)
            state_ref[p] = state * cd_ref[p] + jnp.where(same_head, _dot_tn(k_dec, v), 0.0)
            raw_ref[rows, cols] = o
    o = raw_ref[...]
    ro = o * lax.rsqrt(_group_mean(o * o, W_RET) + RMS_EPS)
    gate = g_ref[...].astype(f32)
    o_ref[...] = (gate * jax.nn.sigmoid(gate) * ro).astype(bf16)


def _retention_tables():
    log_g = jnp.log(1.0 - 2.0 ** (-5.0 - jnp.arange(N_HEADS_RET, dtype=f32)))
    idx = jnp.arange(CHUNK, dtype=f32)
    diff = idx[:, None] - idx[None, :]
    decay = jnp.where(diff >= 0, jnp.exp(log_g[:, None, None] * jnp.maximum(diff, 0.0)), 0.0)
    lane_g = jnp.repeat(log_g, HEAD_DIM).reshape(N_PAIRS, 1, LANES)
    q_decay = jnp.exp(lane_g * (idx + 1.0)[None, :, None])
    k_decay = jnp.exp(lane_g * (CHUNK - 1.0 - idx)[None, :, None])
    chunk_decay = jnp.broadcast_to(jnp.exp(lane_g * CHUNK).reshape(N_PAIRS, LANES, 1), (N_PAIRS, LANES, LANES))
    pair_decay = decay.reshape(N_PAIRS, 2 * CHUNK, CHUNK)
    return pair_decay, q_decay, k_decay, chunk_decay


def _retention(rq, rk, rv, rg, *, seq, tm):
    t = rq.shape[0]
    decay, q_decay, k_decay, chunk_decay = _retention_tables()
    row = pl.BlockSpec((tm, W_RET), lambda i: (i, 0))
    kern = functools.partial(_retention_kernel, tm=tm, blocks_per_seq=seq // tm)
    return pl.pallas_call(
        kern,
        grid=(t // tm,),
        in_specs=[row, row, row, row, _resident(decay.shape), _resident(q_decay.shape),
                  _resident(k_decay.shape), _resident(chunk_decay.shape)],
        out_specs=row,
        out_shape=jax.ShapeDtypeStruct((t, W_RET), bf16),
        scratch_shapes=[pltpu.VMEM((N_PAIRS, LANES, LANES), f32), pltpu.VMEM((tm, W_RET), f32)],
        compiler_params=_params("arbitrary"),
        name="retention",
    )(rq, rk, rv, rg, decay, q_decay, k_decay, chunk_decay)


SUM_ROWS = 16


def _fox_kernel(cend_ref, thr_ref, q_ref, k_ref, vt_ref, o_ref, s_ref, m_ref, acc_ref, *, tq, tk, n_chunks):
    b = pl.program_id(0)
    pair = pl.program_id(1)
    i = pl.program_id(2)

    def first_needed_block():
        def scan(j, first):
            needed = False
            for a in range(2):
                head = (b * N_HEADS_FOX + 2 * pair + a)
                base = head * n_chunks
                key_end = cend_ref[base + (tk // CHUNK) * (j + 1) - 1]
                query_start = cend_ref[base + jnp.maximum((tq // CHUNK) * i - 1, 0)]
                needed = jnp.logical_or(needed, key_end - query_start <= thr_ref[head])
            return jnp.where(jnp.logical_and(needed, first == 2 * i), j, first)
        return lax.fori_loop(0, 2 * i, scan, 2 * i)

    first_pair = first_needed_block() // 2
    q = q_ref[...]
    lane = lax.broadcasted_iota(jnp.int32, (tq, LANES), 1)
    first_head = lane < HEAD_DIM
    zero = jnp.zeros((tq, LANES), bf16)
    q_aug = []
    for a in range(2):
        qa = jnp.where(first_head, q, zero) if a == 0 else jnp.where(first_head, zero, q)
        pick = jnp.where((lane >= C_TERMS * a) & (lane < C_TERMS * (a + 1)), -1.0, 0.0).astype(bf16)
        q_aug.append(jnp.concatenate([qa, pick], axis=1))
    m_ref[...] = jnp.full_like(m_ref, MASK_VALUE)
    acc_ref[...] = jnp.zeros_like(acc_ref)
    ones_rows = jnp.ones((SUM_ROWS, tk), bf16)

    def scores(j, slot, lo=0):
        off = pl.multiple_of(j * tk, tk)
        k = k_ref[pl.ds(off, tk), :]
        for a in range(2):
            s_ref[slot, a, :, lo:] = _dot_nt(k, q_aug[a][lo:])

    def consume(j, slot, masked, lo=0):
        off = pl.multiple_of(j * tk, tk)
        for a in range(2):
            s = s_ref[slot, a, :, lo:]
            if masked:
                kpos = off + lax.broadcasted_iota(jnp.int32, (tk, tq - lo), 0)
                qpos = i * tq + lo + lax.broadcasted_iota(jnp.int32, (tk, tq - lo), 1)
                s = jnp.where(kpos <= qpos, s, MASK_VALUE)
            m_old = m_ref[a, :, lo:]
            m_new = jnp.maximum(m_old, jnp.max(s, axis=0, keepdims=True))
            p = jnp.exp2(s - m_new)
            alpha = jnp.exp2(m_old - m_new)
            vt = jnp.concatenate([vt_ref[a * HEAD_DIM:(a + 1) * HEAD_DIM, pl.ds(off, tk)], ones_rows], axis=0)
            acc_ref[a, :, lo:] = alpha * acc_ref[a, :, lo:] + _dot(vt, p.astype(bf16))
            m_ref[a, :, lo:] = m_new

    scores(2 * first_pair, 0)

    def body(jj, carry):
        scores(2 * jj + 1, 1)
        consume(2 * jj, 0, False)
        scores(2 * jj + 2, 0)
        consume(2 * jj + 1, 1, False)
        return carry

    lax.fori_loop(first_pair, i, body, 0)
    scores(2 * i + 1, 1, lo=tk)
    consume(2 * i, 0, True)
    consume(2 * i + 1, 1, True, lo=tk)
    o_t = jnp.concatenate([acc_ref[a, :HEAD_DIM] / acc_ref[a, HEAD_DIM:HEAD_DIM + 1] for a in range(2)], axis=0)
    o_ref[...] = o_t.T.astype(bf16)


EXP2_UNDERFLOW = 160.0
NORM_SLACK = 1.02


def _fox(fq, fk_aug, fv_t, cend, norm2, *, batch, seq, tq):
    t = fq.shape[0]
    nq = seq // tq
    tk = tq // 2
    n_chunks = seq // CHUNK
    cend_flat = cend.reshape(batch, n_chunks, 8, LANES)[:, :, 0, :N_HEADS_FOX].transpose(0, 2, 1).reshape(-1)
    norm2 = norm2.reshape(batch, 8, LANES)[:, 0, :2 * N_HEADS_FOX].reshape(batch, 2, N_HEADS_FOX)
    thr = (2.0 * NORM_SLACK * jnp.sqrt(norm2[:, 0] * norm2[:, 1]) + EXP2_UNDERFLOW).reshape(-1)
    smem = pl.BlockSpec(memory_space=pltpu.SMEM)
    q_spec = pl.BlockSpec((tq, LANES), lambda b, p, i: (b * nq + i, p))
    return pl.pallas_call(
        functools.partial(_fox_kernel, tq=tq, tk=tk, n_chunks=n_chunks),
        grid=(batch, N_PAIRS, nq),
        in_specs=[smem, smem, q_spec,
                  pl.BlockSpec((seq, 2 * LANES), lambda b, p, i: (b, p)),
                  pl.BlockSpec((LANES, seq), lambda b, p, i: (p, b))],
        out_specs=q_spec,
        out_shape=jax.ShapeDtypeStruct((t, W_FOX), bf16),
        scratch_shapes=[pltpu.VMEM((2, 2, tk, tq), f32), pltpu.VMEM((2, 1, tq), f32),
                        pltpu.VMEM((2, HEAD_DIM + SUM_ROWS, tq), f32)],
        compiler_params=_params("parallel", "parallel", "arbitrary"),
        name="fox_attention",
    )(cend_flat, thr, fq, fk_aug, fv_t)


def _channel_kernel(a_ref, b_ref, c_ref, x_ref, p_ref, wo_ref, gmix_ref,
                    gpre_ref, wg_ref, wu_ref, cw_ref, cb_ref, wd_ref, gpost_ref,
                    gple_ref, wpg_ref, wpp_ref, gplepost_ref, o_ref, halo_ref,
                    *, tm, fc, blocks_per_seq):
    i = pl.program_id(0)

    @pl.when(i % blocks_per_seq == 0)
    def _():
        halo_ref[...] = jnp.zeros_like(halo_ref)

    cat = jnp.concatenate([a_ref[...], b_ref[...], c_ref[...]], axis=1)
    xf = x_ref[...] + _rms(_dot(cat, wo_ref[...]), gmix_ref[...])

    h = _rms(xf, gpre_ref[...]).astype(bf16)
    row8 = lax.broadcasted_iota(jnp.int32, (8, fc), 0)
    acc = jnp.zeros((tm, D_MODEL), f32)
    for c in range(D_FF // fc):
        cols = slice(c * fc, (c + 1) * fc)
        g = _dot(h, wg_ref[:, cols])
        up = _dot(h, wu_ref[:, cols])
        prev = halo_ref[:, cols]
        halo_ref[:, cols] = g[tm - 8:, :]
        g1 = pltpu.roll(g, 1, 0)
        g2 = pltpu.roll(g, 2, 0)
        g1 = jnp.concatenate([jnp.where(row8 < 1, pltpu.roll(prev, 1, 0), g1[:8]), g1[8:]], axis=0)
        g2 = jnp.concatenate([jnp.where(row8 < 2, pltpu.roll(prev, 2, 0), g2[:8]), g2[8:]], axis=0)
        conv = cb_ref[:, cols] + g2 * cw_ref[0:1, cols]
        conv = conv + g1 * cw_ref[1:2, cols]
        conv = conv + g * cw_ref[2:3, cols]
        act = _gelu(conv) * up
        acc = acc + _dot(act.astype(bf16), wd_ref[cols, :])
    xf = xf + _rms(acc, gpost_ref[...])

    gate = jax.nn.sigmoid(_dot(_rms(xf, gple_ref[...]).astype(bf16), wpg_ref[...]))
    e = _dot(p_ref[...].astype(bf16), wpp_ref[...])
    o_ref[...] = xf + _rms(e * gate, gplepost_ref[...])


def _channel(out_a, out_b, out_c, x2d, p2d, w_o, g_mix, g_pre, w_gate, w_up, conv_w, conv_b, w_down, g_post,
             g_ple, w_ple_gate, w_ple_proj, g_ple_post, *, layer, seq, tm, fc):
    t = x2d.shape[0]
    row = lambda width: pl.BlockSpec((tm, width), lambda i: (i, 0))
    gain = _resident((1, D_MODEL))
    kern = functools.partial(_channel_kernel, tm=tm, fc=fc, blocks_per_seq=seq // tm)
    return pl.pallas_call(
        kern,
        grid=(t // tm,),
        in_specs=[row(W_SGU), row(W_RET), row(W_FOX), row(D_MODEL),
                  pl.BlockSpec((tm, D_PLE), lambda i: (layer * (t // tm) + i, 0)),
                  _resident((D_MODEL, D_MODEL), layer), gain,
                  gain, _resident((D_MODEL, D_FF), layer), _resident((D_MODEL, D_FF), layer),
                  _resident(conv_w.shape), _resident((1, D_FF)), _resident((D_FF, D_MODEL), layer), gain,
                  gain, _resident((D_MODEL, D_MODEL), layer), _resident((D_PLE, D_MODEL), layer), gain],
        out_specs=row(D_MODEL),
        out_shape=jax.ShapeDtypeStruct((t, D_MODEL), f32),
        scratch_shapes=[pltpu.VMEM((8, D_FF), f32)],
        compiler_params=_params("arbitrary"),
        name="channel",
    )(out_a, out_b, out_c, x2d, p2d, w_o, g_mix, g_pre, w_gate, w_up, conv_w, conv_b, w_down, g_post,
      g_ple, w_ple_gate, w_ple_proj, g_ple_post)


def _largest_tile(seq, cap):
    tile = cap
    while seq % tile:
        tile //= 2
    return tile


def kernel(x, p, positions, mix_pre_g, w_in, sgu_v_g, sgu_w, sgu_b, fox_b_f, w_o, mix_post_g, ffn_pre_g,
           w_gate, w_up, conv_w, conv_b, w_down, ffn_post_g, ple_pre_g, w_ple_gate, w_ple_proj, ple_post_g):
    batch, seq, _ = x.shape
    depth = w_in.shape[0]
    t = batch * seq
    assert seq % CHUNK == 0
    tm = _largest_tile(seq, 512)
    tm_in = _largest_tile(seq, 1024)
    tq = _largest_tile(seq, 1024)
    fc = 2048

    cos4, sin4 = _rope_tables(positions)
    x2d = x.reshape(t, D_MODEL)
    p2d = p.reshape(depth * t, D_PLE)
    w_pad = jnp.pad(w_in.astype(bf16), ((0, 0), (0, 0), (0, N_IN_PAD - w_in.shape[2])))
    w_o, w_gate, w_up, w_down, w_ple_gate, w_ple_proj = (
        w.astype(bf16) for w in (w_o, w_gate, w_up, w_down, w_ple_gate, w_ple_proj))
    for i in range(depth):
        sgu_b_full = jnp.repeat(sgu_b[i].T, HEAD_DIM, axis=1)
        fox_b = jnp.pad(fox_b_f[i], (0, LANES - N_HEADS_FOX))[None, :]
        out_a, rq, rk, rv, rg, fq, fk_aug, fv_t, cend, norm2 = _proj_in(
            x2d, mix_pre_g[i][None, :], w_pad, cos4, sin4, sgu_w[i], sgu_b_full,
            sgu_v_g[i].reshape(1, W_SGU), fox_b, layer=i, seq=seq, tm=tm_in)
        out_b = _retention(rq, rk, rv, rg, seq=seq, tm=tm)
        out_c = _fox(fq, fk_aug, fv_t, cend, norm2, batch=batch, seq=seq, tq=tq)
        x2d = _channel(out_a, out_b, out_c, x2d, p2d, w_o, mix_post_g[i][None, :],
                       ffn_pre_g[i][None, :], w_gate, w_up, conv_w[i], conv_b[i][None, :], w_down,
                       ffn_post_g[i][None, :], ple_pre_g[i][None, :], w_ple_gate, w_ple_proj,
                       ple_post_g[i][None, :], layer=i, seq=seq, tm=tm, fc=fc)
    return x2d.reshape(batch, seq, D_MODEL)
```

```python
import functools

import numpy as np
import jax
import jax.numpy as jnp
from jax import lax
from jax.experimental import pallas as pl
from jax.experimental.pallas import tpu as pltpu

D_MODEL = 1024
D_PLE = 256
HEAD_DIM = 64
N_HEADS_SGU = 4
N_HEADS_RET = 6
N_HEADS_FOX = 6
W_SGU = N_HEADS_SGU * HEAD_DIM
W_RET = N_HEADS_RET * HEAD_DIM
W_FOX = N_HEADS_FOX * HEAD_DIM
CHUNK = 128
D_FF = 4 * D_MODEL
ROPE_BASE = 10000.0
RMS_EPS = 1e-6
LANES = 128
N_PAIRS = W_RET // LANES
N_IN_PAD = 2 * W_SGU + 4 * W_RET + 3 * W_FOX + LANES
MASK_VALUE = -1e30
LOG2E = 1.4426950408889634
C_TERMS = 3
CUMSUM_ROWS = 512
VMEM_LIMIT = 56 * 1024 * 1024

f32 = jnp.float32
bf16 = jnp.bfloat16


def _params(*sem):
    return pltpu.CompilerParams(dimension_semantics=sem, vmem_limit_bytes=VMEM_LIMIT)


def _resident(shape, layer=None):
    zeros = (0,) * len(shape)
    if layer is None:
        return pl.BlockSpec(shape, lambda *_: zeros, pipeline_mode=pl.Buffered(1))
    return pl.BlockSpec((None,) + tuple(shape), lambda *_: (layer,) + zeros, pipeline_mode=pl.Buffered(1))


def _rms(xf, g):
    y = xf * lax.rsqrt(jnp.mean(xf * xf, axis=-1, keepdims=True) + RMS_EPS)
    return y * g


def _dot(a, b):
    return jnp.dot(a, b, preferred_element_type=f32)


def _dot_nt(a, b):
    return lax.dot_general(a, b, (((1,), (1,)), ((), ())), preferred_element_type=f32)


def _dot_tn(a, b):
    return lax.dot_general(a, b, (((0,), (0,)), ((), ())), preferred_element_type=f32)


def _split_bf16(x, terms):
    parts = []
    r = x
    for _ in range(terms):
        p = r.astype(bf16)
        parts.append(p)
        r = r - p.astype(f32)
    return parts


def _group_mean(y, width):
    r = lax.broadcasted_iota(jnp.int32, (width, width), 0) // HEAD_DIM
    c = lax.broadcasted_iota(jnp.int32, (width, width), 1) // HEAD_DIM
    gm = jnp.where(r == c, 1.0 / HEAD_DIM, 0.0).astype(bf16)
    return _dot(y.astype(bf16), gm)


def _gelu(x):
    return jax.nn.gelu(x, approximate=True)


ROPE_HALF = HEAD_DIM // 2
ROPE_GROUPS = LANES // ROPE_HALF


def _rope_kernel(pos_ref, invf_ref, cos_ref, sin_ref):
    ang = pos_ref[...].astype(f32) * invf_ref[...]
    c = jnp.cos(ang)
    s = jnp.sin(ang)
    lane = lax.broadcasted_iota(jnp.int32, ang.shape, 1)
    first_group = lane < ROPE_HALF
    sign = jnp.where((lane % HEAD_DIM) < ROPE_HALF, -1.0, 1.0)

    def spread(v, j):
        v = pltpu.roll(v, LANES - ROPE_HALF * j, 1) if j else v
        v = jnp.where(first_group, v, 0.0)
        v = v + pltpu.roll(v, ROPE_HALF, 1)
        return v + pltpu.roll(v, 2 * ROPE_HALF, 1)

    for j in range(ROPE_GROUPS):
        cos_ref[j] = spread(c, j)
        sin_ref[j] = spread(s, j) * sign


def _rope_tables(positions):
    t = positions.size
    rows = t // ROPE_GROUPS
    pos = jnp.repeat(positions.reshape(ROPE_GROUPS, rows).T, ROPE_HALF, axis=1)
    inv_freq = ROPE_BASE ** (-jnp.arange(ROPE_HALF, dtype=f32) / ROPE_HALF)
    invf = jnp.tile(inv_freq, ROPE_GROUPS)[None, :]
    tr = min(rows, 1024)
    out_spec = pl.BlockSpec((ROPE_GROUPS, tr, LANES), lambda i: (0, i, 0))
    cos, sin = pl.pallas_call(
        _rope_kernel,
        grid=(rows // tr,),
        in_specs=[pl.BlockSpec((tr, LANES), lambda i: (i, 0)), _resident((1, LANES))],
        out_specs=[out_spec, out_spec],
        out_shape=[jax.ShapeDtypeStruct((ROPE_GROUPS, rows, LANES), f32)] * 2,
        compiler_params=_params("parallel"),
        name="rope_tables",
    )(pos, invf)
    return cos.reshape(t, LANES), sin.reshape(t, LANES)


def _proj_in_kernel(x_ref, g_ref, w_ref, cos_ref, sin_ref, sguw_ref, sgub_ref, vg_ref, fb_ref, place_ref,
                    oa_ref, rq_ref, rk_ref, rv_ref, rg_ref, fq_ref, fk_ref, fvt_ref, cend_ref, nrm_ref,
                    carry_ref, fv_ref, *, tm, blocks_per_seq):
    i = pl.program_id(0)
    n_chunks = tm // CHUNK
    h = _rms(x_ref[...], g_ref[...]).astype(bf16)

    def proj(lo, width):
        return _dot(h, w_ref[:, lo:lo + width])

    z_fox_vf = proj(2 * W_SGU + 4 * W_RET + 2 * W_FOX, W_FOX + LANES)

    ff = z_fox_vf[:, W_FOX:] + fb_ref[...]
    logf = -(jnp.maximum(-ff, 0.0) + jnp.log1p(jnp.exp(-jnp.abs(ff))))

    @pl.when(i % blocks_per_seq == 0)
    def _():
        carry_ref[...] = jnp.zeros_like(carry_ref)

    sub = min(tm, CUMSUM_ROWS)
    sub_tril = (lax.broadcasted_iota(jnp.int32, (sub, sub), 0) >= lax.broadcasted_iota(jnp.int32, (sub, sub), 1))
    sub_tril = jnp.where(sub_tril, 1.0, 0.0).astype(bf16)
    carry = carry_ref[...]
    cums = []
    for r in range(tm // sub):
        within = _dot(sub_tril, jnp.concatenate(_split_bf16(logf[r * sub:(r + 1) * sub], 3), axis=1))
        cums.append(within[:, :LANES] + within[:, LANES:2 * LANES] + within[:, 2 * LANES:] + carry)
        carry = cums[-1][sub - 1:sub, :]
    carry_ref[...] = carry
    cum = jnp.concatenate(cums, axis=0) * LOG2E
    for c in range(n_chunks):
        cend_ref[8 * c:8 * (c + 1), :] = jnp.broadcast_to(cum[(c + 1) * CHUNK - 1:(c + 1) * CHUNK, :], (8, LANES))
    placed = _dot(jnp.concatenate(_split_bf16(cum, C_TERMS), axis=1), place_ref[...]).astype(bf16)
    for p in range(N_PAIRS):
        fk_ref[:, (2 * p + 1) * LANES:(2 * p + 2) * LANES] = placed[:, p * LANES:(p + 1) * LANES]

    sq_row = lax.broadcasted_iota(jnp.int32, (CHUNK, CHUNK), 0)
    sq_col = lax.broadcasted_iota(jnp.int32, (CHUNK, CHUNK), 1)
    tril = sq_row >= sq_col
    first_head = sq_col < HEAD_DIM

    z_sgu = _gelu(proj(0, 2 * W_SGU))
    u = z_sgu[:, :W_SGU]
    v = z_sgu[:, W_SGU:]
    v = v * lax.rsqrt(_group_mean(v * v, W_SGU) + RMS_EPS) * vg_ref[...]
    for p in range(W_SGU // LANES):
        cols = slice(p * LANES, (p + 1) * LANES)
        w_pair = jnp.concatenate(
            [jnp.where(tril, sguw_ref[2 * p], 0.0), jnp.where(tril, sguw_ref[2 * p + 1], 0.0)],
            axis=0).astype(bf16)
        for c in range(n_chunks):
            rows = slice(c * CHUNK, (c + 1) * CHUNK)
            r = _dot(w_pair, v[rows, cols].astype(bf16))
            s = jnp.where(first_head, r[:CHUNK], r[CHUNK:]) + sgub_ref[:, cols]
            oa_ref[rows, cols] = (u[rows, cols] * s).astype(bf16)

    cosv = cos_ref[...]
    sinv = sin_ref[...]
    lane = lax.broadcasted_iota(jnp.int32, (tm, LANES), 1)
    first_half = (lane % HEAD_DIM) < (HEAD_DIM // 2)

    def rotary(z):
        outs = []
        for s in range(N_PAIRS):
            t = z[:, s * LANES:(s + 1) * LANES]
            partner = jnp.where(first_half, pltpu.roll(t, LANES - HEAD_DIM // 2, 1),
                                pltpu.roll(t, HEAD_DIM // 2, 1))
            outs.append(t * cosv + partner * sinv)
        return jnp.concatenate(outs, axis=1)

    base = 2 * W_SGU
    z_qk = proj(base, 2 * W_RET)
    rq_ref[...] = rotary(z_qk[:, :W_RET]).astype(bf16)
    rk_ref[...] = (rotary(z_qk[:, W_RET:]) * (HEAD_DIM ** -0.5)).astype(bf16)
    z_vg = proj(base + 2 * W_RET, 2 * W_RET).astype(bf16)
    rv_ref[...] = z_vg[:, :W_RET]
    rg_ref[...] = z_vg[:, W_RET:]

    base += 4 * W_RET
    z_fox_qk = proj(base, 2 * W_FOX)
    fq = (z_fox_qk[:, :W_FOX] * (LOG2E * HEAD_DIM ** -0.5)).astype(bf16)
    fq_ref[...] = fq
    fk = z_fox_qk[:, W_FOX:].astype(bf16)
    for p in range(N_PAIRS):
        fk_ref[:, 2 * p * LANES:(2 * p + 1) * LANES] = fk[:, p * LANES:(p + 1) * LANES]
    fv_ref[...] = z_fox_vf[:, :W_FOX]
    fvt_ref[...] = fv_ref[...].T.astype(bf16)

    gr = lax.broadcasted_iota(jnp.int32, (2 * W_FOX, LANES), 0) // HEAD_DIM
    gc = lax.broadcasted_iota(jnp.int32, (2 * W_FOX, LANES), 1)
    head_sum = jnp.where(gr == gc, 1.0, 0.0).astype(bf16)
    qk = jnp.concatenate([fq, fk], axis=1).astype(f32)
    norm2 = jnp.max(_dot((qk * qk).astype(bf16), head_sum), axis=0, keepdims=True)

    @pl.when(i % blocks_per_seq == 0)
    def _():
        nrm_ref[...] = jnp.zeros_like(nrm_ref)

    nrm_ref[...] = jnp.maximum(nrm_ref[...], norm2)


def _placement_matrix():
    place = np.zeros((C_TERMS * LANES, N_PAIRS * LANES), np.float32)
    for h in range(N_HEADS_FOX):
        for term in range(C_TERMS):
            place[term * LANES + h, (h // 2) * LANES + C_TERMS * (h % 2) + term] = 1.0
    return jnp.asarray(place, bf16)


def _proj_in(x2d, g, w_pad, cos4, sin4, sgu_w, sgu_b_full, sgu_vg, fox_b, *, layer, seq, tm):
    t = x2d.shape[0]
    row = lambda width: pl.BlockSpec((tm, width), lambda i: (i, 0))
    out_widths = [W_SGU] + [W_RET] * 4 + [W_FOX, 2 * W_FOX]
    place = _placement_matrix()
    kern = functools.partial(_proj_in_kernel, tm=tm, blocks_per_seq=seq // tm)
    return pl.pallas_call(
        kern,
        grid=(t // tm,),
        in_specs=[row(D_MODEL), _resident((1, D_MODEL)), _resident((D_MODEL, N_IN_PAD), layer),
                  row(LANES), row(LANES),
                  _resident((N_HEADS_SGU, CHUNK, CHUNK)), _resident((CHUNK, W_SGU)),
                  _resident((1, W_SGU)), _resident((1, LANES)), _resident(place.shape)],
        out_specs=[row(w) for w in out_widths] + [
            pl.BlockSpec((W_FOX, tm), lambda i: (0, i)),
            pl.BlockSpec((8 * tm // CHUNK, LANES), lambda i: (i, 0)),
            pl.BlockSpec((8, LANES), lambda i: (i // (seq // tm), 0))],
        out_shape=[jax.ShapeDtypeStruct((t, w), bf16) for w in out_widths] + [
            jax.ShapeDtypeStruct((W_FOX, t), bf16),
            jax.ShapeDtypeStruct((8 * t // CHUNK, LANES), f32),
            jax.ShapeDtypeStruct((8 * t // seq, LANES), f32)],
        scratch_shapes=[pltpu.VMEM((1, LANES), f32), pltpu.VMEM((tm, W_FOX), f32)],
        compiler_params=_params("arbitrary"),
        name="proj_in",
    )(x2d, g, w_pad, cos4, sin4, sgu_w, sgu_b_full, sgu_vg, fox_b, place)


def _retention_kernel(q_ref, k_ref, v_ref, g_ref, dec_ref, qd_ref, kd_ref, cd_ref, o_ref, state_ref, raw_ref,
                      *, tm, blocks_per_seq):
    i = pl.program_id(0)

    @pl.when(i % blocks_per_seq == 0)
    def _():
        state_ref[...] = jnp.zeros_like(state_ref)

    row = lax.broadcasted_iota(jnp.int32, (CHUNK, LANES), 0)
    lane = lax.broadcasted_iota(jnp.int32, (CHUNK, LANES), 1)
    first_head = lane < HEAD_DIM
    same_head = (row < HEAD_DIM) == first_head
    zero = jnp.zeros((CHUNK, LANES), bf16)
    for c in range(tm // CHUNK):
        rows = slice(c * CHUNK, (c + 1) * CHUNK)
        for p in range(N_PAIRS):
            cols = slice(p * LANES, (p + 1) * LANES)
            q = q_ref[rows, cols]
            k = k_ref[rows, cols]
            v = v_ref[rows, cols]
            q_heads = jnp.concatenate([jnp.where(first_head, q, zero), jnp.where(first_head, zero, q)], axis=0)
            inner = _dot_nt(q_heads, k) * dec_ref[p]
            o_heads = _dot(inner.astype(bf16), v)
            state = state_ref[p]
            o = (jnp.where(first_head, o_heads[:CHUNK], o_heads[CHUNK:])
                 + _dot(q, state.astype(bf16)) * qd_ref[p])
            k_dec = (k.astype(f32) * kd_ref[p]).astype(bf16)
            state_ref[p] = state * cd_ref[p] + jnp.where(same_head, _dot_tn(k_dec, v), 0.0)
            raw_ref[rows, cols] = o
    o = raw_ref[...]
    ro = o * lax.rsqrt(_group_mean(o * o, W_RET) + RMS_EPS)
    gate = g_ref[...].astype(f32)
    o_ref[...] = (gate * jax.nn.sigmoid(gate) * ro).astype(bf16)


def _retention_tables():
    log_g = jnp.log(1.0 - 2.0 ** (-5.0 - jnp.arange(N_HEADS_RET, dtype=f32)))
    idx = jnp.arange(CHUNK, dtype=f32)
    diff = idx[:, None] - idx[None, :]
    decay = jnp.where(diff >= 0, jnp.exp(log_g[:, None, None] * jnp.maximum(diff, 0.0)), 0.0)
    lane_g = jnp.repeat(log_g, HEAD_DIM).reshape(N_PAIRS, 1, LANES)
    q_decay = jnp.exp(lane_g * (idx + 1.0)[None, :, None])
    k_decay = jnp.exp(lane_g * (CHUNK - 1.0 - idx)[None, :, None])
    chunk_decay = jnp.broadcast_to(jnp.exp(lane_g * CHUNK).reshape(N_PAIRS, LANES, 1), (N_PAIRS, LANES, LANES))
    pair_decay = decay.reshape(N_PAIRS, 2 * CHUNK, CHUNK)
    return pair_decay, q_decay, k_decay, chunk_decay


def _retention(rq, rk, rv, rg, *, seq, tm):
    t = rq.shape[0]
    decay, q_decay, k_decay, chunk_decay = _retention_tables()
    row = pl.BlockSpec((tm, W_RET), lambda i: (i, 0))
    kern = functools.partial(_retention_kernel, tm=tm, blocks_per_seq=seq // tm)
    return pl.pallas_call(
        kern,
        grid=(t // tm,),
        in_specs=[row, row, row, row, _resident(decay.shape), _resident(q_decay.shape),
                  _resident(k_decay.shape), _resident(chunk_decay.shape)],
        out_specs=row,
        out_shape=jax.ShapeDtypeStruct((t, W_RET), bf16),
        scratch_shapes=[pltpu.VMEM((N_PAIRS, LANES, LANES), f32), pltpu.VMEM((tm, W_RET), f32)],
        compiler_params=_params("arbitrary"),
        name="retention",
    )(rq, rk, rv, rg, decay, q_decay, k_decay, chunk_decay)


SUM_ROWS = 16


def _fox_kernel(cend_ref, thr_ref, q_ref, k_ref, vt_ref, o_ref, s_ref, m_ref, acc_ref, *, tq, tk, n_chunks):
    b = pl.program_id(0)
    pair = pl.program_id(1)
    i = pl.program_id(2)

    def first_needed_block():
        def scan(j, first):
            needed = False
            for a in range(2):
                head = (b * N_HEADS_FOX + 2 * pair + a)
                base = head * n_chunks
                key_end = cend_ref[base + (tk // CHUNK) * (j + 1) - 1]
                query_start = cend_ref[base + jnp.maximum((tq // CHUNK) * i - 1, 0)]
                needed = jnp.logical_or(needed, key_end - query_start <= thr_ref[head])
            return jnp.where(jnp.logical_and(needed, first == 2 * i), j, first)
        return lax.fori_loop(0, 2 * i, scan, 2 * i)

    first_block = first_needed_block()
    first_pair = (first_block + 1) // 2
    q = q_ref[...]
    lane = lax.broadcasted_iota(jnp.int32, (tq, LANES), 1)
    first_head = lane < HEAD_DIM
    zero = jnp.zeros((tq, LANES), bf16)
    q_aug = []
    for a in range(2):
        qa = jnp.where(first_head, q, zero) if a == 0 else jnp.where(first_head, zero, q)
        pick = jnp.where((lane >= C_TERMS * a) & (lane < C_TERMS * (a + 1)), -1.0, 0.0).astype(bf16)
        q_aug.append(jnp.concatenate([qa, pick], axis=1))
    m_ref[...] = jnp.full_like(m_ref, MASK_VALUE)
    acc_ref[...] = jnp.zeros_like(acc_ref)
    ones_rows = jnp.ones((SUM_ROWS, tk), bf16)

    def scores(j, slot, lo=0):
        off = pl.multiple_of(j * tk, tk)
        k = k_ref[pl.ds(off, tk), :]
        for a in range(2):
            s_ref[slot, a, :, lo:] = _dot_nt(k, q_aug[a][lo:])

    def consume(j, slot, diag=None, lo=0):
        off = pl.multiple_of(j * tk, tk)
        for a in range(2):
            s = s_ref[slot, a, :, lo:]
            if diag is not None:
                kpos = diag * tk + lax.broadcasted_iota(jnp.int32, (tk, tq - lo), 0)
                qpos = lo + lax.broadcasted_iota(jnp.int32, (tk, tq - lo), 1)
                s = jnp.where(kpos <= qpos, s, MASK_VALUE)
            m_old = m_ref[a, :, lo:]
            m_new = jnp.maximum(m_old, jnp.max(s, axis=0, keepdims=True))
            p = jnp.exp2(s - m_new)
            alpha = jnp.exp2(m_old - m_new)
            vt = jnp.concatenate([vt_ref[a * HEAD_DIM:(a + 1) * HEAD_DIM, pl.ds(off, tk)], ones_rows], axis=0)
            acc_ref[a, :, lo:] = alpha * acc_ref[a, :, lo:] + _dot(vt, p.astype(bf16))
            m_ref[a, :, lo:] = m_new

    @pl.when(first_block % 2 == 1)
    def _():
        scores(first_block, 1)
        consume(first_block, 1)

    scores(2 * first_pair, 0)

    def body(jj, carry):
        scores(2 * jj + 1, 1)
        consume(2 * jj, 0)
        scores(2 * jj + 2, 0)
        consume(2 * jj + 1, 1)
        return carry

    lax.fori_loop(first_pair, i, body, 0)
    scores(2 * i + 1, 1, lo=tk)
    consume(2 * i, 0, diag=0)
    consume(2 * i + 1, 1, diag=1, lo=tk)
    o_t = jnp.concatenate([acc_ref[a, :HEAD_DIM] / acc_ref[a, HEAD_DIM:HEAD_DIM + 1] for a in range(2)], axis=0)
    o_ref[...] = o_t.T.astype(bf16)


EXP2_UNDERFLOW = 160.0
NORM_SLACK = 1.02


def _fox(fq, fk_aug, fv_t, cend, norm2, *, batch, seq, tq):
    t = fq.shape[0]
    nq = seq // tq
    tk = tq // 2
    n_chunks = seq // CHUNK
    cend_flat = cend.reshape(batch, n_chunks, 8, LANES)[:, :, 0, :N_HEADS_FOX].transpose(0, 2, 1).reshape(-1)
    norm2 = norm2.reshape(batch, 8, LANES)[:, 0, :2 * N_HEADS_FOX].reshape(batch, 2, N_HEADS_FOX)
    thr = (2.0 * NORM_SLACK * jnp.sqrt(norm2[:, 0] * norm2[:, 1]) + EXP2_UNDERFLOW).reshape(-1)
    smem = pl.BlockSpec(memory_space=pltpu.SMEM)
    q_spec = pl.BlockSpec((tq, LANES), lambda b, p, i: (b * nq + i, p))
    return pl.pallas_call(
        functools.partial(_fox_kernel, tq=tq, tk=tk, n_chunks=n_chunks),
        grid=(batch, N_PAIRS, nq),
        in_specs=[smem, smem, q_spec,
                  pl.BlockSpec((seq, 2 * LANES), lambda b, p, i: (b, p)),
                  pl.BlockSpec((LANES, seq), lambda b, p, i: (p, b))],
        out_specs=q_spec,
        out_shape=jax.ShapeDtypeStruct((t, W_FOX), bf16),
        scratch_shapes=[pltpu.VMEM((2, 2, tk, tq), f32), pltpu.VMEM((2, 1, tq), f32),
                        pltpu.VMEM((2, HEAD_DIM + SUM_ROWS, tq), f32)],
        compiler_params=_params("parallel", "parallel", "arbitrary"),
        name="fox_attention",
    )(cend_flat, thr, fq, fk_aug, fv_t)


def _channel_kernel(a_ref, b_ref, c_ref, x_ref, p_ref, wo_ref, gmix_ref,
                    gpre_ref, wg_ref, wu_ref, cw_ref, cb_ref, wd_ref, gpost_ref,
                    gple_ref, wpg_ref, wpp_ref, gplepost_ref, o_ref, halo_ref,
                    *, tm, fc, blocks_per_seq):
    i = pl.program_id(0)

    @pl.when(i % blocks_per_seq == 0)
    def _():
        halo_ref[...] = jnp.zeros_like(halo_ref)

    cat = jnp.concatenate([a_ref[...], b_ref[...], c_ref[...]], axis=1)
    xf = x_ref[...] + _rms(_dot(cat, wo_ref[...]), gmix_ref[...])

    h = _rms(xf, gpre_ref[...]).astype(bf16)
    row8 = lax.broadcasted_iota(jnp.int32, (8, fc), 0)
    acc = jnp.zeros((tm, D_MODEL), f32)
    for c in range(D_FF // fc):
        cols = slice(c * fc, (c + 1) * fc)
        g = _dot(h, wg_ref[:, cols])
        up = _dot(h, wu_ref[:, cols])
        prev = halo_ref[:, cols]
        halo_ref[:, cols] = g[tm - 8:, :]
        g1 = pltpu.roll(g, 1, 0)
        g2 = pltpu.roll(g, 2, 0)
        g1 = jnp.concatenate([jnp.where(row8 < 1, pltpu.roll(prev, 1, 0), g1[:8]), g1[8:]], axis=0)
        g2 = jnp.concatenate([jnp.where(row8 < 2, pltpu.roll(prev, 2, 0), g2[:8]), g2[8:]], axis=0)
        conv = cb_ref[:, cols] + g2 * cw_ref[0:1, cols]
        conv = conv + g1 * cw_ref[1:2, cols]
        conv = conv + g * cw_ref[2:3, cols]
        act = _gelu(conv) * up
        acc = acc + _dot(act.astype(bf16), wd_ref[cols, :])
    xf = xf + _rms(acc, gpost_ref[...])

    gate = jax.nn.sigmoid(_dot(_rms(xf, gple_ref[...]).astype(bf16), wpg_ref[...]))
    e = _dot(p_ref[...].astype(bf16), wpp_ref[...])
    o_ref[...] = xf + _rms(e * gate, gplepost_ref[...])


def _channel(out_a, out_b, out_c, x2d, p2d, w_o, g_mix, g_pre, w_gate, w_up, conv_w, conv_b, w_down, g_post,
             g_ple, w_ple_gate, w_ple_proj, g_ple_post, *, layer, seq, tm, fc):
    t = x2d.shape[0]
    row = lambda width: pl.BlockSpec((tm, width), lambda i: (i, 0))
    gain = _resident((1, D_MODEL))
    kern = functools.partial(_channel_kernel, tm=tm, fc=fc, blocks_per_seq=seq // tm)
    return pl.pallas_call(
        kern,
        grid=(t // tm,),
        in_specs=[row(W_SGU), row(W_RET), row(W_FOX), row(D_MODEL),
                  pl.BlockSpec((tm, D_PLE), lambda i: (layer * (t // tm) + i, 0)),
                  _resident((D_MODEL, D_MODEL), layer), gain,
                  gain, _resident((D_MODEL, D_FF), layer), _resident((D_MODEL, D_FF), layer),
                  _resident(conv_w.shape), _resident((1, D_FF)), _resident((D_FF, D_MODEL), layer), gain,
                  gain, _resident((D_MODEL, D_MODEL), layer), _resident((D_PLE, D_MODEL), layer), gain],
        out_specs=row(D_MODEL),
        out_shape=jax.ShapeDtypeStruct((t, D_MODEL), f32),
        scratch_shapes=[pltpu.VMEM((8, D_FF), f32)],
        compiler_params=_params("arbitrary"),
        name="channel",
    )(out_a, out_b, out_c, x2d, p2d, w_o, g_mix, g_pre, w_gate, w_up, conv_w, conv_b, w_down, g_post,
      g_ple, w_ple_gate, w_ple_proj, g_ple_post)


def _largest_tile(seq, cap):
    tile = cap
    while seq % tile:
        tile //= 2
    return tile


def kernel(x, p, positions, mix_pre_g, w_in, sgu_v_g, sgu_w, sgu_b, fox_b_f, w_o, mix_post_g, ffn_pre_g,
           w_gate, w_up, conv_w, conv_b, w_down, ffn_post_g, ple_pre_g, w_ple_gate, w_ple_proj, ple_post_g):
    batch, seq, _ = x.shape
    depth = w_in.shape[0]
    t = batch * seq
    assert seq % CHUNK == 0
    tm = _largest_tile(seq, 512)
    tm_in = _largest_tile(seq, 1024)
    tq = _largest_tile(seq, 1024)
    fc = 2048

    cos4, sin4 = _rope_tables(positions)
    x2d = x.reshape(t, D_MODEL)
    p2d = p.reshape(depth * t, D_PLE)
    w_pad = jnp.pad(w_in.astype(bf16), ((0, 0), (0, 0), (0, N_IN_PAD - w_in.shape[2])))
    w_o, w_gate, w_up, w_down, w_ple_gate, w_ple_proj = (
        w.astype(bf16) for w in (w_o, w_gate, w_up, w_down, w_ple_gate, w_ple_proj))
    for i in range(depth):
        sgu_b_full = jnp.repeat(sgu_b[i].T, HEAD_DIM, axis=1)
        fox_b = jnp.pad(fox_b_f[i], (0, LANES - N_HEADS_FOX))[None, :]
        out_a, rq, rk, rv, rg, fq, fk_aug, fv_t, cend, norm2 = _proj_in(
            x2d, mix_pre_g[i][None, :], w_pad, cos4, sin4, sgu_w[i], sgu_b_full,
            sgu_v_g[i].reshape(1, W_SGU), fox_b, layer=i, seq=seq, tm=tm_in)
        out_b = _retention(rq, rk, rv, rg, seq=seq, tm=tm_in)
        out_c = _fox(fq, fk_aug, fv_t, cend, norm2, batch=batch, seq=seq, tq=tq)
        x2d = _channel(out_a, out_b, out_c, x2d, p2d, w_o, mix_post_g[i][None, :],
                       ffn_pre_g[i][None, :], w_gate, w_up, conv_w[i], conv_b[i][None, :], w_down,
                       ffn_post_g[i][None, :], ple_pre_g[i][None, :], w_ple_gate, w_ple_proj,
                       ple_post_g[i][None, :], layer=i, seq=seq, tm=tm, fc=fc)
    return x2d.reshape(batch, seq, D_MODEL)
```

```python
import functools

import numpy as np
import jax
import jax.numpy as jnp
from jax import lax
from jax.experimental import pallas as pl
from jax.experimental.pallas import tpu as pltpu

D_MODEL = 1024
D_PLE = 256
HEAD_DIM = 64
N_HEADS_SGU = 4
N_HEADS_RET = 6
N_HEADS_FOX = 6
W_SGU = N_HEADS_SGU * HEAD_DIM
W_RET = N_HEADS_RET * HEAD_DIM
W_FOX = N_HEADS_FOX * HEAD_DIM
CHUNK = 128
D_FF = 4 * D_MODEL
ROPE_BASE = 10000.0
RMS_EPS = 1e-6
LANES = 128
N_PAIRS = W_RET // LANES
N_IN_PAD = 2 * W_SGU + 4 * W_RET + 3 * W_FOX + LANES
MASK_VALUE = -1e30
LOG2E = 1.4426950408889634
C_TERMS = 3
CUMSUM_ROWS = 512
VMEM_LIMIT = 56 * 1024 * 1024

f32 = jnp.float32
bf16 = jnp.bfloat16


def _params(*sem):
    return pltpu.CompilerParams(dimension_semantics=sem, vmem_limit_bytes=VMEM_LIMIT)


def _resident(shape, layer=None):
    zeros = (0,) * len(shape)
    if layer is None:
        return pl.BlockSpec(shape, lambda *_: zeros, pipeline_mode=pl.Buffered(1))
    return pl.BlockSpec((None,) + tuple(shape), lambda *_: (layer,) + zeros, pipeline_mode=pl.Buffered(1))


def _rms(xf, g):
    y = xf * lax.rsqrt(jnp.mean(xf * xf, axis=-1, keepdims=True) + RMS_EPS)
    return y * g


def _dot(a, b):
    return jnp.dot(a, b, preferred_element_type=f32)


def _dot_nt(a, b):
    return lax.dot_general(a, b, (((1,), (1,)), ((), ())), preferred_element_type=f32)


def _dot_tn(a, b):
    return lax.dot_general(a, b, (((0,), (0,)), ((), ())), preferred_element_type=f32)


def _split_bf16(x, terms):
    parts = []
    r = x
    for _ in range(terms):
        p = r.astype(bf16)
        parts.append(p)
        r = r - p.astype(f32)
    return parts


def _group_mean(y, width):
    r = lax.broadcasted_iota(jnp.int32, (width, width), 0) // HEAD_DIM
    c = lax.broadcasted_iota(jnp.int32, (width, width), 1) // HEAD_DIM
    gm = jnp.where(r == c, 1.0 / HEAD_DIM, 0.0).astype(bf16)
    return _dot(y.astype(bf16), gm)


def _gelu(x):
    return jax.nn.gelu(x, approximate=True)


ROPE_HALF = HEAD_DIM // 2
ROPE_GROUPS = LANES // ROPE_HALF


def _rope_kernel(pos_ref, invf_ref, cos_ref, sin_ref):
    ang = pos_ref[...].astype(f32) * invf_ref[...]
    c = jnp.cos(ang)
    s = jnp.sin(ang)
    lane = lax.broadcasted_iota(jnp.int32, ang.shape, 1)
    first_group = lane < ROPE_HALF
    sign = jnp.where((lane % HEAD_DIM) < ROPE_HALF, -1.0, 1.0)

    def spread(v, j):
        v = pltpu.roll(v, LANES - ROPE_HALF * j, 1) if j else v
        v = jnp.where(first_group, v, 0.0)
        v = v + pltpu.roll(v, ROPE_HALF, 1)
        return v + pltpu.roll(v, 2 * ROPE_HALF, 1)

    for j in range(ROPE_GROUPS):
        cos_ref[j] = spread(c, j)
        sin_ref[j] = spread(s, j) * sign


def _rope_tables(positions):
    t = positions.size
    rows = t // ROPE_GROUPS
    pos = jnp.repeat(positions.reshape(ROPE_GROUPS, rows).T, ROPE_HALF, axis=1)
    inv_freq = ROPE_BASE ** (-jnp.arange(ROPE_HALF, dtype=f32) / ROPE_HALF)
    invf = jnp.tile(inv_freq, ROPE_GROUPS)[None, :]
    tr = min(rows, 1024)
    out_spec = pl.BlockSpec((ROPE_GROUPS, tr, LANES), lambda i: (0, i, 0))
    cos, sin = pl.pallas_call(
        _rope_kernel,
        grid=(rows // tr,),
        in_specs=[pl.BlockSpec((tr, LANES), lambda i: (i, 0)), _resident((1, LANES))],
        out_specs=[out_spec, out_spec],
        out_shape=[jax.ShapeDtypeStruct((ROPE_GROUPS, rows, LANES), f32)] * 2,
        compiler_params=_params("parallel"),
        name="rope_tables",
    )(pos, invf)
    return cos.reshape(t, LANES), sin.reshape(t, LANES)


def _proj_in_kernel(x_ref, g_ref, w_ref, cos_ref, sin_ref, sguw_ref, sgub_ref, vg_ref, fb_ref, place_ref,
                    oa_ref, rq_ref, rk_ref, rv_ref, rg_ref, fq_ref, fk_ref, fvt_ref, cend_ref, nrm_ref,
                    carry_ref, fv_ref, *, tm, blocks_per_seq):
    i = pl.program_id(0)
    n_chunks = tm // CHUNK
    h = _rms(x_ref[...], g_ref[...]).astype(bf16)

    def proj(lo, width):
        return _dot(h, w_ref[:, lo:lo + width])

    z_fox_vf = proj(2 * W_SGU + 4 * W_RET + 2 * W_FOX, W_FOX + LANES)

    ff = z_fox_vf[:, W_FOX:] + fb_ref[...]
    logf = -(jnp.maximum(-ff, 0.0) + jnp.log1p(jnp.exp(-jnp.abs(ff))))

    @pl.when(i % blocks_per_seq == 0)
    def _():
        carry_ref[...] = jnp.zeros_like(carry_ref)

    sub = min(tm, CUMSUM_ROWS)
    sub_tril = (lax.broadcasted_iota(jnp.int32, (sub, sub), 0) >= lax.broadcasted_iota(jnp.int32, (sub, sub), 1))
    sub_tril = jnp.where(sub_tril, 1.0, 0.0).astype(bf16)
    carry = carry_ref[...]
    cums = []
    for r in range(tm // sub):
        within = _dot(sub_tril, jnp.concatenate(_split_bf16(logf[r * sub:(r + 1) * sub], 3), axis=1))
        cums.append(within[:, :LANES] + within[:, LANES:2 * LANES] + within[:, 2 * LANES:] + carry)
        carry = cums[-1][sub - 1:sub, :]
    carry_ref[...] = carry
    cum = jnp.concatenate(cums, axis=0) * LOG2E
    for c in range(n_chunks):
        cend_ref[8 * c:8 * (c + 1), :] = jnp.broadcast_to(cum[(c + 1) * CHUNK - 1:(c + 1) * CHUNK, :], (8, LANES))
    placed = _dot(jnp.concatenate(_split_bf16(cum, C_TERMS), axis=1), place_ref[...]).astype(bf16)
    for p in range(N_PAIRS):
        fk_ref[:, (2 * p + 1) * LANES:(2 * p + 2) * LANES] = placed[:, p * LANES:(p + 1) * LANES]

    sq_row = lax.broadcasted_iota(jnp.int32, (CHUNK, CHUNK), 0)
    sq_col = lax.broadcasted_iota(jnp.int32, (CHUNK, CHUNK), 1)
    tril = sq_row >= sq_col
    first_head = sq_col < HEAD_DIM

    z_sgu = _gelu(proj(0, 2 * W_SGU))
    u = z_sgu[:, :W_SGU]
    v = z_sgu[:, W_SGU:]
    v = v * lax.rsqrt(_group_mean(v * v, W_SGU) + RMS_EPS) * vg_ref[...]
    for p in range(W_SGU // LANES):
        cols = slice(p * LANES, (p + 1) * LANES)
        w_pair = jnp.concatenate(
            [jnp.where(tril, sguw_ref[2 * p], 0.0), jnp.where(tril, sguw_ref[2 * p + 1], 0.0)],
            axis=0).astype(bf16)
        for c in range(n_chunks):
            rows = slice(c * CHUNK, (c + 1) * CHUNK)
            r = _dot(w_pair, v[rows, cols].astype(bf16))
            s = jnp.where(first_head, r[:CHUNK], r[CHUNK:]) + sgub_ref[:, cols]
            oa_ref[rows, cols] = (u[rows, cols] * s).astype(bf16)

    cosv = cos_ref[...]
    sinv = sin_ref[...]
    lane = lax.broadcasted_iota(jnp.int32, (tm, LANES), 1)
    first_half = (lane % HEAD_DIM) < (HEAD_DIM // 2)

    def rotary(z):
        outs = []
        for s in range(N_PAIRS):
            t = z[:, s * LANES:(s + 1) * LANES]
            partner = jnp.where(first_half, pltpu.roll(t, LANES - HEAD_DIM // 2, 1),
                                pltpu.roll(t, HEAD_DIM // 2, 1))
            outs.append(t * cosv + partner * sinv)
        return jnp.concatenate(outs, axis=1)

    base = 2 * W_SGU
    z_qk = proj(base, 2 * W_RET)
    rq_ref[...] = rotary(z_qk[:, :W_RET]).astype(bf16)
    rk_ref[...] = (rotary(z_qk[:, W_RET:]) * (HEAD_DIM ** -0.5)).astype(bf16)
    z_vg = proj(base + 2 * W_RET, 2 * W_RET).astype(bf16)
    rv_ref[...] = z_vg[:, :W_RET]
    rg_ref[...] = z_vg[:, W_RET:]

    base += 4 * W_RET
    z_fox_qk = proj(base, 2 * W_FOX)
    fq = (z_fox_qk[:, :W_FOX] * (LOG2E * HEAD_DIM ** -0.5)).astype(bf16)
    fq_ref[...] = fq
    fk = z_fox_qk[:, W_FOX:].astype(bf16)
    for p in range(N_PAIRS):
        fk_ref[:, 2 * p * LANES:(2 * p + 1) * LANES] = fk[:, p * LANES:(p + 1) * LANES]
    fv_ref[...] = z_fox_vf[:, :W_FOX]
    fvt_ref[...] = fv_ref[...].T.astype(bf16)

    gr = lax.broadcasted_iota(jnp.int32, (2 * W_FOX, LANES), 0) // HEAD_DIM
    gc = lax.broadcasted_iota(jnp.int32, (2 * W_FOX, LANES), 1)
    head_sum = jnp.where(gr == gc, 1.0, 0.0).astype(bf16)
    qk = jnp.concatenate([fq, fk], axis=1).astype(f32)
    norm2 = jnp.max(_dot((qk * qk).astype(bf16), head_sum), axis=0, keepdims=True)

    @pl.when(i % blocks_per_seq == 0)
    def _():
        nrm_ref[...] = jnp.zeros_like(nrm_ref)

    nrm_ref[...] = jnp.maximum(nrm_ref[...], norm2)


def _placement_matrix():
    place = np.zeros((C_TERMS * LANES, N_PAIRS * LANES), np.float32)
    for h in range(N_HEADS_FOX):
        for term in range(C_TERMS):
            place[term * LANES + h, (h // 2) * LANES + C_TERMS * (h % 2) + term] = 1.0
    return jnp.asarray(place, bf16)


def _proj_in(x2d, g, w_pad, cos4, sin4, sgu_w, sgu_b_full, sgu_vg, fox_b, *, layer, seq, tm):
    t = x2d.shape[0]
    row = lambda width: pl.BlockSpec((tm, width), lambda i: (i, 0))
    out_widths = [W_SGU] + [W_RET] * 4 + [W_FOX, 2 * W_FOX]
    place = _placement_matrix()
    kern = functools.partial(_proj_in_kernel, tm=tm, blocks_per_seq=seq // tm)
    return pl.pallas_call(
        kern,
        grid=(t // tm,),
        in_specs=[row(D_MODEL), _resident((1, D_MODEL)), _resident((D_MODEL, N_IN_PAD), layer),
                  row(LANES), row(LANES),
                  _resident((N_HEADS_SGU, CHUNK, CHUNK)), _resident((CHUNK, W_SGU)),
                  _resident((1, W_SGU)), _resident((1, LANES)), _resident(place.shape)],
        out_specs=[row(w) for w in out_widths] + [
            pl.BlockSpec((W_FOX, tm), lambda i: (0, i)),
            pl.BlockSpec((8 * tm // CHUNK, LANES), lambda i: (i, 0)),
            pl.BlockSpec((8, LANES), lambda i: (i // (seq // tm), 0))],
        out_shape=[jax.ShapeDtypeStruct((t, w), bf16) for w in out_widths] + [
            jax.ShapeDtypeStruct((W_FOX, t), bf16),
            jax.ShapeDtypeStruct((8 * t // CHUNK, LANES), f32),
            jax.ShapeDtypeStruct((8 * t // seq, LANES), f32)],
        scratch_shapes=[pltpu.VMEM((1, LANES), f32), pltpu.VMEM((tm, W_FOX), f32)],
        compiler_params=_params("arbitrary"),
        name="proj_in",
    )(x2d, g, w_pad, cos4, sin4, sgu_w, sgu_b_full, sgu_vg, fox_b, place)


def _retention_kernel(q_ref, k_ref, v_ref, g_ref, dec_ref, qd_ref, kd_ref, cd_ref, o_ref, state_ref, raw_ref,
                      *, tm, blocks_per_seq):
    i = pl.program_id(0)

    @pl.when(i % blocks_per_seq == 0)
    def _():
        state_ref[...] = jnp.zeros_like(state_ref)

    row = lax.broadcasted_iota(jnp.int32, (CHUNK, LANES), 0)
    lane = lax.broadcasted_iota(jnp.int32, (CHUNK, LANES), 1)
    first_head = lane < HEAD_DIM
    same_head = (row < HEAD_DIM) == first_head
    zero = jnp.zeros((CHUNK, LANES), bf16)
    for c in range(tm // CHUNK):
        rows = slice(c * CHUNK, (c + 1) * CHUNK)
        for p in range(N_PAIRS):
            cols = slice(p * LANES, (p + 1) * LANES)
            q = q_ref[rows, cols]
            k = k_ref[rows, cols]
            v = v_ref[rows, cols]
            q_heads = jnp.concatenate([jnp.where(first_head, q, zero), jnp.where(first_head, zero, q)], axis=0)
            inner = _dot_nt(q_heads, k) * dec_ref[p]
            o_heads = _dot(inner.astype(bf16), v)
            state = state_ref[p]
            o = (jnp.where(first_head, o_heads[:CHUNK], o_heads[CHUNK:])
                 + _dot(q, state.astype(bf16)) * qd_ref[p])
            k_dec = (k.astype(f32) * kd_ref[p]).astype(bf16)
            state_ref[p] = state * cd_ref[p] + jnp.where(same_head, _dot_tn(k_dec, v), 0.0)
            raw_ref[rows, cols] = o
    o = raw_ref[...]
    ro = o * lax.rsqrt(_group_mean(o * o, W_RET) + RMS_EPS)
    gate = g_ref[...].astype(f32)
    o_ref[...] = (gate * jax.nn.sigmoid(gate) * ro).astype(bf16)


def _retention_tables():
    log_g = jnp.log(1.0 - 2.0 ** (-5.0 - jnp.arange(N_HEADS_RET, dtype=f32)))
    idx = jnp.arange(CHUNK, dtype=f32)
    diff = idx[:, None] - idx[None, :]
    decay = jnp.where(diff >= 0, jnp.exp(log_g[:, None, None] * jnp.maximum(diff, 0.0)), 0.0)
    lane_g = jnp.repeat(log_g, HEAD_DIM).reshape(N_PAIRS, 1, LANES)
    q_decay = jnp.exp(lane_g * (idx + 1.0)[None, :, None])
    k_decay = jnp.exp(lane_g * (CHUNK - 1.0 - idx)[None, :, None])
    chunk_decay = jnp.broadcast_to(jnp.exp(lane_g * CHUNK).reshape(N_PAIRS, LANES, 1), (N_PAIRS, LANES, LANES))
    pair_decay = decay.reshape(N_PAIRS, 2 * CHUNK, CHUNK)
    return pair_decay, q_decay, k_decay, chunk_decay


def _retention(rq, rk, rv, rg, *, seq, tm):
    t = rq.shape[0]
    decay, q_decay, k_decay, chunk_decay = _retention_tables()
    row = pl.BlockSpec((tm, W_RET), lambda i: (i, 0))
    kern = functools.partial(_retention_kernel, tm=tm, blocks_per_seq=seq // tm)
    return pl.pallas_call(
        kern,
        grid=(t // tm,),
        in_specs=[row, row, row, row, _resident(decay.shape), _resident(q_decay.shape),
                  _resident(k_decay.shape), _resident(chunk_decay.shape)],
        out_specs=row,
        out_shape=jax.ShapeDtypeStruct((t, W_RET), bf16),
        scratch_shapes=[pltpu.VMEM((N_PAIRS, LANES, LANES), f32), pltpu.VMEM((tm, W_RET), f32)],
        compiler_params=_params("arbitrary"),
        name="retention",
    )(rq, rk, rv, rg, decay, q_decay, k_decay, chunk_decay)


SUM_ROWS = 16


def _fox_kernel(cend_ref, thr_ref, q_ref, k_ref, vt_ref, o_ref, s_ref, m_ref, acc_ref, *, tq, tk, n_chunks):
    b = pl.program_id(0)
    pair = pl.program_id(1)
    i = pl.program_id(2)

    def first_needed_block():
        heads = [b * N_HEADS_FOX + 2 * pair + a for a in range(2)]
        q_start = jnp.maximum((tq // CHUNK) * i - 1, 0)
        limit = [cend_ref[h * n_chunks + q_start] + thr_ref[h] for h in heads]

        def needed(j):
            key_end = (tk // CHUNK) * (jnp.maximum(j, 0) + 1) - 1
            flags = [cend_ref[h * n_chunks + key_end] <= lim for h, lim in zip(heads, limit)]
            return jnp.logical_and(j >= 0, jnp.logical_or(flags[0], flags[1]))

        return lax.while_loop(needed, lambda j: j - 1, 2 * i - 1) + 1

    first_block = first_needed_block()
    first_pair = (first_block + 1) // 2
    q = q_ref[...]
    lane = lax.broadcasted_iota(jnp.int32, (tq, LANES), 1)
    first_head = lane < HEAD_DIM
    zero = jnp.zeros((tq, LANES), bf16)
    q_aug = []
    for a in range(2):
        qa = jnp.where(first_head, q, zero) if a == 0 else jnp.where(first_head, zero, q)
        pick = jnp.where((lane >= C_TERMS * a) & (lane < C_TERMS * (a + 1)), -1.0, 0.0).astype(bf16)
        q_aug.append(jnp.concatenate([qa, pick], axis=1))
    m_ref[...] = jnp.full_like(m_ref, MASK_VALUE)
    acc_ref[...] = jnp.zeros_like(acc_ref)
    ones_rows = jnp.ones((SUM_ROWS, tk), bf16)

    def scores(j, slot, lo=0):
        off = pl.multiple_of(j * tk, tk)
        k = k_ref[pl.ds(off, tk), :]
        for a in range(2):
            s_ref[slot, a, :, lo:] = _dot_nt(k, q_aug[a][lo:])

    def consume(j, slot, diag=None, lo=0):
        off = pl.multiple_of(j * tk, tk)
        for a in range(2):
            s = s_ref[slot, a, :, lo:]
            if diag is not None:
                kpos = diag * tk + lax.broadcasted_iota(jnp.int32, (tk, tq - lo), 0)
                qpos = lo + lax.broadcasted_iota(jnp.int32, (tk, tq - lo), 1)
                s = jnp.where(kpos <= qpos, s, MASK_VALUE)
            m_old = m_ref[a, :, lo:]
            m_new = jnp.maximum(m_old, jnp.max(s, axis=0, keepdims=True))
            p = jnp.exp2(s - m_new)
            alpha = jnp.exp2(m_old - m_new)
            vt = jnp.concatenate([vt_ref[a * HEAD_DIM:(a + 1) * HEAD_DIM, pl.ds(off, tk)], ones_rows], axis=0)
            acc_ref[a, :, lo:] = alpha * acc_ref[a, :, lo:] + _dot(vt, p.astype(bf16))
            m_ref[a, :, lo:] = m_new

    @pl.when(first_block % 2 == 1)
    def _():
        scores(first_block, 1)
        consume(first_block, 1)

    scores(2 * first_pair, 0)

    def body(jj, carry):
        scores(2 * jj + 1, 1)
        consume(2 * jj, 0)
        scores(2 * jj + 2, 0)
        consume(2 * jj + 1, 1)
        return carry

    lax.fori_loop(first_pair, i, body, 0)
    scores(2 * i + 1, 1, lo=tk)
    consume(2 * i, 0, diag=0)
    consume(2 * i + 1, 1, diag=1, lo=tk)
    o_t = jnp.concatenate([acc_ref[a, :HEAD_DIM] / acc_ref[a, HEAD_DIM:HEAD_DIM + 1] for a in range(2)], axis=0)
    o_ref[...] = o_t.T.astype(bf16)


EXP2_UNDERFLOW = 160.0
NORM_SLACK = 1.02


def _fox(fq, fk_aug, fv_t, cend, norm2, *, batch, seq, tq):
    t = fq.shape[0]
    nq = seq // tq
    tk = tq // 2
    n_chunks = seq // CHUNK
    cend_flat = cend.reshape(batch, n_chunks, 8, LANES)[:, :, 0, :N_HEADS_FOX].transpose(0, 2, 1).reshape(-1)
    norm2 = norm2.reshape(batch, 8, LANES)[:, 0, :2 * N_HEADS_FOX].reshape(batch, 2, N_HEADS_FOX)
    thr = (2.0 * NORM_SLACK * jnp.sqrt(norm2[:, 0] * norm2[:, 1]) + EXP2_UNDERFLOW).reshape(-1)
    smem = pl.BlockSpec(memory_space=pltpu.SMEM)
    q_spec = pl.BlockSpec((tq, LANES), lambda b, p, i: (b * nq + i, p))
    return pl.pallas_call(
        functools.partial(_fox_kernel, tq=tq, tk=tk, n_chunks=n_chunks),
        grid=(batch, N_PAIRS, nq),
        in_specs=[smem, smem, q_spec,
                  pl.BlockSpec((seq, 2 * LANES), lambda b, p, i: (b, p)),
                  pl.BlockSpec((LANES, seq), lambda b, p, i: (p, b))],
        out_specs=q_spec,
        out_shape=jax.ShapeDtypeStruct((t, W_FOX), bf16),
        scratch_shapes=[pltpu.VMEM((2, 2, tk, tq), f32), pltpu.VMEM((2, 1, tq), f32),
                        pltpu.VMEM((2, HEAD_DIM + SUM_ROWS, tq), f32)],
        compiler_params=_params("parallel", "parallel", "arbitrary"),
        name="fox_attention",
    )(cend_flat, thr, fq, fk_aug, fv_t)


def _channel_kernel(a_ref, b_ref, c_ref, x_ref, p_ref, wo_ref, gmix_ref,
                    gpre_ref, wg_ref, wu_ref, cw_ref, cb_ref, wd_ref, gpost_ref,
                    gple_ref, wpg_ref, wpp_ref, gplepost_ref, o_ref, halo_ref,
                    *, tm, fc, blocks_per_seq):
    i = pl.program_id(0)

    @pl.when(i % blocks_per_seq == 0)
    def _():
        halo_ref[...] = jnp.zeros_like(halo_ref)

    cat = jnp.concatenate([a_ref[...], b_ref[...], c_ref[...]], axis=1)
    xf = x_ref[...] + _rms(_dot(cat, wo_ref[...]), gmix_ref[...])

    h = _rms(xf, gpre_ref[...]).astype(bf16)
    row8 = lax.broadcasted_iota(jnp.int32, (8, fc), 0)
    acc = jnp.zeros((tm, D_MODEL), f32)
    for c in range(D_FF // fc):
        cols = slice(c * fc, (c + 1) * fc)
        g = _dot(h, wg_ref[:, cols])
        up = _dot(h, wu_ref[:, cols])
        prev = halo_ref[:, cols]
        halo_ref[:, cols] = g[tm - 8:, :]
        g1 = pltpu.roll(g, 1, 0)
        g2 = pltpu.roll(g, 2, 0)
        g1 = jnp.concatenate([jnp.where(row8 < 1, pltpu.roll(prev, 1, 0), g1[:8]), g1[8:]], axis=0)
        g2 = jnp.concatenate([jnp.where(row8 < 2, pltpu.roll(prev, 2, 0), g2[:8]), g2[8:]], axis=0)
        conv = cb_ref[:, cols] + g2 * cw_ref[0:1, cols]
        conv = conv + g1 * cw_ref[1:2, cols]
        conv = conv + g * cw_ref[2:3, cols]
        act = _gelu(conv) * up
        acc = acc + _dot(act.astype(bf16), wd_ref[cols, :])
    xf = xf + _rms(acc, gpost_ref[...])

    gate = jax.nn.sigmoid(_dot(_rms(xf, gple_ref[...]).astype(bf16), wpg_ref[...]))
    e = _dot(p_ref[...].astype(bf16), wpp_ref[...])
    o_ref[...] = xf + _rms(e * gate, gplepost_ref[...])


def _channel(out_a, out_b, out_c, x2d, p2d, w_o, g_mix, g_pre, w_gate, w_up, conv_w, conv_b, w_down, g_post,
             g_ple, w_ple_gate, w_ple_proj, g_ple_post, *, layer, seq, tm, fc):
    t = x2d.shape[0]
    row = lambda width: pl.BlockSpec((tm, width), lambda i: (i, 0))
    gain = _resident((1, D_MODEL))
    kern = functools.partial(_channel_kernel, tm=tm, fc=fc, blocks_per_seq=seq // tm)
    return pl.pallas_call(
        kern,
        grid=(t // tm,),
        in_specs=[row(W_SGU), row(W_RET), row(W_FOX), row(D_MODEL),
                  pl.BlockSpec((tm, D_PLE), lambda i: (layer * (t // tm) + i, 0)),
                  _resident((D_MODEL, D_MODEL), layer), gain,
                  gain, _resident((D_MODEL, D_FF), layer), _resident((D_MODEL, D_FF), layer),
                  _resident(conv_w.shape), _resident((1, D_FF)), _resident((D_FF, D_MODEL), layer), gain,
                  gain, _resident((D_MODEL, D_MODEL), layer), _resident((D_PLE, D_MODEL), layer), gain],
        out_specs=row(D_MODEL),
        out_shape=jax.ShapeDtypeStruct((t, D_MODEL), f32),
        scratch_shapes=[pltpu.VMEM((8, D_FF), f32)],
        compiler_params=_params("arbitrary"),
        name="channel",
    )(out_a, out_b, out_c, x2d, p2d, w_o, g_mix, g_pre, w_gate, w_up, conv_w, conv_b, w_down, g_post,
      g_ple, w_ple_gate, w_ple_proj, g_ple_post)


def _largest_tile(seq, cap):
    tile = cap
    while seq % tile:
        tile //= 2
    return tile


def kernel(x, p, positions, mix_pre_g, w_in, sgu_v_g, sgu_w, sgu_b, fox_b_f, w_o, mix_post_g, ffn_pre_g,
           w_gate, w_up, conv_w, conv_b, w_down, ffn_post_g, ple_pre_g, w_ple_gate, w_ple_proj, ple_post_g):
    batch, seq, _ = x.shape
    depth = w_in.shape[0]
    t = batch * seq
    assert seq % CHUNK == 0
    tm = _largest_tile(seq, 512)
    tm_in = _largest_tile(seq, 1024)
    tq = _largest_tile(seq, 1024)
    fc = 2048

    cos4, sin4 = _rope_tables(positions)
    x2d = x.reshape(t, D_MODEL)
    p2d = p.reshape(depth * t, D_PLE)
    w_pad = jnp.pad(w_in.astype(bf16), ((0, 0), (0, 0), (0, N_IN_PAD - w_in.shape[2])))
    w_o, w_gate, w_up, w_down, w_ple_gate, w_ple_proj = (
        w.astype(bf16) for w in (w_o, w_gate, w_up, w_down, w_ple_gate, w_ple_proj))
    for i in range(depth):
        sgu_b_full = jnp.repeat(sgu_b[i].T, HEAD_DIM, axis=1)
        fox_b = jnp.pad(fox_b_f[i], (0, LANES - N_HEADS_FOX))[None, :]
        out_a, rq, rk, rv, rg, fq, fk_aug, fv_t, cend, norm2 = _proj_in(
            x2d, mix_pre_g[i][None, :], w_pad, cos4, sin4, sgu_w[i], sgu_b_full,
            sgu_v_g[i].reshape(1, W_SGU), fox_b, layer=i, seq=seq, tm=tm_in)
        out_b = _retention(rq, rk, rv, rg, seq=seq, tm=tm_in)
        out_c = _fox(fq, fk_aug, fv_t, cend, norm2, batch=batch, seq=seq, tq=tq)
        x2d = _channel(out_a, out_b, out_c, x2d, p2d, w_o, mix_post_g[i][None, :],
                       ffn_pre_g[i][None, :], w_gate, w_up, conv_w[i], conv_b[i][None, :], w_down,
                       ffn_post_g[i][None, :], ple_pre_g[i][None, :], w_ple_gate, w_ple_proj,
                       ple_post_g[i][None, :], layer=i, seq=seq, tm=tm, fc=fc)
    return x2d.reshape(batch, seq, D_MODEL)
```

```python
import functools

import numpy as np
import jax
import jax.numpy as jnp
from jax import lax
from jax.experimental import pallas as pl
from jax.experimental.pallas import tpu as pltpu

D_MODEL = 1024
D_PLE = 256
HEAD_DIM = 64
N_HEADS_SGU = 4
N_HEADS_RET = 6
N_HEADS_FOX = 6
W_SGU = N_HEADS_SGU * HEAD_DIM
W_RET = N_HEADS_RET * HEAD_DIM
W_FOX = N_HEADS_FOX * HEAD_DIM
CHUNK = 128
D_FF = 4 * D_MODEL
ROPE_BASE = 10000.0
RMS_EPS = 1e-6
LANES = 128
N_PAIRS = W_RET // LANES
N_IN_PAD = 2 * W_SGU + 4 * W_RET + 3 * W_FOX + LANES
MASK_VALUE = -1e30
LOG2E = 1.4426950408889634
C_TERMS = 3
CUMSUM_ROWS = 512
VMEM_LIMIT = 56 * 1024 * 1024

f32 = jnp.float32
bf16 = jnp.bfloat16


def _params(*sem):
    return pltpu.CompilerParams(dimension_semantics=sem, vmem_limit_bytes=VMEM_LIMIT)


def _resident(shape, layer=None):
    zeros = (0,) * len(shape)
    if layer is None:
        return pl.BlockSpec(shape, lambda *_: zeros, pipeline_mode=pl.Buffered(1))
    return pl.BlockSpec((None,) + tuple(shape), lambda *_: (layer,) + zeros, pipeline_mode=pl.Buffered(1))


def _rms(xf, g):
    y = xf * lax.rsqrt(jnp.mean(xf * xf, axis=-1, keepdims=True) + RMS_EPS)
    return y * g


def _dot(a, b):
    return jnp.dot(a, b, preferred_element_type=f32)


def _dot_nt(a, b):
    return lax.dot_general(a, b, (((1,), (1,)), ((), ())), preferred_element_type=f32)


def _dot_tn(a, b):
    return lax.dot_general(a, b, (((0,), (0,)), ((), ())), preferred_element_type=f32)


def _split_bf16(x, terms):
    parts = []
    r = x
    for _ in range(terms):
        p = r.astype(bf16)
        parts.append(p)
        r = r - p.astype(f32)
    return parts


def _group_mean(y, width):
    r = lax.broadcasted_iota(jnp.int32, (width, width), 0) // HEAD_DIM
    c = lax.broadcasted_iota(jnp.int32, (width, width), 1) // HEAD_DIM
    gm = jnp.where(r == c, 1.0 / HEAD_DIM, 0.0).astype(bf16)
    return _dot(y.astype(bf16), gm)


def _gelu(x):
    return jax.nn.gelu(x, approximate=True)


ROPE_HALF = HEAD_DIM // 2
ROPE_GROUPS = LANES // ROPE_HALF


def _rope_kernel(pos_ref, invf_ref, cos_ref, sin_ref):
    ang = pos_ref[...].astype(f32) * invf_ref[...]
    c = jnp.cos(ang)
    s = jnp.sin(ang)
    lane = lax.broadcasted_iota(jnp.int32, ang.shape, 1)
    first_group = lane < ROPE_HALF
    sign = jnp.where((lane % HEAD_DIM) < ROPE_HALF, -1.0, 1.0)

    def spread(v, j):
        v = pltpu.roll(v, LANES - ROPE_HALF * j, 1) if j else v
        v = jnp.where(first_group, v, 0.0)
        v = v + pltpu.roll(v, ROPE_HALF, 1)
        return v + pltpu.roll(v, 2 * ROPE_HALF, 1)

    for j in range(ROPE_GROUPS):
        cos_ref[j] = spread(c, j)
        sin_ref[j] = spread(s, j) * sign


def _rope_tables(positions):
    t = positions.size
    rows = t // ROPE_GROUPS
    pos = jnp.repeat(positions.reshape(ROPE_GROUPS, rows).T, ROPE_HALF, axis=1)
    inv_freq = ROPE_BASE ** (-jnp.arange(ROPE_HALF, dtype=f32) / ROPE_HALF)
    invf = jnp.tile(inv_freq, ROPE_GROUPS)[None, :]
    tr = min(rows, 1024)
    out_spec = pl.BlockSpec((ROPE_GROUPS, tr, LANES), lambda i: (0, i, 0))
    cos, sin = pl.pallas_call(
        _rope_kernel,
        grid=(rows // tr,),
        in_specs=[pl.BlockSpec((tr, LANES), lambda i: (i, 0)), _resident((1, LANES))],
        out_specs=[out_spec, out_spec],
        out_shape=[jax.ShapeDtypeStruct((ROPE_GROUPS, rows, LANES), f32)] * 2,
        compiler_params=_params("parallel"),
        name="rope_tables",
    )(pos, invf)
    return cos.reshape(t, LANES), sin.reshape(t, LANES)


def _proj_in_kernel(x_ref, g_ref, w_ref, cos_ref, sin_ref, sguw_ref, sgub_ref, vg_ref, fb_ref, place_ref,
                    oa_ref, rq_ref, rk_ref, rv_ref, rg_ref, fq_ref, fk_ref, fvt_ref, cend_ref, nrm_ref,
                    carry_ref, fv_ref, *, tm, blocks_per_seq):
    i = pl.program_id(0)
    n_chunks = tm // CHUNK
    h = _rms(x_ref[...], g_ref[...]).astype(bf16)

    def proj(lo, width):
        return _dot(h, w_ref[:, lo:lo + width])

    z_fox_vf = proj(2 * W_SGU + 4 * W_RET + 2 * W_FOX, W_FOX + LANES)

    ff = z_fox_vf[:, W_FOX:] + fb_ref[...]
    logf = -(jnp.maximum(-ff, 0.0) + jnp.log1p(jnp.exp(-jnp.abs(ff))))

    @pl.when(i % blocks_per_seq == 0)
    def _():
        carry_ref[...] = jnp.zeros_like(carry_ref)

    sub = min(tm, CUMSUM_ROWS)
    sub_tril = (lax.broadcasted_iota(jnp.int32, (sub, sub), 0) >= lax.broadcasted_iota(jnp.int32, (sub, sub), 1))
    sub_tril = jnp.where(sub_tril, 1.0, 0.0).astype(bf16)
    carry = carry_ref[...]
    cums = []
    for r in range(tm // sub):
        within = _dot(sub_tril, jnp.concatenate(_split_bf16(logf[r * sub:(r + 1) * sub], 3), axis=1))
        cums.append(within[:, :LANES] + within[:, LANES:2 * LANES] + within[:, 2 * LANES:] + carry)
        carry = cums[-1][sub - 1:sub, :]
    carry_ref[...] = carry
    cum = jnp.concatenate(cums, axis=0) * LOG2E
    for c in range(n_chunks):
        cend_ref[8 * c:8 * (c + 1), :] = jnp.broadcast_to(cum[(c + 1) * CHUNK - 1:(c + 1) * CHUNK, :], (8, LANES))
    placed = _dot(jnp.concatenate(_split_bf16(cum, C_TERMS), axis=1), place_ref[...]).astype(bf16)
    for p in range(N_PAIRS):
        fk_ref[:, (2 * p + 1) * LANES:(2 * p + 2) * LANES] = placed[:, p * LANES:(p + 1) * LANES]

    sq_row = lax.broadcasted_iota(jnp.int32, (CHUNK, CHUNK), 0)
    sq_col = lax.broadcasted_iota(jnp.int32, (CHUNK, CHUNK), 1)
    tril = sq_row >= sq_col
    first_head = sq_col < HEAD_DIM

    z_sgu = _gelu(proj(0, 2 * W_SGU))
    u = z_sgu[:, :W_SGU]
    v = z_sgu[:, W_SGU:]
    v = v * lax.rsqrt(_group_mean(v * v, W_SGU) + RMS_EPS) * vg_ref[...]
    for p in range(W_SGU // LANES):
        cols = slice(p * LANES, (p + 1) * LANES)
        w_pair = jnp.concatenate(
            [jnp.where(tril, sguw_ref[2 * p], 0.0), jnp.where(tril, sguw_ref[2 * p + 1], 0.0)],
            axis=0).astype(bf16)
        for c in range(n_chunks):
            rows = slice(c * CHUNK, (c + 1) * CHUNK)
            r = _dot(w_pair, v[rows, cols].astype(bf16))
            s = jnp.where(first_head, r[:CHUNK], r[CHUNK:]) + sgub_ref[:, cols]
            oa_ref[rows, cols] = (u[rows, cols] * s).astype(bf16)

    cosv = cos_ref[...]
    sinv = sin_ref[...]
    lane = lax.broadcasted_iota(jnp.int32, (tm, LANES), 1)
    first_half = (lane % HEAD_DIM) < (HEAD_DIM // 2)

    def rotary(z):
        outs = []
        for s in range(N_PAIRS):
            t = z[:, s * LANES:(s + 1) * LANES]
            partner = jnp.where(first_half, pltpu.roll(t, LANES - HEAD_DIM // 2, 1),
                                pltpu.roll(t, HEAD_DIM // 2, 1))
            outs.append(t * cosv + partner * sinv)
        return jnp.concatenate(outs, axis=1)

    base = 2 * W_SGU
    z_qk = proj(base, 2 * W_RET)
    rq_ref[...] = rotary(z_qk[:, :W_RET]).astype(bf16)
    rk_ref[...] = (rotary(z_qk[:, W_RET:]) * (HEAD_DIM ** -0.5)).astype(bf16)
    z_vg = proj(base + 2 * W_RET, 2 * W_RET).astype(bf16)
    rv_ref[...] = z_vg[:, :W_RET]
    rg_ref[...] = z_vg[:, W_RET:]

    base += 4 * W_RET
    z_fox_qk = proj(base, 2 * W_FOX)
    fq = (z_fox_qk[:, :W_FOX] * (LOG2E * HEAD_DIM ** -0.5)).astype(bf16)
    fq_ref[...] = fq
    fk = z_fox_qk[:, W_FOX:].astype(bf16)
    for p in range(N_PAIRS):
        fk_ref[:, 2 * p * LANES:(2 * p + 1) * LANES] = fk[:, p * LANES:(p + 1) * LANES]
    fv_ref[...] = z_fox_vf[:, :W_FOX]
    fvt_ref[...] = fv_ref[...].T.astype(bf16)

    gr = lax.broadcasted_iota(jnp.int32, (2 * W_FOX, LANES), 0) // HEAD_DIM
    gc = lax.broadcasted_iota(jnp.int32, (2 * W_FOX, LANES), 1)
    head_sum = jnp.where(gr == gc, 1.0, 0.0).astype(bf16)
    qk = jnp.concatenate([fq, fk], axis=1).astype(f32)
    norm2 = jnp.max(_dot((qk * qk).astype(bf16), head_sum), axis=0, keepdims=True)

    @pl.when(i % blocks_per_seq == 0)
    def _():
        nrm_ref[...] = jnp.zeros_like(nrm_ref)

    nrm_ref[...] = jnp.maximum(nrm_ref[...], norm2)


def _placement_matrix():
    place = np.zeros((C_TERMS * LANES, N_PAIRS * LANES), np.float32)
    for h in range(N_HEADS_FOX):
        for term in range(C_TERMS):
            place[term * LANES + h, (h // 2) * LANES + C_TERMS * (h % 2) + term] = 1.0
    return jnp.asarray(place, bf16)


def _proj_in(x2d, g, w_pad, cos4, sin4, sgu_w, sgu_b_full, sgu_vg, fox_b, *, layer, seq, tm):
    t = x2d.shape[0]
    row = lambda width: pl.BlockSpec((tm, width), lambda i: (i, 0))
    out_widths = [W_SGU] + [W_RET] * 4 + [W_FOX, 2 * W_FOX]
    place = _placement_matrix()
    kern = functools.partial(_proj_in_kernel, tm=tm, blocks_per_seq=seq // tm)
    return pl.pallas_call(
        kern,
        grid=(t // tm,),
        in_specs=[row(D_MODEL), _resident((1, D_MODEL)), _resident((D_MODEL, N_IN_PAD), layer),
                  row(LANES), row(LANES),
                  _resident((N_HEADS_SGU, CHUNK, CHUNK)), _resident((CHUNK, W_SGU)),
                  _resident((1, W_SGU)), _resident((1, LANES)), _resident(place.shape)],
        out_specs=[row(w) for w in out_widths] + [
            pl.BlockSpec((W_FOX, tm), lambda i: (0, i)),
            pl.BlockSpec((8 * tm // CHUNK, LANES), lambda i: (i, 0)),
            pl.BlockSpec((8, LANES), lambda i: (i // (seq // tm), 0))],
        out_shape=[jax.ShapeDtypeStruct((t, w), bf16) for w in out_widths] + [
            jax.ShapeDtypeStruct((W_FOX, t), bf16),
            jax.ShapeDtypeStruct((8 * t // CHUNK, LANES), f32),
            jax.ShapeDtypeStruct((8 * t // seq, LANES), f32)],
        scratch_shapes=[pltpu.VMEM((1, LANES), f32), pltpu.VMEM((tm, W_FOX), f32)],
        compiler_params=_params("arbitrary"),
        name="proj_in",
    )(x2d, g, w_pad, cos4, sin4, sgu_w, sgu_b_full, sgu_vg, fox_b, place)


def _retention_kernel(q_ref, k_ref, v_ref, g_ref, dec_ref, qd_ref, kd_ref, cd_ref, o_ref, state_ref, raw_ref,
                      *, tm, blocks_per_seq):
    i = pl.program_id(0)

    @pl.when(i % blocks_per_seq == 0)
    def _():
        state_ref[...] = jnp.zeros_like(state_ref)

    row = lax.broadcasted_iota(jnp.int32, (CHUNK, LANES), 0)
    lane = lax.broadcasted_iota(jnp.int32, (CHUNK, LANES), 1)
    first_head = lane < HEAD_DIM
    same_head = (row < HEAD_DIM) == first_head
    zero = jnp.zeros((CHUNK, LANES), bf16)
    for c in range(tm // CHUNK):
        rows = slice(c * CHUNK, (c + 1) * CHUNK)
        for p in range(N_PAIRS):
            cols = slice(p * LANES, (p + 1) * LANES)
            q = q_ref[rows, cols]
            k = k_ref[rows, cols]
            v = v_ref[rows, cols]
            q_heads = jnp.concatenate([jnp.where(first_head, q, zero), jnp.where(first_head, zero, q)], axis=0)
            inner = _dot_nt(q_heads, k) * dec_ref[p]
            o_heads = _dot(inner.astype(bf16), v)
            state = state_ref[p]
            o = (jnp.where(first_head, o_heads[:CHUNK], o_heads[CHUNK:])
                 + _dot(q, state.astype(bf16)) * qd_ref[p])
            k_dec = (k.astype(f32) * kd_ref[p]).astype(bf16)
            state_ref[p] = state * cd_ref[p] + jnp.where(same_head, _dot_tn(k_dec, v), 0.0)
            raw_ref[rows, cols] = o
    o = raw_ref[...]
    ro = o * lax.rsqrt(_group_mean(o * o, W_RET) + RMS_EPS)
    gate = g_ref[...].astype(f32)
    o_ref[...] = (gate * jax.nn.sigmoid(gate) * ro).astype(bf16)


def _retention_tables():
    log_g = jnp.log(1.0 - 2.0 ** (-5.0 - jnp.arange(N_HEADS_RET, dtype=f32)))
    idx = jnp.arange(CHUNK, dtype=f32)
    diff = idx[:, None] - idx[None, :]
    decay = jnp.where(diff >= 0, jnp.exp(log_g[:, None, None] * jnp.maximum(diff, 0.0)), 0.0)
    lane_g = jnp.repeat(log_g, HEAD_DIM).reshape(N_PAIRS, 1, LANES)
    q_decay = jnp.exp(lane_g * (idx + 1.0)[None, :, None])
    k_decay = jnp.exp(lane_g * (CHUNK - 1.0 - idx)[None, :, None])
    chunk_decay = jnp.broadcast_to(jnp.exp(lane_g * CHUNK).reshape(N_PAIRS, LANES, 1), (N_PAIRS, LANES, LANES))
    pair_decay = decay.reshape(N_PAIRS, 2 * CHUNK, CHUNK)
    return pair_decay, q_decay, k_decay, chunk_decay


def _retention(rq, rk, rv, rg, *, seq, tm):
    t = rq.shape[0]
    decay, q_decay, k_decay, chunk_decay = _retention_tables()
    row = pl.BlockSpec((tm, W_RET), lambda i: (i, 0))
    kern = functools.partial(_retention_kernel, tm=tm, blocks_per_seq=seq // tm)
    return pl.pallas_call(
        kern,
        grid=(t // tm,),
        in_specs=[row, row, row, row, _resident(decay.shape), _resident(q_decay.shape),
                  _resident(k_decay.shape), _resident(chunk_decay.shape)],
        out_specs=row,
        out_shape=jax.ShapeDtypeStruct((t, W_RET), bf16),
        scratch_shapes=[pltpu.VMEM((N_PAIRS, LANES, LANES), f32), pltpu.VMEM((tm, W_RET), f32)],
        compiler_params=_params("arbitrary"),
        name="retention",
    )(rq, rk, rv, rg, decay, q_decay, k_decay, chunk_decay)


SUM_ROWS = 16


def _fox_kernel(cend_ref, thr_ref, q_ref, k_ref, vt_ref, o_ref, s_ref, m_ref, acc_ref, *, tq, tk, n_chunks):
    b = pl.program_id(0)
    pair = pl.program_id(1)
    i = pl.program_id(2)

    def first_needed_block():
        heads = [b * N_HEADS_FOX + 2 * pair + a for a in range(2)]
        q_start = jnp.maximum((tq // CHUNK) * i - 1, 0)
        limit = [cend_ref[h * n_chunks + q_start] + thr_ref[h] for h in heads]

        def needed(j):
            key_end = (tk // CHUNK) * (jnp.maximum(j, 0) + 1) - 1
            flags = [cend_ref[h * n_chunks + key_end] <= lim for h, lim in zip(heads, limit)]
            return jnp.logical_and(j >= 0, jnp.logical_or(flags[0], flags[1]))

        return lax.while_loop(needed, lambda j: j - 1, 2 * i - 1) + 1

    first_block = first_needed_block()
    first_pair = (first_block + 1) // 2
    q = q_ref[...]
    lane = lax.broadcasted_iota(jnp.int32, (tq, LANES), 1)
    first_head = lane < HEAD_DIM
    zero = jnp.zeros((tq, LANES), bf16)
    q_aug = []
    for a in range(2):
        qa = jnp.where(first_head, q, zero) if a == 0 else jnp.where(first_head, zero, q)
        pick = jnp.where((lane >= C_TERMS * a) & (lane < C_TERMS * (a + 1)), -1.0, 0.0).astype(bf16)
        q_aug.append(jnp.concatenate([qa, pick], axis=1))
    m_ref[...] = jnp.full_like(m_ref, MASK_VALUE)
    acc_ref[...] = jnp.zeros_like(acc_ref)
    ones_rows = jnp.ones((SUM_ROWS, tk), bf16)

    def scores(j, slot, lo=0):
        off = pl.multiple_of(j * tk, tk)
        k = k_ref[pl.ds(off, tk), :]
        for a in range(2):
            s_ref[slot, a, :, lo:] = _dot_nt(k, q_aug[a][lo:])

    def consume(j, slot, diag=None, lo=0):
        off = pl.multiple_of(j * tk, tk)
        for a in range(2):
            s = s_ref[slot, a, :, lo:]
            if diag is not None:
                kpos = diag * tk + lax.broadcasted_iota(jnp.int32, (tk, tq - lo), 0)
                qpos = lo + lax.broadcasted_iota(jnp.int32, (tk, tq - lo), 1)
                s = jnp.where(kpos <= qpos, s, MASK_VALUE)
            m_old = m_ref[a, :, lo:]
            m_new = jnp.maximum(m_old, jnp.max(s, axis=0, keepdims=True))
            p = jnp.exp2(s - m_new)
            alpha = jnp.exp2(m_old - m_new)
            vt = jnp.concatenate([vt_ref[a * HEAD_DIM:(a + 1) * HEAD_DIM, pl.ds(off, tk)], ones_rows], axis=0)
            acc_ref[a, :, lo:] = alpha * acc_ref[a, :, lo:] + _dot(vt, p.astype(bf16))
            m_ref[a, :, lo:] = m_new

    @pl.when(first_block % 2 == 1)
    def _():
        scores(first_block, 1)
        consume(first_block, 1)

    scores(2 * first_pair, 0)

    def body(jj, carry):
        scores(2 * jj + 1, 1)
        consume(2 * jj, 0)
        scores(2 * jj + 2, 0)
        consume(2 * jj + 1, 1)
        return carry

    lax.fori_loop(first_pair, i, body, 0)
    scores(2 * i + 1, 1, lo=tk)
    consume(2 * i, 0, diag=0)
    consume(2 * i + 1, 1, diag=1, lo=tk)
    o_t = jnp.concatenate([acc_ref[a, :HEAD_DIM] / acc_ref[a, HEAD_DIM:HEAD_DIM + 1] for a in range(2)], axis=0)
    o_ref[...] = o_t.T.astype(bf16)


EXP2_UNDERFLOW = 160.0
NORM_SLACK = 1.02


def _fox(fq, fk_aug, fv_t, cend, norm2, *, batch, seq, tq):
    t = fq.shape[0]
    nq = seq // tq
    tk = tq // 2
    n_chunks = seq // CHUNK
    cend_flat = cend.reshape(batch, n_chunks, 8, LANES)[:, :, 0, :N_HEADS_FOX].transpose(0, 2, 1).reshape(-1)
    norm2 = norm2.reshape(batch, 8, LANES)[:, 0, :2 * N_HEADS_FOX].reshape(batch, 2, N_HEADS_FOX)
    thr = (2.0 * NORM_SLACK * jnp.sqrt(norm2[:, 0] * norm2[:, 1]) + EXP2_UNDERFLOW).reshape(-1)
    smem = pl.BlockSpec(memory_space=pltpu.SMEM)
    q_spec = pl.BlockSpec((tq, LANES), lambda b, p, i: (b * nq + i, p))
    return pl.pallas_call(
        functools.partial(_fox_kernel, tq=tq, tk=tk, n_chunks=n_chunks),
        grid=(batch, N_PAIRS, nq),
        in_specs=[smem, smem, q_spec,
                  pl.BlockSpec((seq, 2 * LANES), lambda b, p, i: (b, p)),
                  pl.BlockSpec((LANES, seq), lambda b, p, i: (p, b))],
        out_specs=q_spec,
        out_shape=jax.ShapeDtypeStruct((t, W_FOX), bf16),
        scratch_shapes=[pltpu.VMEM((2, 2, tk, tq), f32), pltpu.VMEM((2, 1, tq), f32),
                        pltpu.VMEM((2, HEAD_DIM + SUM_ROWS, tq), f32)],
        compiler_params=_params("parallel", "parallel", "arbitrary"),
        name="fox_attention",
    )(cend_flat, thr, fq, fk_aug, fv_t)


def _channel_kernel(a_ref, b_ref, c_ref, x_ref, p_ref, wo_ref, gmix_ref,
                    gpre_ref, wg_ref, wu_ref, cw_ref, cb_ref, wd_ref, gpost_ref,
                    gple_ref, wpg_ref, wpp_ref, gplepost_ref, o_ref, halo_ref,
                    *, tm, fc, blocks_per_seq):
    i = pl.program_id(0)

    @pl.when(i % blocks_per_seq == 0)
    def _():
        halo_ref[...] = jnp.zeros_like(halo_ref)

    cat = jnp.concatenate([a_ref[...], b_ref[...], c_ref[...]], axis=1)
    xf = x_ref[...] + _rms(_dot(cat, wo_ref[...]), gmix_ref[...])

    h = _rms(xf, gpre_ref[...]).astype(bf16)
    row8 = lax.broadcasted_iota(jnp.int32, (8, fc), 0)
    acc = jnp.zeros((tm, D_MODEL), f32)
    for c in range(D_FF // fc):
        cols = slice(c * fc, (c + 1) * fc)
        g = _dot(h, wg_ref[:, cols])
        up = _dot(h, wu_ref[:, cols])
        prev = halo_ref[:, cols]
        halo_ref[:, cols] = g[tm - 8:, :]
        g1 = pltpu.roll(g, 1, 0)
        g2 = pltpu.roll(g, 2, 0)
        g1 = jnp.concatenate([jnp.where(row8 < 1, pltpu.roll(prev, 1, 0), g1[:8]), g1[8:]], axis=0)
        g2 = jnp.concatenate([jnp.where(row8 < 2, pltpu.roll(prev, 2, 0), g2[:8]), g2[8:]], axis=0)
        conv = cb_ref[:, cols] + g2 * cw_ref[0:1, cols]
        conv = conv + g1 * cw_ref[1:2, cols]
        conv = conv + g * cw_ref[2:3, cols]
        act = _gelu(conv) * up
        acc = acc + _dot(act.astype(bf16), wd_ref[cols, :])
    xf = xf + _rms(acc, gpost_ref[...])

    gate = jax.nn.sigmoid(_dot(_rms(xf, gple_ref[...]).astype(bf16), wpg_ref[...]))
    e = _dot(p_ref[...].astype(bf16), wpp_ref[...])
    o_ref[...] = xf + _rms(e * gate, gplepost_ref[...])


def _channel(out_a, out_b, out_c, x2d, p2d, w_o, g_mix, g_pre, w_gate, w_up, conv_w, conv_b, w_down, g_post,
             g_ple, w_ple_gate, w_ple_proj, g_ple_post, *, layer, seq, tm, fc):
    t = x2d.shape[0]
    row = lambda width: pl.BlockSpec((tm, width), lambda i: (i, 0))
    gain = _resident((1, D_MODEL))
    kern = functools.partial(_channel_kernel, tm=tm, fc=fc, blocks_per_seq=seq // tm)
    return pl.pallas_call(
        kern,
        grid=(t // tm,),
        in_specs=[row(W_SGU), row(W_RET), row(W_FOX), row(D_MODEL),
                  pl.BlockSpec((tm, D_PLE), lambda i: (layer * (t // tm) + i, 0)),
                  _resident((D_MODEL, D_MODEL), layer), gain,
                  gain, _resident((D_MODEL, D_FF), layer), _resident((D_MODEL, D_FF), layer),
                  _resident(conv_w.shape), _resident((1, D_FF)), _resident((D_FF, D_MODEL), layer), gain,
                  gain, _resident((D_MODEL, D_MODEL), layer), _resident((D_PLE, D_MODEL), layer), gain],
        out_specs=row(D_MODEL),
        out_shape=jax.ShapeDtypeStruct((t, D_MODEL), f32),
        scratch_shapes=[pltpu.VMEM((8, D_FF), f32)],
        compiler_params=_params("arbitrary"),
        name="channel",
    )(out_a, out_b, out_c, x2d, p2d, w_o, g_mix, g_pre, w_gate, w_up, conv_w, conv_b, w_down, g_post,
      g_ple, w_ple_gate, w_ple_proj, g_ple_post)


def _largest_tile(seq, cap):
    tile = cap
    while seq % tile:
        tile //= 2
    return tile


def kernel(x, p, positions, mix_pre_g, w_in, sgu_v_g, sgu_w, sgu_b, fox_b_f, w_o, mix_post_g, ffn_pre_g,
           w_gate, w_up, conv_w, conv_b, w_down, ffn_post_g, ple_pre_g, w_ple_gate, w_ple_proj, ple_post_g):
    batch, seq, _ = x.shape
    depth = w_in.shape[0]
    t = batch * seq
    assert seq % CHUNK == 0
    tm = _largest_tile(seq, 512)
    tm_in = _largest_tile(seq, 1024)
    tq = _largest_tile(seq, 512)
    fc = 2048

    cos4, sin4 = _rope_tables(positions)
    x2d = x.reshape(t, D_MODEL)
    p2d = p.reshape(depth * t, D_PLE)
    w_pad = jnp.pad(w_in.astype(bf16), ((0, 0), (0, 0), (0, N_IN_PAD - w_in.shape[2])))
    w_o, w_gate, w_up, w_down, w_ple_gate, w_ple_proj = (
        w.astype(bf16) for w in (w_o, w_gate, w_up, w_down, w_ple_gate, w_ple_proj))
    for i in range(depth):
        sgu_b_full = jnp.repeat(sgu_b[i].T, HEAD_DIM, axis=1)
        fox_b = jnp.pad(fox_b_f[i], (0, LANES - N_HEADS_FOX))[None, :]
        out_a, rq, rk, rv, rg, fq, fk_aug, fv_t, cend, norm2 = _proj_in(
            x2d, mix_pre_g[i][None, :], w_pad, cos4, sin4, sgu_w[i], sgu_b_full,
            sgu_v_g[i].reshape(1, W_SGU), fox_b, layer=i, seq=seq, tm=tm_in)
        out_b = _retention(rq, rk, rv, rg, seq=seq, tm=tm_in)
        out_c = _fox(fq, fk_aug, fv_t, cend, norm2, batch=batch, seq=seq, tq=tq)
        x2d = _channel(out_a, out_b, out_c, x2d, p2d, w_o, mix_post_g[i][None, :],
                       ffn_pre_g[i][None, :], w_gate, w_up, conv_w[i], conv_b[i][None, :], w_down,
                       ffn_post_g[i][None, :], ple_pre_g[i][None, :], w_ple_gate, w_ple_proj,
                       ple_post_g[i][None, :], layer=i, seq=seq, tm=tm, fc=fc)
    return x2d.reshape(batch, seq, D_MODEL)
```

```python
import functools

import numpy as np
import jax
import jax.numpy as jnp
from jax import lax
from jax.experimental import pallas as pl
from jax.experimental.pallas import tpu as pltpu

D_MODEL = 1024
D_PLE = 256
HEAD_DIM = 64
N_HEADS_SGU = 4
N_HEADS_RET = 6
N_HEADS_FOX = 6
W_SGU = N_HEADS_SGU * HEAD_DIM
W_RET = N_HEADS_RET * HEAD_DIM
W_FOX = N_HEADS_FOX * HEAD_DIM
CHUNK = 128
D_FF = 4 * D_MODEL
ROPE_BASE = 10000.0
RMS_EPS = 1e-6
LANES = 128
N_PAIRS = W_RET // LANES
N_IN_PAD = 2 * W_SGU + 4 * W_RET + 3 * W_FOX + LANES
MASK_VALUE = -1e30
LOG2E = 1.4426950408889634
C_TERMS = 3
CUMSUM_ROWS = 512
RET_CHUNK = 256
VMEM_LIMIT = 56 * 1024 * 1024

f32 = jnp.float32
bf16 = jnp.bfloat16


def _params(*sem):
    return pltpu.CompilerParams(dimension_semantics=sem, vmem_limit_bytes=VMEM_LIMIT)


def _resident(shape, layer=None):
    zeros = (0,) * len(shape)
    if layer is None:
        return pl.BlockSpec(shape, lambda *_: zeros, pipeline_mode=pl.Buffered(1))
    return pl.BlockSpec((None,) + tuple(shape), lambda *_: (layer,) + zeros, pipeline_mode=pl.Buffered(1))


def _rms(xf, g):
    y = xf * lax.rsqrt(jnp.mean(xf * xf, axis=-1, keepdims=True) + RMS_EPS)
    return y * g


def _dot(a, b):
    return jnp.dot(a, b, preferred_element_type=f32)


def _dot_nt(a, b):
    return lax.dot_general(a, b, (((1,), (1,)), ((), ())), preferred_element_type=f32)


def _dot_tn(a, b):
    return lax.dot_general(a, b, (((0,), (0,)), ((), ())), preferred_element_type=f32)


def _split_bf16(x, terms):
    parts = []
    r = x
    for _ in range(terms):
        p = r.astype(bf16)
        parts.append(p)
        r = r - p.astype(f32)
    return parts


def _group_mean(y, width):
    r = lax.broadcasted_iota(jnp.int32, (width, width), 0) // HEAD_DIM
    c = lax.broadcasted_iota(jnp.int32, (width, width), 1) // HEAD_DIM
    gm = jnp.where(r == c, 1.0 / HEAD_DIM, 0.0).astype(bf16)
    return _dot(y.astype(bf16), gm)


def _gelu(x):
    return jax.nn.gelu(x, approximate=True)


ROPE_HALF = HEAD_DIM // 2
ROPE_GROUPS = LANES // ROPE_HALF


def _rope_kernel(pos_ref, invf_ref, cos_ref, sin_ref):
    ang = pos_ref[...].astype(f32) * invf_ref[...]
    c = jnp.cos(ang)
    s = jnp.sin(ang)
    lane = lax.broadcasted_iota(jnp.int32, ang.shape, 1)
    first_group = lane < ROPE_HALF
    sign = jnp.where((lane % HEAD_DIM) < ROPE_HALF, -1.0, 1.0)

    def spread(v, j):
        v = pltpu.roll(v, LANES - ROPE_HALF * j, 1) if j else v
        v = jnp.where(first_group, v, 0.0)
        v = v + pltpu.roll(v, ROPE_HALF, 1)
        return v + pltpu.roll(v, 2 * ROPE_HALF, 1)

    for j in range(ROPE_GROUPS):
        cos_ref[j] = spread(c, j)
        sin_ref[j] = spread(s, j) * sign


def _rope_tables(positions):
    t = positions.size
    rows = t // ROPE_GROUPS
    pos = jnp.repeat(positions.reshape(ROPE_GROUPS, rows).T, ROPE_HALF, axis=1)
    inv_freq = ROPE_BASE ** (-jnp.arange(ROPE_HALF, dtype=f32) / ROPE_HALF)
    invf = jnp.tile(inv_freq, ROPE_GROUPS)[None, :]
    tr = min(rows, 1024)
    out_spec = pl.BlockSpec((ROPE_GROUPS, tr, LANES), lambda i: (0, i, 0))
    cos, sin = pl.pallas_call(
        _rope_kernel,
        grid=(rows // tr,),
        in_specs=[pl.BlockSpec((tr, LANES), lambda i: (i, 0)), _resident((1, LANES))],
        out_specs=[out_spec, out_spec],
        out_shape=[jax.ShapeDtypeStruct((ROPE_GROUPS, rows, LANES), f32)] * 2,
        compiler_params=_params("parallel"),
        name="rope_tables",
    )(pos, invf)
    return cos.reshape(t, LANES), sin.reshape(t, LANES)


def _proj_in_kernel(x_ref, g_ref, w_ref, cos_ref, sin_ref, sguw_ref, sgub_ref, vg_ref, fb_ref, place_ref,
                    oa_ref, rq_ref, rk_ref, rv_ref, rg_ref, fq_ref, fk_ref, fvt_ref, cend_ref, nrm_ref,
                    carry_ref, fv_ref, *, tm, blocks_per_seq):
    i = pl.program_id(0)
    n_chunks = tm // CHUNK
    h = _rms(x_ref[...], g_ref[...]).astype(bf16)

    def proj(lo, width):
        return _dot(h, w_ref[:, lo:lo + width])

    z_fox_vf = proj(2 * W_SGU + 4 * W_RET + 2 * W_FOX, W_FOX + LANES)

    ff = z_fox_vf[:, W_FOX:] + fb_ref[...]
    logf = -(jnp.maximum(-ff, 0.0) + jnp.log1p(jnp.exp(-jnp.abs(ff))))

    @pl.when(i % blocks_per_seq == 0)
    def _():
        carry_ref[...] = jnp.zeros_like(carry_ref)

    sub = min(tm, CUMSUM_ROWS)
    sub_tril = (lax.broadcasted_iota(jnp.int32, (sub, sub), 0) >= lax.broadcasted_iota(jnp.int32, (sub, sub), 1))
    sub_tril = jnp.where(sub_tril, 1.0, 0.0).astype(bf16)
    carry = carry_ref[...]
    cums = []
    for r in range(tm // sub):
        within = _dot(sub_tril, jnp.concatenate(_split_bf16(logf[r * sub:(r + 1) * sub], 3), axis=1))
        cums.append(within[:, :LANES] + within[:, LANES:2 * LANES] + within[:, 2 * LANES:] + carry)
        carry = cums[-1][sub - 1:sub, :]
    carry_ref[...] = carry
    cum = jnp.concatenate(cums, axis=0) * LOG2E
    for c in range(n_chunks):
        cend_ref[8 * c:8 * (c + 1), :] = jnp.broadcast_to(cum[(c + 1) * CHUNK - 1:(c + 1) * CHUNK, :], (8, LANES))
    placed = _dot(jnp.concatenate(_split_bf16(cum, C_TERMS), axis=1), place_ref[...]).astype(bf16)
    for p in range(N_PAIRS):
        fk_ref[:, (2 * p + 1) * LANES:(2 * p + 2) * LANES] = placed[:, p * LANES:(p + 1) * LANES]

    sq_row = lax.broadcasted_iota(jnp.int32, (CHUNK, CHUNK), 0)
    sq_col = lax.broadcasted_iota(jnp.int32, (CHUNK, CHUNK), 1)
    tril = sq_row >= sq_col
    first_head = sq_col < HEAD_DIM

    z_sgu = _gelu(proj(0, 2 * W_SGU))
    u = z_sgu[:, :W_SGU]
    v = z_sgu[:, W_SGU:]
    v = v * lax.rsqrt(_group_mean(v * v, W_SGU) + RMS_EPS) * vg_ref[...]
    for p in range(W_SGU // LANES):
        cols = slice(p * LANES, (p + 1) * LANES)
        w_pair = jnp.concatenate(
            [jnp.where(tril, sguw_ref[2 * p], 0.0), jnp.where(tril, sguw_ref[2 * p + 1], 0.0)],
            axis=0).astype(bf16)
        for c in range(n_chunks):
            rows = slice(c * CHUNK, (c + 1) * CHUNK)
            r = _dot(w_pair, v[rows, cols].astype(bf16))
            s = jnp.where(first_head, r[:CHUNK], r[CHUNK:]) + sgub_ref[:, cols]
            oa_ref[rows, cols] = (u[rows, cols] * s).astype(bf16)

    cosv = cos_ref[...]
    sinv = sin_ref[...]
    lane = lax.broadcasted_iota(jnp.int32, (tm, LANES), 1)
    first_half = (lane % HEAD_DIM) < (HEAD_DIM // 2)

    def rotary(z):
        outs = []
        for s in range(N_PAIRS):
            t = z[:, s * LANES:(s + 1) * LANES]
            partner = jnp.where(first_half, pltpu.roll(t, LANES - HEAD_DIM // 2, 1),
                                pltpu.roll(t, HEAD_DIM // 2, 1))
            outs.append(t * cosv + partner * sinv)
        return jnp.concatenate(outs, axis=1)

    base = 2 * W_SGU
    z_qk = proj(base, 2 * W_RET)
    rq_ref[...] = rotary(z_qk[:, :W_RET]).astype(bf16)
    rk_ref[...] = (rotary(z_qk[:, W_RET:]) * (HEAD_DIM ** -0.5)).astype(bf16)
    z_vg = proj(base + 2 * W_RET, 2 * W_RET).astype(bf16)
    rv_ref[...] = z_vg[:, :W_RET]
    rg_ref[...] = z_vg[:, W_RET:]

    base += 4 * W_RET
    z_fox_qk = proj(base, 2 * W_FOX)
    fq = (z_fox_qk[:, :W_FOX] * (LOG2E * HEAD_DIM ** -0.5)).astype(bf16)
    fq_ref[...] = fq
    fk = z_fox_qk[:, W_FOX:].astype(bf16)
    for p in range(N_PAIRS):
        fk_ref[:, 2 * p * LANES:(2 * p + 1) * LANES] = fk[:, p * LANES:(p + 1) * LANES]
    fv_ref[...] = z_fox_vf[:, :W_FOX]
    fvt_ref[...] = fv_ref[...].T.astype(bf16)

    gr = lax.broadcasted_iota(jnp.int32, (2 * W_FOX, LANES), 0) // HEAD_DIM
    gc = lax.broadcasted_iota(jnp.int32, (2 * W_FOX, LANES), 1)
    head_sum = jnp.where(gr == gc, 1.0, 0.0).astype(bf16)
    qk = jnp.concatenate([fq, fk], axis=1).astype(f32)
    norm2 = jnp.max(_dot((qk * qk).astype(bf16), head_sum), axis=0, keepdims=True)

    @pl.when(i % blocks_per_seq == 0)
    def _():
        nrm_ref[...] = jnp.zeros_like(nrm_ref)

    nrm_ref[...] = jnp.maximum(nrm_ref[...], norm2)


def _placement_matrix():
    place = np.zeros((C_TERMS * LANES, N_PAIRS * LANES), np.float32)
    for h in range(N_HEADS_FOX):
        for term in range(C_TERMS):
            place[term * LANES + h, (h // 2) * LANES + C_TERMS * (h % 2) + term] = 1.0
    return jnp.asarray(place, bf16)


def _proj_in(x2d, g, w_pad, cos4, sin4, sgu_w, sgu_b_full, sgu_vg, fox_b, *, layer, seq, tm):
    t = x2d.shape[0]
    row = lambda width: pl.BlockSpec((tm, width), lambda i: (i, 0))
    out_widths = [W_SGU] + [W_RET] * 4 + [W_FOX, 2 * W_FOX]
    place = _placement_matrix()
    kern = functools.partial(_proj_in_kernel, tm=tm, blocks_per_seq=seq // tm)
    return pl.pallas_call(
        kern,
        grid=(t // tm,),
        in_specs=[row(D_MODEL), _resident((1, D_MODEL)), _resident((D_MODEL, N_IN_PAD), layer),
                  row(LANES), row(LANES),
                  _resident((N_HEADS_SGU, CHUNK, CHUNK)), _resident((CHUNK, W_SGU)),
                  _resident((1, W_SGU)), _resident((1, LANES)), _resident(place.shape)],
        out_specs=[row(w) for w in out_widths] + [
            pl.BlockSpec((W_FOX, tm), lambda i: (0, i)),
            pl.BlockSpec((8 * tm // CHUNK, LANES), lambda i: (i, 0)),
            pl.BlockSpec((8, LANES), lambda i: (i // (seq // tm), 0))],
        out_shape=[jax.ShapeDtypeStruct((t, w), bf16) for w in out_widths] + [
            jax.ShapeDtypeStruct((W_FOX, t), bf16),
            jax.ShapeDtypeStruct((8 * t // CHUNK, LANES), f32),
            jax.ShapeDtypeStruct((8 * t // seq, LANES), f32)],
        scratch_shapes=[pltpu.VMEM((1, LANES), f32), pltpu.VMEM((tm, W_FOX), f32)],
        compiler_params=_params("arbitrary"),
        name="proj_in",
    )(x2d, g, w_pad, cos4, sin4, sgu_w, sgu_b_full, sgu_vg, fox_b, place)


def _retention_kernel(q_ref, k_ref, v_ref, g_ref, dec_ref, qd_ref, kd_ref, cd_ref, o_ref, state_ref, raw_ref,
                      *, tm, blocks_per_seq):
    i = pl.program_id(0)

    @pl.when(i % blocks_per_seq == 0)
    def _():
        state_ref[...] = jnp.zeros_like(state_ref)

    rc = RET_CHUNK
    first_head = lax.broadcasted_iota(jnp.int32, (rc, LANES), 1) < HEAD_DIM
    same_head = ((lax.broadcasted_iota(jnp.int32, (LANES, LANES), 0) < HEAD_DIM)
                 == (lax.broadcasted_iota(jnp.int32, (LANES, LANES), 1) < HEAD_DIM))
    zero = jnp.zeros((rc, LANES), bf16)
    states = [state_ref[p] for p in range(N_PAIRS)]
    for c in range(tm // rc):
        rows = slice(c * rc, (c + 1) * rc)
        for p in range(N_PAIRS):
            cols = slice(p * LANES, (p + 1) * LANES)
            q = q_ref[rows, cols]
            k = k_ref[rows, cols]
            v = v_ref[rows, cols]
            q_heads = jnp.concatenate([jnp.where(first_head, q, zero), jnp.where(first_head, zero, q)], axis=0)
            inner = _dot_nt(q_heads, k) * dec_ref[p]
            o_heads = _dot(inner.astype(bf16), v)
            o = (jnp.where(first_head, o_heads[:rc], o_heads[rc:])
                 + _dot(q, states[p].astype(bf16)) * qd_ref[p])
            k_dec = (k.astype(f32) * kd_ref[p]).astype(bf16)
            states[p] = states[p] * cd_ref[p] + jnp.where(same_head, _dot_tn(k_dec, v), 0.0)
            raw_ref[rows, cols] = o
    for p in range(N_PAIRS):
        state_ref[p] = states[p]
    o = raw_ref[...]
    ro = o * lax.rsqrt(_group_mean(o * o, W_RET) + RMS_EPS)
    gate = g_ref[...].astype(f32)
    o_ref[...] = (gate * jax.nn.sigmoid(gate) * ro).astype(bf16)


def _retention_tables():
    log_g = jnp.log(1.0 - 2.0 ** (-5.0 - jnp.arange(N_HEADS_RET, dtype=f32)))
    idx = jnp.arange(RET_CHUNK, dtype=f32)
    diff = idx[:, None] - idx[None, :]
    decay = jnp.where(diff >= 0, jnp.exp(log_g[:, None, None] * jnp.maximum(diff, 0.0)), 0.0)
    lane_g = jnp.repeat(log_g, HEAD_DIM).reshape(N_PAIRS, 1, LANES)
    q_decay = jnp.exp(lane_g * (idx + 1.0)[None, :, None])
    k_decay = jnp.exp(lane_g * (RET_CHUNK - 1.0 - idx)[None, :, None])
    chunk_decay = jnp.broadcast_to(jnp.exp(lane_g * RET_CHUNK).reshape(N_PAIRS, LANES, 1),
                                   (N_PAIRS, LANES, LANES))
    pair_decay = decay.reshape(N_PAIRS, 2 * RET_CHUNK, RET_CHUNK)
    return pair_decay, q_decay, k_decay, chunk_decay


def _retention(rq, rk, rv, rg, *, seq, tm):
    t = rq.shape[0]
    assert tm % RET_CHUNK == 0
    decay, q_decay, k_decay, chunk_decay = _retention_tables()
    row = pl.BlockSpec((tm, W_RET), lambda i: (i, 0))
    kern = functools.partial(_retention_kernel, tm=tm, blocks_per_seq=seq // tm)
    return pl.pallas_call(
        kern,
        grid=(t // tm,),
        in_specs=[row, row, row, row, _resident(decay.shape), _resident(q_decay.shape),
                  _resident(k_decay.shape), _resident(chunk_decay.shape)],
        out_specs=row,
        out_shape=jax.ShapeDtypeStruct((t, W_RET), bf16),
        scratch_shapes=[pltpu.VMEM((N_PAIRS, LANES, LANES), f32), pltpu.VMEM((tm, W_RET), f32)],
        compiler_params=_params("arbitrary"),
        name="retention",
    )(rq, rk, rv, rg, decay, q_decay, k_decay, chunk_decay)


SUM_ROWS = 16


def _fox_kernel(cend_ref, thr_ref, q_ref, k_ref, vt_ref, o_ref, s_ref, m_ref, acc_ref, *, tq, tk, n_chunks):
    b = pl.program_id(0)
    pair = pl.program_id(1)
    i = pl.program_id(2)

    def first_needed_block():
        heads = [b * N_HEADS_FOX + 2 * pair + a for a in range(2)]
        q_start = jnp.maximum((tq // CHUNK) * i - 1, 0)
        limit = [cend_ref[h * n_chunks + q_start] + thr_ref[h] for h in heads]

        def needed(j):
            key_end = (tk // CHUNK) * (jnp.maximum(j, 0) + 1) - 1
            flags = [cend_ref[h * n_chunks + key_end] <= lim for h, lim in zip(heads, limit)]
            return jnp.logical_and(j >= 0, jnp.logical_or(flags[0], flags[1]))

        return lax.while_loop(needed, lambda j: j - 1, 2 * i - 1) + 1

    first_block = first_needed_block()
    first_pair = (first_block + 1) // 2
    q = q_ref[...]
    lane = lax.broadcasted_iota(jnp.int32, (tq, LANES), 1)
    first_head = lane < HEAD_DIM
    zero = jnp.zeros((tq, LANES), bf16)
    q_aug = []
    for a in range(2):
        qa = jnp.where(first_head, q, zero) if a == 0 else jnp.where(first_head, zero, q)
        pick = jnp.where((lane >= C_TERMS * a) & (lane < C_TERMS * (a + 1)), -1.0, 0.0).astype(bf16)
        q_aug.append(jnp.concatenate([qa, pick], axis=1))
    m_ref[...] = jnp.full_like(m_ref, MASK_VALUE)
    acc_ref[...] = jnp.zeros_like(acc_ref)
    ones_rows = jnp.ones((SUM_ROWS, tk), bf16)

    def scores(j, slot, lo=0):
        off = pl.multiple_of(j * tk, tk)
        k = k_ref[pl.ds(off, tk), :]
        for a in range(2):
            s_ref[slot, a, :, lo:] = _dot_nt(k, q_aug[a][lo:])

    def consume(j, slot, diag=None, lo=0):
        off = pl.multiple_of(j * tk, tk)
        for a in range(2):
            s = s_ref[slot, a, :, lo:]
            if diag is not None:
                kpos = diag * tk + lax.broadcasted_iota(jnp.int32, (tk, tq - lo), 0)
                qpos = lo + lax.broadcasted_iota(jnp.int32, (tk, tq - lo), 1)
                s = jnp.where(kpos <= qpos, s, MASK_VALUE)
            m_old = m_ref[a, :, lo:]
            m_new = jnp.maximum(m_old, jnp.max(s, axis=0, keepdims=True))
            p = jnp.exp2(s - m_new)
            alpha = jnp.exp2(m_old - m_new)
            vt = jnp.concatenate([vt_ref[a * HEAD_DIM:(a + 1) * HEAD_DIM, pl.ds(off, tk)], ones_rows], axis=0)
            acc_ref[a, :, lo:] = alpha * acc_ref[a, :, lo:] + _dot(vt, p.astype(bf16))
            m_ref[a, :, lo:] = m_new

    @pl.when(first_block % 2 == 1)
    def _():
        scores(first_block, 1)
        consume(first_block, 1)

    scores(2 * first_pair, 0)

    def body(jj, carry):
        scores(2 * jj + 1, 1)
        consume(2 * jj, 0)
        scores(2 * jj + 2, 0)
        consume(2 * jj + 1, 1)
        return carry

    lax.fori_loop(first_pair, i, body, 0)
    scores(2 * i + 1, 1, lo=tk)
    consume(2 * i, 0, diag=0)
    consume(2 * i + 1, 1, diag=1, lo=tk)
    o_t = jnp.concatenate([acc_ref[a, :HEAD_DIM] / acc_ref[a, HEAD_DIM:HEAD_DIM + 1] for a in range(2)], axis=0)
    o_ref[...] = o_t.T.astype(bf16)


EXP2_UNDERFLOW = 160.0
NORM_SLACK = 1.02


def _fox(fq, fk_aug, fv_t, cend, norm2, *, batch, seq, tq):
    t = fq.shape[0]
    nq = seq // tq
    tk = tq // 2
    n_chunks = seq // CHUNK
    cend_flat = cend.reshape(batch, n_chunks, 8, LANES)[:, :, 0, :N_HEADS_FOX].transpose(0, 2, 1).reshape(-1)
    norm2 = norm2.reshape(batch, 8, LANES)[:, 0, :2 * N_HEADS_FOX].reshape(batch, 2, N_HEADS_FOX)
    thr = (2.0 * NORM_SLACK * jnp.sqrt(norm2[:, 0] * norm2[:, 1]) + EXP2_UNDERFLOW).reshape(-1)
    smem = pl.BlockSpec(memory_space=pltpu.SMEM)
    q_spec = pl.BlockSpec((tq, LANES), lambda b, p, i: (b * nq + i, p))
    return pl.pallas_call(
        functools.partial(_fox_kernel, tq=tq, tk=tk, n_chunks=n_chunks),
        grid=(batch, N_PAIRS, nq),
        in_specs=[smem, smem, q_spec,
                  pl.BlockSpec((seq, 2 * LANES), lambda b, p, i: (b, p)),
                  pl.BlockSpec((LANES, seq), lambda b, p, i: (p, b))],
        out_specs=q_spec,
        out_shape=jax.ShapeDtypeStruct((t, W_FOX), bf16),
        scratch_shapes=[pltpu.VMEM((2, 2, tk, tq), f32), pltpu.VMEM((2, 1, tq), f32),
                        pltpu.VMEM((2, HEAD_DIM + SUM_ROWS, tq), f32)],
        compiler_params=_params("parallel", "parallel", "arbitrary"),
        name="fox_attention",
    )(cend_flat, thr, fq, fk_aug, fv_t)


def _channel_kernel(a_ref, b_ref, c_ref, x_ref, p_ref, wo_ref, gmix_ref,
                    gpre_ref, wg_ref, wu_ref, cw_ref, cb_ref, wd_ref, gpost_ref,
                    gple_ref, wpg_ref, wpp_ref, gplepost_ref, o_ref, halo_ref,
                    *, tm, fc, blocks_per_seq):
    i = pl.program_id(0)

    @pl.when(i % blocks_per_seq == 0)
    def _():
        halo_ref[...] = jnp.zeros_like(halo_ref)

    cat = jnp.concatenate([a_ref[...], b_ref[...], c_ref[...]], axis=1)
    xf = x_ref[...] + _rms(_dot(cat, wo_ref[...]), gmix_ref[...])

    h = _rms(xf, gpre_ref[...]).astype(bf16)
    row8 = lax.broadcasted_iota(jnp.int32, (8, fc), 0)
    acc = jnp.zeros((tm, D_MODEL), f32)
    for c in range(D_FF // fc):
        cols = slice(c * fc, (c + 1) * fc)
        g = _dot(h, wg_ref[:, cols])
        up = _dot(h, wu_ref[:, cols])
        prev = halo_ref[:, cols]
        halo_ref[:, cols] = g[tm - 8:, :]
        g1 = pltpu.roll(g, 1, 0)
        g2 = pltpu.roll(g, 2, 0)
        g1 = jnp.concatenate([jnp.where(row8 < 1, pltpu.roll(prev, 1, 0), g1[:8]), g1[8:]], axis=0)
        g2 = jnp.concatenate([jnp.where(row8 < 2, pltpu.roll(prev, 2, 0), g2[:8]), g2[8:]], axis=0)
        conv = cb_ref[:, cols] + g2 * cw_ref[0:1, cols]
        conv = conv + g1 * cw_ref[1:2, cols]
        conv = conv + g * cw_ref[2:3, cols]
        act = _gelu(conv) * up
        acc = acc + _dot(act.astype(bf16), wd_ref[cols, :])
    xf = xf + _rms(acc, gpost_ref[...])

    gate = jax.nn.sigmoid(_dot(_rms(xf, gple_ref[...]).astype(bf16), wpg_ref[...]))
    e = _dot(p_ref[...].astype(bf16), wpp_ref[...])
    o_ref[...] = xf + _rms(e * gate, gplepost_ref[...])


def _channel(out_a, out_b, out_c, x2d, p2d, w_o, g_mix, g_pre, w_gate, w_up, conv_w, conv_b, w_down, g_post,
             g_ple, w_ple_gate, w_ple_proj, g_ple_post, *, layer, seq, tm, fc):
    t = x2d.shape[0]
    row = lambda width: pl.BlockSpec((tm, width), lambda i: (i, 0))
    gain = _resident((1, D_MODEL))
    kern = functools.partial(_channel_kernel, tm=tm, fc=fc, blocks_per_seq=seq // tm)
    return pl.pallas_call(
        kern,
        grid=(t // tm,),
        in_specs=[row(W_SGU), row(W_RET), row(W_FOX), row(D_MODEL),
                  pl.BlockSpec((tm, D_PLE), lambda i: (layer * (t // tm) + i, 0)),
                  _resident((D_MODEL, D_MODEL), layer), gain,
                  gain, _resident((D_MODEL, D_FF), layer), _resident((D_MODEL, D_FF), layer),
                  _resident(conv_w.shape), _resident((1, D_FF)), _resident((D_FF, D_MODEL), layer), gain,
                  gain, _resident((D_MODEL, D_MODEL), layer), _resident((D_PLE, D_MODEL), layer), gain],
        out_specs=row(D_MODEL),
        out_shape=jax.ShapeDtypeStruct((t, D_MODEL), f32),
        scratch_shapes=[pltpu.VMEM((8, D_FF), f32)],
        compiler_params=_params("arbitrary"),
        name="channel",
    )(out_a, out_b, out_c, x2d, p2d, w_o, g_mix, g_pre, w_gate, w_up, conv_w, conv_b, w_down, g_post,
      g_ple, w_ple_gate, w_ple_proj, g_ple_post)


def _largest_tile(seq, cap):
    tile = cap
    while seq % tile:
        tile //= 2
    return tile


def kernel(x, p, positions, mix_pre_g, w_in, sgu_v_g, sgu_w, sgu_b, fox_b_f, w_o, mix_post_g, ffn_pre_g,
           w_gate, w_up, conv_w, conv_b, w_down, ffn_post_g, ple_pre_g, w_ple_gate, w_ple_proj, ple_post_g):
    batch, seq, _ = x.shape
    depth = w_in.shape[0]
    t = batch * seq
    assert seq % CHUNK == 0
    tm = _largest_tile(seq, 512)
    tm_in = _largest_tile(seq, 1024)
    tq = _largest_tile(seq, 1024)
    fc = 2048

    cos4, sin4 = _rope_tables(positions)
    x2d = x.reshape(t, D_MODEL)
    p2d = p.reshape(depth * t, D_PLE)
    w_pad = jnp.pad(w_in.astype(bf16), ((0, 0), (0, 0), (0, N_IN_PAD - w_in.shape[2])))
    w_o, w_gate, w_up, w_down, w_ple_gate, w_ple_proj = (
        w.astype(bf16) for w in (w_o, w_gate, w_up, w_down, w_ple_gate, w_ple_proj))
    for i in range(depth):
        sgu_b_full = jnp.repeat(sgu_b[i].T, HEAD_DIM, axis=1)
        fox_b = jnp.pad(fox_b_f[i], (0, LANES - N_HEADS_FOX))[None, :]
        out_a, rq, rk, rv, rg, fq, fk_aug, fv_t, cend, norm2 = _proj_in(
            x2d, mix_pre_g[i][None, :], w_pad, cos4, sin4, sgu_w[i], sgu_b_full,
            sgu_v_g[i].reshape(1, W_SGU), fox_b, layer=i, seq=seq, tm=tm_in)
        out_b = _retention(rq, rk, rv, rg, seq=seq, tm=tm_in)
        out_c = _fox(fq, fk_aug, fv_t, cend, norm2, batch=batch, seq=seq, tq=tq)
        x2d = _channel(out_a, out_b, out_c, x2d, p2d, w_o, mix_post_g[i][None, :],
                       ffn_pre_g[i][None, :], w_gate, w_up, conv_w[i], conv_b[i][None, :], w_down,
                       ffn_post_g[i][None, :], ple_pre_g[i][None, :], w_ple_gate, w_ple_proj,
                       ple_post_g[i][None, :], layer=i, seq=seq, tm=tm, fc=fc)
    return x2d.reshape(batch, seq, D_MODEL)
```

```python
import functools

import numpy as np
import jax
import jax.numpy as jnp
from jax import lax
from jax.experimental import pallas as pl
from jax.experimental.pallas import tpu as pltpu

D_MODEL = 1024
D_PLE = 256
HEAD_DIM = 64
N_HEADS_SGU = 4
N_HEADS_RET = 6
N_HEADS_FOX = 6
W_SGU = N_HEADS_SGU * HEAD_DIM
W_RET = N_HEADS_RET * HEAD_DIM
W_FOX = N_HEADS_FOX * HEAD_DIM
CHUNK = 128
D_FF = 4 * D_MODEL
ROPE_BASE = 10000.0
RMS_EPS = 1e-6
LANES = 128
N_PAIRS = W_RET // LANES
N_IN_PAD = 2 * W_SGU + 4 * W_RET + 3 * W_FOX + LANES
MASK_VALUE = -1e30
LOG2E = 1.4426950408889634
C_TERMS = 3
CUMSUM_ROWS = 512
RET_CHUNK = 256
VMEM_LIMIT = 56 * 1024 * 1024

f32 = jnp.float32
bf16 = jnp.bfloat16


def _params(*sem):
    return pltpu.CompilerParams(dimension_semantics=sem, vmem_limit_bytes=VMEM_LIMIT)


def _resident(shape, layer=None):
    zeros = (0,) * len(shape)
    if layer is None:
        return pl.BlockSpec(shape, lambda *_: zeros, pipeline_mode=pl.Buffered(1))
    return pl.BlockSpec((None,) + tuple(shape), lambda *_: (layer,) + zeros, pipeline_mode=pl.Buffered(1))


def _rms(xf, g):
    y = xf * lax.rsqrt(jnp.mean(xf * xf, axis=-1, keepdims=True) + RMS_EPS)
    return y * g


def _dot(a, b):
    return jnp.dot(a, b, preferred_element_type=f32)


def _dot_nt(a, b):
    return lax.dot_general(a, b, (((1,), (1,)), ((), ())), preferred_element_type=f32)


def _dot_tn(a, b):
    return lax.dot_general(a, b, (((0,), (0,)), ((), ())), preferred_element_type=f32)


def _split_bf16(x, terms):
    parts = []
    r = x
    for _ in range(terms):
        p = r.astype(bf16)
        parts.append(p)
        r = r - p.astype(f32)
    return parts


def _group_mean(y, width):
    r = lax.broadcasted_iota(jnp.int32, (width, width), 0) // HEAD_DIM
    c = lax.broadcasted_iota(jnp.int32, (width, width), 1) // HEAD_DIM
    gm = jnp.where(r == c, 1.0 / HEAD_DIM, 0.0).astype(bf16)
    return _dot(y.astype(bf16), gm)


def _gelu(x):
    return jax.nn.gelu(x, approximate=True)


ROPE_HALF = HEAD_DIM // 2
ROPE_GROUPS = LANES // ROPE_HALF


def _rope_kernel(pos_ref, invf_ref, cos_ref, sin_ref):
    ang = pos_ref[...].astype(f32) * invf_ref[...]
    c = jnp.cos(ang)
    s = jnp.sin(ang)
    lane = lax.broadcasted_iota(jnp.int32, ang.shape, 1)
    first_group = lane < ROPE_HALF
    sign = jnp.where((lane % HEAD_DIM) < ROPE_HALF, -1.0, 1.0)

    def spread(v, j):
        v = pltpu.roll(v, LANES - ROPE_HALF * j, 1) if j else v
        v = jnp.where(first_group, v, 0.0)
        v = v + pltpu.roll(v, ROPE_HALF, 1)
        return v + pltpu.roll(v, 2 * ROPE_HALF, 1)

    for j in range(ROPE_GROUPS):
        cos_ref[j] = spread(c, j)
        sin_ref[j] = spread(s, j) * sign


def _rope_tables(positions):
    t = positions.size
    rows = t // ROPE_GROUPS
    pos = jnp.repeat(positions.reshape(ROPE_GROUPS, rows).T, ROPE_HALF, axis=1)
    inv_freq = ROPE_BASE ** (-jnp.arange(ROPE_HALF, dtype=f32) / ROPE_HALF)
    invf = jnp.tile(inv_freq, ROPE_GROUPS)[None, :]
    tr = min(rows, 1024)
    out_spec = pl.BlockSpec((ROPE_GROUPS, tr, LANES), lambda i: (0, i, 0))
    cos, sin = pl.pallas_call(
        _rope_kernel,
        grid=(rows // tr,),
        in_specs=[pl.BlockSpec((tr, LANES), lambda i: (i, 0)), _resident((1, LANES))],
        out_specs=[out_spec, out_spec],
        out_shape=[jax.ShapeDtypeStruct((ROPE_GROUPS, rows, LANES), f32)] * 2,
        compiler_params=_params("parallel"),
        name="rope_tables",
    )(pos, invf)
    return cos.reshape(t, LANES), sin.reshape(t, LANES)


def _proj_in_kernel(x_ref, g_ref, w_ref, cos_ref, sin_ref, sguw_ref, sgub_ref, vg_ref, fb_ref, place_ref,
                    oa_ref, rq_ref, rk_ref, rv_ref, rg_ref, fq_ref, fk_ref, fvt_ref, cend_ref, nrm_ref,
                    carry_ref, fv_ref, *, tm, blocks_per_seq):
    i = pl.program_id(0)
    n_chunks = tm // CHUNK
    h = _rms(x_ref[...], g_ref[...]).astype(bf16)

    def proj(lo, width):
        return _dot(h, w_ref[:, lo:lo + width])

    z_fox_vf = proj(2 * W_SGU + 4 * W_RET + 2 * W_FOX, W_FOX + LANES)

    ff = z_fox_vf[:, W_FOX:] + fb_ref[...]
    logf = -(jnp.maximum(-ff, 0.0) + jnp.log1p(jnp.exp(-jnp.abs(ff))))

    @pl.when(i % blocks_per_seq == 0)
    def _():
        carry_ref[...] = jnp.zeros_like(carry_ref)

    sub = min(tm, CUMSUM_ROWS)
    sub_tril = (lax.broadcasted_iota(jnp.int32, (sub, sub), 0) >= lax.broadcasted_iota(jnp.int32, (sub, sub), 1))
    sub_tril = jnp.where(sub_tril, 1.0, 0.0).astype(bf16)
    carry = carry_ref[...]
    cums = []
    for r in range(tm // sub):
        within = _dot(sub_tril, jnp.concatenate(_split_bf16(logf[r * sub:(r + 1) * sub], 3), axis=1))
        cums.append(within[:, :LANES] + within[:, LANES:2 * LANES] + within[:, 2 * LANES:] + carry)
        carry = cums[-1][sub - 1:sub, :]
    carry_ref[...] = carry
    cum = jnp.concatenate(cums, axis=0) * LOG2E
    for c in range(n_chunks):
        cend_ref[8 * c:8 * (c + 1), :] = jnp.broadcast_to(cum[(c + 1) * CHUNK - 1:(c + 1) * CHUNK, :], (8, LANES))
    placed = _dot(jnp.concatenate(_split_bf16(cum, C_TERMS), axis=1), place_ref[...]).astype(bf16)
    for p in range(N_PAIRS):
        fk_ref[:, (2 * p + 1) * LANES:(2 * p + 2) * LANES] = placed[:, p * LANES:(p + 1) * LANES]

    sq_row = lax.broadcasted_iota(jnp.int32, (CHUNK, CHUNK), 0)
    sq_col = lax.broadcasted_iota(jnp.int32, (CHUNK, CHUNK), 1)
    tril = sq_row >= sq_col
    first_head = sq_col < HEAD_DIM

    z_sgu = _gelu(proj(0, 2 * W_SGU))
    u = z_sgu[:, :W_SGU]
    v = z_sgu[:, W_SGU:]
    v = v * lax.rsqrt(_group_mean(v * v, W_SGU) + RMS_EPS) * vg_ref[...]
    for p in range(W_SGU // LANES):
        cols = slice(p * LANES, (p + 1) * LANES)
        w_pair = jnp.concatenate(
            [jnp.where(tril, sguw_ref[2 * p], 0.0), jnp.where(tril, sguw_ref[2 * p + 1], 0.0)],
            axis=0).astype(bf16)
        for c in range(n_chunks):
            rows = slice(c * CHUNK, (c + 1) * CHUNK)
            r = _dot(w_pair, v[rows, cols].astype(bf16))
            s = jnp.where(first_head, r[:CHUNK], r[CHUNK:]) + sgub_ref[:, cols]
            oa_ref[rows, cols] = (u[rows, cols] * s).astype(bf16)

    cosv = cos_ref[...]
    sinv = sin_ref[...]
    lane = lax.broadcasted_iota(jnp.int32, (tm, LANES), 1)
    first_half = (lane % HEAD_DIM) < (HEAD_DIM // 2)

    def rotary(z):
        outs = []
        for s in range(N_PAIRS):
            t = z[:, s * LANES:(s + 1) * LANES]
            partner = jnp.where(first_half, pltpu.roll(t, LANES - HEAD_DIM // 2, 1),
                                pltpu.roll(t, HEAD_DIM // 2, 1))
            outs.append(t * cosv + partner * sinv)
        return jnp.concatenate(outs, axis=1)

    base = 2 * W_SGU
    z_qk = proj(base, 2 * W_RET)
    rq_ref[...] = rotary(z_qk[:, :W_RET]).astype(bf16)
    rk_ref[...] = (rotary(z_qk[:, W_RET:]) * (HEAD_DIM ** -0.5)).astype(bf16)
    z_vg = proj(base + 2 * W_RET, 2 * W_RET).astype(bf16)
    rv_ref[...] = z_vg[:, :W_RET]
    rg_ref[...] = z_vg[:, W_RET:]

    base += 4 * W_RET
    z_fox_qk = proj(base, 2 * W_FOX)
    fq = (z_fox_qk[:, :W_FOX] * (LOG2E * HEAD_DIM ** -0.5)).astype(bf16)
    fq_ref[...] = fq
    fk = z_fox_qk[:, W_FOX:].astype(bf16)
    for p in range(N_PAIRS):
        fk_ref[:, 2 * p * LANES:(2 * p + 1) * LANES] = fk[:, p * LANES:(p + 1) * LANES]
    fv_ref[...] = z_fox_vf[:, :W_FOX]
    fvt_ref[...] = fv_ref[...].T.astype(bf16)

    gr = lax.broadcasted_iota(jnp.int32, (2 * W_FOX, LANES), 0) // HEAD_DIM
    gc = lax.broadcasted_iota(jnp.int32, (2 * W_FOX, LANES), 1)
    head_sum = jnp.where(gr == gc, 1.0, 0.0).astype(bf16)
    qk = jnp.concatenate([fq, fk], axis=1).astype(f32)
    norm2 = jnp.max(_dot((qk * qk).astype(bf16), head_sum), axis=0, keepdims=True)

    @pl.when(i % blocks_per_seq == 0)
    def _():
        nrm_ref[...] = jnp.zeros_like(nrm_ref)

    nrm_ref[...] = jnp.maximum(nrm_ref[...], norm2)


def _placement_matrix():
    place = np.zeros((C_TERMS * LANES, N_PAIRS * LANES), np.float32)
    for h in range(N_HEADS_FOX):
        for term in range(C_TERMS):
            place[term * LANES + h, (h // 2) * LANES + C_TERMS * (h % 2) + term] = 1.0
    return jnp.asarray(place, bf16)


def _proj_in(x2d, g, w_pad, cos4, sin4, sgu_w, sgu_b_full, sgu_vg, fox_b, *, layer, seq, tm):
    t = x2d.shape[0]
    row = lambda width: pl.BlockSpec((tm, width), lambda i: (i, 0))
    out_widths = [W_SGU] + [W_RET] * 4 + [W_FOX, 2 * W_FOX]
    place = _placement_matrix()
    kern = functools.partial(_proj_in_kernel, tm=tm, blocks_per_seq=seq // tm)
    return pl.pallas_call(
        kern,
        grid=(t // tm,),
        in_specs=[row(D_MODEL), _resident((1, D_MODEL)), _resident((D_MODEL, N_IN_PAD), layer),
                  row(LANES), row(LANES),
                  _resident((N_HEADS_SGU, CHUNK, CHUNK)), _resident((CHUNK, W_SGU)),
                  _resident((1, W_SGU)), _resident((1, LANES)), _resident(place.shape)],
        out_specs=[row(w) for w in out_widths] + [
            pl.BlockSpec((W_FOX, tm), lambda i: (0, i)),
            pl.BlockSpec((8 * tm // CHUNK, LANES), lambda i: (i, 0)),
            pl.BlockSpec((8, LANES), lambda i: (i // (seq // tm), 0))],
        out_shape=[jax.ShapeDtypeStruct((t, w), bf16) for w in out_widths] + [
            jax.ShapeDtypeStruct((W_FOX, t), bf16),
            jax.ShapeDtypeStruct((8 * t // CHUNK, LANES), f32),
            jax.ShapeDtypeStruct((8 * t // seq, LANES), f32)],
        scratch_shapes=[pltpu.VMEM((1, LANES), f32), pltpu.VMEM((tm, W_FOX), f32)],
        compiler_params=_params("arbitrary"),
        name="proj_in",
    )(x2d, g, w_pad, cos4, sin4, sgu_w, sgu_b_full, sgu_vg, fox_b, place)


def _retention_kernel(q_ref, k_ref, v_ref, g_ref, dec_ref, qd_ref, kd_ref, cd_ref, o_ref, state_ref, raw_ref,
                      *, tm, blocks_per_seq):
    i = pl.program_id(0)

    @pl.when(i % blocks_per_seq == 0)
    def _():
        state_ref[...] = jnp.zeros_like(state_ref)

    rc = RET_CHUNK
    first_head = lax.broadcasted_iota(jnp.int32, (rc, LANES), 1) < HEAD_DIM
    same_head = ((lax.broadcasted_iota(jnp.int32, (LANES, LANES), 0) < HEAD_DIM)
                 == (lax.broadcasted_iota(jnp.int32, (LANES, LANES), 1) < HEAD_DIM))
    zero = jnp.zeros((rc, LANES), bf16)
    states = [state_ref[p] for p in range(N_PAIRS)]
    for c in range(tm // rc):
        rows = slice(c * rc, (c + 1) * rc)
        for p in range(N_PAIRS):
            cols = slice(p * LANES, (p + 1) * LANES)
            q = q_ref[rows, cols]
            k = k_ref[rows, cols]
            v = v_ref[rows, cols]
            q_heads = jnp.concatenate([jnp.where(first_head, q, zero), jnp.where(first_head, zero, q)], axis=0)
            inner = _dot_nt(q_heads, k) * dec_ref[p]
            o_heads = _dot(inner.astype(bf16), v)
            o = (jnp.where(first_head, o_heads[:rc], o_heads[rc:])
                 + _dot(q, states[p].astype(bf16)) * qd_ref[p])
            k_dec = (k.astype(f32) * kd_ref[p]).astype(bf16)
            states[p] = states[p] * cd_ref[p] + jnp.where(same_head, _dot_tn(k_dec, v), 0.0)
            raw_ref[rows, cols] = o
    for p in range(N_PAIRS):
        state_ref[p] = states[p]
    o = raw_ref[...]
    ro = o * lax.rsqrt(_group_mean(o * o, W_RET) + RMS_EPS)
    gate = g_ref[...].astype(f32)
    o_ref[...] = (gate * jax.nn.sigmoid(gate) * ro).astype(bf16)


def _retention_tables():
    log_g = jnp.log(1.0 - 2.0 ** (-5.0 - jnp.arange(N_HEADS_RET, dtype=f32)))
    idx = jnp.arange(RET_CHUNK, dtype=f32)
    diff = idx[:, None] - idx[None, :]
    decay = jnp.where(diff >= 0, jnp.exp(log_g[:, None, None] * jnp.maximum(diff, 0.0)), 0.0)
    lane_g = jnp.repeat(log_g, HEAD_DIM).reshape(N_PAIRS, 1, LANES)
    q_decay = jnp.exp(lane_g * (idx + 1.0)[None, :, None])
    k_decay = jnp.exp(lane_g * (RET_CHUNK - 1.0 - idx)[None, :, None])
    chunk_decay = jnp.broadcast_to(jnp.exp(lane_g * RET_CHUNK).reshape(N_PAIRS, LANES, 1),
                                   (N_PAIRS, LANES, LANES))
    pair_decay = decay.reshape(N_PAIRS, 2 * RET_CHUNK, RET_CHUNK)
    return pair_decay, q_decay, k_decay, chunk_decay


def _retention(rq, rk, rv, rg, *, seq, tm):
    t = rq.shape[0]
    assert tm % RET_CHUNK == 0
    decay, q_decay, k_decay, chunk_decay = _retention_tables()
    row = pl.BlockSpec((tm, W_RET), lambda i: (i, 0))
    kern = functools.partial(_retention_kernel, tm=tm, blocks_per_seq=seq // tm)
    return pl.pallas_call(
        kern,
        grid=(t // tm,),
        in_specs=[row, row, row, row, _resident(decay.shape), _resident(q_decay.shape),
                  _resident(k_decay.shape), _resident(chunk_decay.shape)],
        out_specs=row,
        out_shape=jax.ShapeDtypeStruct((t, W_RET), bf16),
        scratch_shapes=[pltpu.VMEM((N_PAIRS, LANES, LANES), f32), pltpu.VMEM((tm, W_RET), f32)],
        compiler_params=_params("arbitrary"),
        name="retention",
    )(rq, rk, rv, rg, decay, q_decay, k_decay, chunk_decay)


SUM_ROWS = 16


def _fox_kernel(cend_ref, thr_ref, q_ref, k_ref, vt_ref, o_ref, s_ref, m_ref, acc_ref, *, tq, tk, n_chunks):
    b = pl.program_id(0)
    pair = pl.program_id(1)
    i = pl.program_id(2)

    def first_needed_block():
        heads = [b * N_HEADS_FOX + 2 * pair + a for a in range(2)]
        q_start = jnp.maximum((tq // CHUNK) * i - 1, 0)
        limit = [cend_ref[h * n_chunks + q_start] + thr_ref[h] for h in heads]

        def needed(j):
            key_end = (tk // CHUNK) * (jnp.maximum(j, 0) + 1) - 1
            flags = [cend_ref[h * n_chunks + key_end] <= lim for h, lim in zip(heads, limit)]
            return jnp.logical_and(j >= 0, jnp.logical_or(flags[0], flags[1]))

        return lax.while_loop(needed, lambda j: j - 1, 2 * i - 1) + 1

    first_block = first_needed_block()
    first_pair = (first_block + 1) // 2
    q = q_ref[...]
    lane = lax.broadcasted_iota(jnp.int32, (tq, LANES), 1)
    first_head = lane < HEAD_DIM
    zero = jnp.zeros((tq, LANES), bf16)
    q_aug = []
    for a in range(2):
        qa = jnp.where(first_head, q, zero) if a == 0 else jnp.where(first_head, zero, q)
        pick = jnp.where((lane >= C_TERMS * a) & (lane < C_TERMS * (a + 1)), -1.0, 0.0).astype(bf16)
        q_aug.append(jnp.concatenate([qa, pick], axis=1))
    m_ref[...] = jnp.full_like(m_ref, MASK_VALUE)
    acc_ref[...] = jnp.zeros_like(acc_ref)
    ones_rows = jnp.ones((SUM_ROWS, tk), bf16)

    def scores(j, slot, lo=0):
        off = pl.multiple_of(j * tk, tk)
        k = k_ref[pl.ds(off, tk), :]
        for a in range(2):
            s_ref[slot, a, :, lo:] = _dot_nt(k, q_aug[a][lo:])

    def consume(j, slot, diag=None, lo=0):
        off = pl.multiple_of(j * tk, tk)
        for a in range(2):
            s = s_ref[slot, a, :, lo:]
            if diag is not None:
                kpos = diag * tk + lax.broadcasted_iota(jnp.int32, (tk, tq - lo), 0)
                qpos = lo + lax.broadcasted_iota(jnp.int32, (tk, tq - lo), 1)
                s = jnp.where(kpos <= qpos, s, MASK_VALUE)
            m_old = m_ref[a, :, lo:]
            m_new = jnp.maximum(m_old, jnp.max(s, axis=0, keepdims=True))
            p = jnp.exp2(s - m_new)
            alpha = jnp.exp2(m_old - m_new)
            vt = jnp.concatenate([vt_ref[a * HEAD_DIM:(a + 1) * HEAD_DIM, pl.ds(off, tk)], ones_rows], axis=0)
            acc_ref[a, :, lo:] = alpha * acc_ref[a, :, lo:] + _dot(vt, p.astype(bf16))
            m_ref[a, :, lo:] = m_new

    @pl.when(first_block % 2 == 1)
    def _():
        scores(first_block, 1)
        scores(2 * first_pair, 0)
        consume(first_block, 1)

    @pl.when(first_block % 2 == 0)
    def _():
        scores(2 * first_pair, 0)

    def body(jj, carry):
        scores(2 * jj + 1, 1)
        consume(2 * jj, 0)
        scores(2 * jj + 2, 0)
        consume(2 * jj + 1, 1)
        return carry

    lax.fori_loop(first_pair, i, body, 0)
    scores(2 * i + 1, 1, lo=tk)
    consume(2 * i, 0, diag=0)
    consume(2 * i + 1, 1, diag=1, lo=tk)
    o_t = jnp.concatenate([acc_ref[a, :HEAD_DIM] / acc_ref[a, HEAD_DIM:HEAD_DIM + 1] for a in range(2)], axis=0)
    o_ref[...] = o_t.T.astype(bf16)


EXP2_UNDERFLOW = 160.0
NORM_SLACK = 1.02


def _fox(fq, fk_aug, fv_t, cend, norm2, *, batch, seq, tq):
    t = fq.shape[0]
    nq = seq // tq
    tk = tq // 2
    n_chunks = seq // CHUNK
    cend_flat = cend.reshape(batch, n_chunks, 8, LANES)[:, :, 0, :N_HEADS_FOX].transpose(0, 2, 1).reshape(-1)
    norm2 = norm2.reshape(batch, 8, LANES)[:, 0, :2 * N_HEADS_FOX].reshape(batch, 2, N_HEADS_FOX)
    thr = (2.0 * NORM_SLACK * jnp.sqrt(norm2[:, 0] * norm2[:, 1]) + EXP2_UNDERFLOW).reshape(-1)
    smem = pl.BlockSpec(memory_space=pltpu.SMEM)
    q_spec = pl.BlockSpec((tq, LANES), lambda b, p, i: (b * nq + i, p))
    return pl.pallas_call(
        functools.partial(_fox_kernel, tq=tq, tk=tk, n_chunks=n_chunks),
        grid=(batch, N_PAIRS, nq),
        in_specs=[smem, smem, q_spec,
                  pl.BlockSpec((seq, 2 * LANES), lambda b, p, i: (b, p)),
                  pl.BlockSpec((LANES, seq), lambda b, p, i: (p, b))],
        out_specs=q_spec,
        out_shape=jax.ShapeDtypeStruct((t, W_FOX), bf16),
        scratch_shapes=[pltpu.VMEM((2, 2, tk, tq), f32), pltpu.VMEM((2, 1, tq), f32),
                        pltpu.VMEM((2, HEAD_DIM + SUM_ROWS, tq), f32)],
        compiler_params=_params("parallel", "parallel", "arbitrary"),
        name="fox_attention",
    )(cend_flat, thr, fq, fk_aug, fv_t)


def _channel_kernel(a_ref, b_ref, c_ref, x_ref, p_ref, wo_ref, gmix_ref,
                    gpre_ref, wg_ref, wu_ref, cw_ref, cb_ref, wd_ref, gpost_ref,
                    gple_ref, wpg_ref, wpp_ref, gplepost_ref, o_ref, halo_ref,
                    *, tm, fc, blocks_per_seq):
    i = pl.program_id(0)

    @pl.when(i % blocks_per_seq == 0)
    def _():
        halo_ref[...] = jnp.zeros_like(halo_ref)

    cat = jnp.concatenate([a_ref[...], b_ref[...], c_ref[...]], axis=1)
    xf = x_ref[...] + _rms(_dot(cat, wo_ref[...]), gmix_ref[...])

    h = _rms(xf, gpre_ref[...]).astype(bf16)
    row8 = lax.broadcasted_iota(jnp.int32, (8, fc), 0)
    acc = jnp.zeros((tm, D_MODEL), f32)
    for c in range(D_FF // fc):
        cols = slice(c * fc, (c + 1) * fc)
        g = _dot(h, wg_ref[:, cols])
        up = _dot(h, wu_ref[:, cols])
        prev = halo_ref[:, cols]
        halo_ref[:, cols] = g[tm - 8:, :]
        g1 = pltpu.roll(g, 1, 0)
        g2 = pltpu.roll(g, 2, 0)
        g1 = jnp.concatenate([jnp.where(row8 < 1, pltpu.roll(prev, 1, 0), g1[:8]), g1[8:]], axis=0)
        g2 = jnp.concatenate([jnp.where(row8 < 2, pltpu.roll(prev, 2, 0), g2[:8]), g2[8:]], axis=0)
        conv = cb_ref[:, cols] + g2 * cw_ref[0:1, cols]
        conv = conv + g1 * cw_ref[1:2, cols]
        conv = conv + g * cw_ref[2:3, cols]
        act = _gelu(conv) * up
        acc = acc + _dot(act.astype(bf16), wd_ref[cols, :])
    xf = xf + _rms(acc, gpost_ref[...])

    gate = jax.nn.sigmoid(_dot(_rms(xf, gple_ref[...]).astype(bf16), wpg_ref[...]))
    e = _dot(p_ref[...].astype(bf16), wpp_ref[...])
    o_ref[...] = xf + _rms(e * gate, gplepost_ref[...])


def _channel(out_a, out_b, out_c, x2d, p2d, w_o, g_mix, g_pre, w_gate, w_up, conv_w, conv_b, w_down, g_post,
             g_ple, w_ple_gate, w_ple_proj, g_ple_post, *, layer, seq, tm, fc):
    t = x2d.shape[0]
    row = lambda width: pl.BlockSpec((tm, width), lambda i: (i, 0))
    gain = _resident((1, D_MODEL))
    kern = functools.partial(_channel_kernel, tm=tm, fc=fc, blocks_per_seq=seq // tm)
    return pl.pallas_call(
        kern,
        grid=(t // tm,),
        in_specs=[row(W_SGU), row(W_RET), row(W_FOX), row(D_MODEL),
                  pl.BlockSpec((tm, D_PLE), lambda i: (layer * (t // tm) + i, 0)),
                  _resident((D_MODEL, D_MODEL), layer), gain,
                  gain, _resident((D_MODEL, D_FF), layer), _resident((D_MODEL, D_FF), layer),
                  _resident(conv_w.shape), _resident((1, D_FF)), _resident((D_FF, D_MODEL), layer), gain,
                  gain, _resident((D_MODEL, D_MODEL), layer), _resident((D_PLE, D_MODEL), layer), gain],
        out_specs=row(D_MODEL),
        out_shape=jax.ShapeDtypeStruct((t, D_MODEL), f32),
        scratch_shapes=[pltpu.VMEM((8, D_FF), f32)],
        compiler_params=_params("arbitrary"),
        name="channel",
    )(out_a, out_b, out_c, x2d, p2d, w_o, g_mix, g_pre, w_gate, w_up, conv_w, conv_b, w_down, g_post,
      g_ple, w_ple_gate, w_ple_proj, g_ple_post)


def _largest_tile(seq, cap):
    tile = cap
    while seq % tile:
        tile //= 2
    return tile


def kernel(x, p, positions, mix_pre_g, w_in, sgu_v_g, sgu_w, sgu_b, fox_b_f, w_o, mix_post_g, ffn_pre_g,
           w_gate, w_up, conv_w, conv_b, w_down, ffn_post_g, ple_pre_g, w_ple_gate, w_ple_proj, ple_post_g):
    batch, seq, _ = x.shape
    depth = w_in.shape[0]
    t = batch * seq
    assert seq % CHUNK == 0
    tm = _largest_tile(seq, 512)
    tm_in = _largest_tile(seq, 1024)
    tq = _largest_tile(seq, 1024)
    fc = 2048

    cos4, sin4 = _rope_tables(positions)
    x2d = x.reshape(t, D_MODEL)
    p2d = p.reshape(depth * t, D_PLE)
    w_pad = jnp.pad(w_in.astype(bf16), ((0, 0), (0, 0), (0, N_IN_PAD - w_in.shape[2])))
    w_o, w_gate, w_up, w_down, w_ple_gate, w_ple_proj = (
        w.astype(bf16) for w in (w_o, w_gate, w_up, w_down, w_ple_gate, w_ple_proj))
    for i in range(depth):
        sgu_b_full = jnp.repeat(sgu_b[i].T, HEAD_DIM, axis=1)
        fox_b = jnp.pad(fox_b_f[i], (0, LANES - N_HEADS_FOX))[None, :]
        out_a, rq, rk, rv, rg, fq, fk_aug, fv_t, cend, norm2 = _proj_in(
            x2d, mix_pre_g[i][None, :], w_pad, cos4, sin4, sgu_w[i], sgu_b_full,
            sgu_v_g[i].reshape(1, W_SGU), fox_b, layer=i, seq=seq, tm=tm_in)
        out_b = _retention(rq, rk, rv, rg, seq=seq, tm=tm_in)
        out_c = _fox(fq, fk_aug, fv_t, cend, norm2, batch=batch, seq=seq, tq=tq)
        x2d = _channel(out_a, out_b, out_c, x2d, p2d, w_o, mix_post_g[i][None, :],
                       ffn_pre_g[i][None, :], w_gate, w_up, conv_w[i], conv_b[i][None, :], w_down,
                       ffn_post_g[i][None, :], ple_pre_g[i][None, :], w_ple_gate, w_ple_proj,
                       ple_post_g[i][None, :], layer=i, seq=seq, tm=tm, fc=fc)
    return x2d.reshape(batch, seq, D_MODEL)
```

```python
import functools

import numpy as np
import jax
import jax.numpy as jnp
from jax import lax
from jax.experimental import pallas as pl
from jax.experimental.pallas import tpu as pltpu

D_MODEL = 1024
D_PLE = 256
HEAD_DIM = 64
N_HEADS_SGU = 4
N_HEADS_RET = 6
N_HEADS_FOX = 6
W_SGU = N_HEADS_SGU * HEAD_DIM
W_RET = N_HEADS_RET * HEAD_DIM
W_FOX = N_HEADS_FOX * HEAD_DIM
CHUNK = 128
D_FF = 4 * D_MODEL
ROPE_BASE = 10000.0
RMS_EPS = 1e-6
LANES = 128
N_PAIRS = W_RET // LANES
N_IN_PAD = 2 * W_SGU + 4 * W_RET + 3 * W_FOX + LANES
MASK_VALUE = -1e30
LOG2E = 1.4426950408889634
C_TERMS = 3
CUMSUM_ROWS = 512
RET_CHUNK = 256
VMEM_LIMIT = 56 * 1024 * 1024

f32 = jnp.float32
bf16 = jnp.bfloat16


def _params(*sem):
    return pltpu.CompilerParams(dimension_semantics=sem, vmem_limit_bytes=VMEM_LIMIT)


def _resident(shape, layer=None):
    zeros = (0,) * len(shape)
    if layer is None:
        return pl.BlockSpec(shape, lambda *_: zeros, pipeline_mode=pl.Buffered(1))
    return pl.BlockSpec((None,) + tuple(shape), lambda *_: (layer,) + zeros, pipeline_mode=pl.Buffered(1))


def _rms(xf, g):
    y = xf * lax.rsqrt(jnp.mean(xf * xf, axis=-1, keepdims=True) + RMS_EPS)
    return y * g


def _dot(a, b):
    return jnp.dot(a, b, preferred_element_type=f32)


def _dot_nt(a, b):
    return lax.dot_general(a, b, (((1,), (1,)), ((), ())), preferred_element_type=f32)


def _dot_tn(a, b):
    return lax.dot_general(a, b, (((0,), (0,)), ((), ())), preferred_element_type=f32)


def _split_bf16(x, terms):
    parts = []
    r = x
    for _ in range(terms):
        p = r.astype(bf16)
        parts.append(p)
        r = r - p.astype(f32)
    return parts


def _group_mean(y, width):
    r = lax.broadcasted_iota(jnp.int32, (width, width), 0) // HEAD_DIM
    c = lax.broadcasted_iota(jnp.int32, (width, width), 1) // HEAD_DIM
    gm = jnp.where(r == c, 1.0 / HEAD_DIM, 0.0).astype(bf16)
    return _dot(y.astype(bf16), gm)


def _gelu(x):
    return jax.nn.gelu(x, approximate=True)


ROPE_HALF = HEAD_DIM // 2
ROPE_GROUPS = LANES // ROPE_HALF


def _rope_kernel(pos_ref, invf_ref, cos_ref, sin_ref):
    ang = pos_ref[...].astype(f32) * invf_ref[...]
    c = jnp.cos(ang)
    s = jnp.sin(ang)
    lane = lax.broadcasted_iota(jnp.int32, ang.shape, 1)
    first_group = lane < ROPE_HALF
    sign = jnp.where((lane % HEAD_DIM) < ROPE_HALF, -1.0, 1.0)

    def spread(v, j):
        v = pltpu.roll(v, LANES - ROPE_HALF * j, 1) if j else v
        v = jnp.where(first_group, v, 0.0)
        v = v + pltpu.roll(v, ROPE_HALF, 1)
        return v + pltpu.roll(v, 2 * ROPE_HALF, 1)

    for j in range(ROPE_GROUPS):
        cos_ref[j] = spread(c, j)
        sin_ref[j] = spread(s, j) * sign


def _rope_tables(positions):
    t = positions.size
    rows = t // ROPE_GROUPS
    pos = jnp.repeat(positions.reshape(ROPE_GROUPS, rows).T, ROPE_HALF, axis=1)
    inv_freq = ROPE_BASE ** (-jnp.arange(ROPE_HALF, dtype=f32) / ROPE_HALF)
    invf = jnp.tile(inv_freq, ROPE_GROUPS)[None, :]
    tr = min(rows, 1024)
    out_spec = pl.BlockSpec((ROPE_GROUPS, tr, LANES), lambda i: (0, i, 0))
    cos, sin = pl.pallas_call(
        _rope_kernel,
        grid=(rows // tr,),
        in_specs=[pl.BlockSpec((tr, LANES), lambda i: (i, 0)), _resident((1, LANES))],
        out_specs=[out_spec, out_spec],
        out_shape=[jax.ShapeDtypeStruct((ROPE_GROUPS, rows, LANES), f32)] * 2,
        compiler_params=_params("parallel"),
        name="rope_tables",
    )(pos, invf)
    return cos.reshape(t, LANES), sin.reshape(t, LANES)


def _proj_in_kernel(x_ref, g_ref, w_ref, cos_ref, sin_ref, sguw_ref, sgub_ref, vg_ref, fb_ref, place_ref,
                    oa_ref, rq_ref, rk_ref, rv_ref, rg_ref, fq_ref, fk_ref, fvt_ref, cend_ref, nrm_ref,
                    carry_ref, fv_ref, *, tm, blocks_per_seq):
    i = pl.program_id(0)
    n_chunks = tm // CHUNK
    h = _rms(x_ref[...], g_ref[...]).astype(bf16)

    def proj(lo, width):
        return _dot(h, w_ref[:, lo:lo + width])

    z_fox_vf = proj(2 * W_SGU + 4 * W_RET + 2 * W_FOX, W_FOX + LANES)

    ff = z_fox_vf[:, W_FOX:] + fb_ref[...]
    logf = -(jnp.maximum(-ff, 0.0) + jnp.log1p(jnp.exp(-jnp.abs(ff))))

    @pl.when(i % blocks_per_seq == 0)
    def _():
        carry_ref[...] = jnp.zeros_like(carry_ref)

    sub = min(tm, CUMSUM_ROWS)
    sub_tril = (lax.broadcasted_iota(jnp.int32, (sub, sub), 0) >= lax.broadcasted_iota(jnp.int32, (sub, sub), 1))
    sub_tril = jnp.where(sub_tril, 1.0, 0.0).astype(bf16)
    carry = carry_ref[...]
    cums = []
    for r in range(tm // sub):
        within = _dot(sub_tril, jnp.concatenate(_split_bf16(logf[r * sub:(r + 1) * sub], 3), axis=1))
        cums.append(within[:, :LANES] + within[:, LANES:2 * LANES] + within[:, 2 * LANES:] + carry)
        carry = cums[-1][sub - 1:sub, :]
    carry_ref[...] = carry
    cum = jnp.concatenate(cums, axis=0) * LOG2E
    for c in range(n_chunks):
        cend_ref[8 * c:8 * (c + 1), :] = jnp.broadcast_to(cum[(c + 1) * CHUNK - 1:(c + 1) * CHUNK, :], (8, LANES))
    placed = _dot(jnp.concatenate(_split_bf16(cum, C_TERMS), axis=1), place_ref[...]).astype(bf16)
    for p in range(N_PAIRS):
        fk_ref[:, (2 * p + 1) * LANES:(2 * p + 2) * LANES] = placed[:, p * LANES:(p + 1) * LANES]

    sq_row = lax.broadcasted_iota(jnp.int32, (CHUNK, CHUNK), 0)
    sq_col = lax.broadcasted_iota(jnp.int32, (CHUNK, CHUNK), 1)
    tril = sq_row >= sq_col
    first_head = sq_col < HEAD_DIM

    z_sgu = _gelu(proj(0, 2 * W_SGU))
    u = z_sgu[:, :W_SGU]
    v = z_sgu[:, W_SGU:]
    v = v * lax.rsqrt(_group_mean(v * v, W_SGU) + RMS_EPS) * vg_ref[...]
    for p in range(W_SGU // LANES):
        cols = slice(p * LANES, (p + 1) * LANES)
        w_pair = jnp.concatenate(
            [jnp.where(tril, sguw_ref[2 * p], 0.0), jnp.where(tril, sguw_ref[2 * p + 1], 0.0)],
            axis=0).astype(bf16)
        for c in range(n_chunks):
            rows = slice(c * CHUNK, (c + 1) * CHUNK)
            r = _dot(w_pair, v[rows, cols].astype(bf16))
            s = jnp.where(first_head, r[:CHUNK], r[CHUNK:]) + sgub_ref[:, cols]
            oa_ref[rows, cols] = (u[rows, cols] * s).astype(bf16)

    cosv = cos_ref[...]
    sinv = sin_ref[...]
    lane = lax.broadcasted_iota(jnp.int32, (tm, LANES), 1)
    first_half = (lane % HEAD_DIM) < (HEAD_DIM // 2)

    def rotary(z):
        outs = []
        for s in range(N_PAIRS):
            t = z[:, s * LANES:(s + 1) * LANES]
            partner = jnp.where(first_half, pltpu.roll(t, LANES - HEAD_DIM // 2, 1),
                                pltpu.roll(t, HEAD_DIM // 2, 1))
            outs.append(t * cosv + partner * sinv)
        return jnp.concatenate(outs, axis=1)

    base = 2 * W_SGU
    z_qk = proj(base, 2 * W_RET)
    rq_ref[...] = rotary(z_qk[:, :W_RET]).astype(bf16)
    rk_ref[...] = (rotary(z_qk[:, W_RET:]) * (HEAD_DIM ** -0.5)).astype(bf16)
    z_vg = proj(base + 2 * W_RET, 2 * W_RET).astype(bf16)
    rv_ref[...] = z_vg[:, :W_RET]
    rg_ref[...] = z_vg[:, W_RET:]

    base += 4 * W_RET
    z_fox_qk = proj(base, 2 * W_FOX)
    fq = (z_fox_qk[:, :W_FOX] * (LOG2E * HEAD_DIM ** -0.5)).astype(bf16)
    fq_ref[...] = fq
    fk = z_fox_qk[:, W_FOX:].astype(bf16)
    for p in range(N_PAIRS):
        fk_ref[:, 2 * p * LANES:(2 * p + 1) * LANES] = fk[:, p * LANES:(p + 1) * LANES]
    fv_ref[...] = z_fox_vf[:, :W_FOX]
    fvt_ref[...] = fv_ref[...].T.astype(bf16)

    gr = lax.broadcasted_iota(jnp.int32, (2 * W_FOX, LANES), 0) // HEAD_DIM
    gc = lax.broadcasted_iota(jnp.int32, (2 * W_FOX, LANES), 1)
    head_sum = jnp.where(gr == gc, 1.0, 0.0).astype(bf16)
    qk = jnp.concatenate([fq, fk], axis=1).astype(f32)
    norm2 = jnp.max(_dot((qk * qk).astype(bf16), head_sum), axis=0, keepdims=True)

    @pl.when(i % blocks_per_seq == 0)
    def _():
        nrm_ref[...] = jnp.zeros_like(nrm_ref)

    nrm_ref[...] = jnp.maximum(nrm_ref[...], norm2)


def _placement_matrix():
    place = np.zeros((C_TERMS * LANES, N_PAIRS * LANES), np.float32)
    for h in range(N_HEADS_FOX):
        for term in range(C_TERMS):
            place[term * LANES + h, (h // 2) * LANES + C_TERMS * (h % 2) + term] = 1.0
    return jnp.asarray(place, bf16)


def _proj_in(x2d, g, w_pad, cos4, sin4, sgu_w, sgu_b_full, sgu_vg, fox_b, *, layer, seq, tm):
    t = x2d.shape[0]
    row = lambda width: pl.BlockSpec((tm, width), lambda i: (i, 0))
    out_widths = [W_SGU] + [W_RET] * 4 + [W_FOX, 2 * W_FOX]
    place = _placement_matrix()
    kern = functools.partial(_proj_in_kernel, tm=tm, blocks_per_seq=seq // tm)
    return pl.pallas_call(
        kern,
        grid=(t // tm,),
        in_specs=[row(D_MODEL), _resident((1, D_MODEL)), _resident((D_MODEL, N_IN_PAD), layer),
                  row(LANES), row(LANES),
                  _resident((N_HEADS_SGU, CHUNK, CHUNK)), _resident((CHUNK, W_SGU)),
                  _resident((1, W_SGU)), _resident((1, LANES)), _resident(place.shape)],
        out_specs=[row(w) for w in out_widths] + [
            pl.BlockSpec((W_FOX, tm), lambda i: (0, i)),
            pl.BlockSpec((8 * tm // CHUNK, LANES), lambda i: (i, 0)),
            pl.BlockSpec((8, LANES), lambda i: (i // (seq // tm), 0))],
        out_shape=[jax.ShapeDtypeStruct((t, w), bf16) for w in out_widths] + [
            jax.ShapeDtypeStruct((W_FOX, t), bf16),
            jax.ShapeDtypeStruct((8 * t // CHUNK, LANES), f32),
            jax.ShapeDtypeStruct((8 * t // seq, LANES), f32)],
        scratch_shapes=[pltpu.VMEM((1, LANES), f32), pltpu.VMEM((tm, W_FOX), f32)],
        compiler_params=_params("arbitrary"),
        name="proj_in",
    )(x2d, g, w_pad, cos4, sin4, sgu_w, sgu_b_full, sgu_vg, fox_b, place)


def _retention_kernel(q_ref, k_ref, v_ref, g_ref, dec_ref, qd_ref, kd_ref, cd_ref, o_ref, state_ref, raw_ref,
                      *, tm, blocks_per_seq):
    i = pl.program_id(0)

    @pl.when(i % blocks_per_seq == 0)
    def _():
        state_ref[...] = jnp.zeros_like(state_ref)

    rc = RET_CHUNK
    first_head = lax.broadcasted_iota(jnp.int32, (rc, LANES), 1) < HEAD_DIM
    same_head = ((lax.broadcasted_iota(jnp.int32, (LANES, LANES), 0) < HEAD_DIM)
                 == (lax.broadcasted_iota(jnp.int32, (LANES, LANES), 1) < HEAD_DIM))
    zero = jnp.zeros((rc, LANES), bf16)
    states = [state_ref[p] for p in range(N_PAIRS)]
    for c in range(tm // rc):
        rows = slice(c * rc, (c + 1) * rc)
        for p in range(N_PAIRS):
            cols = slice(p * LANES, (p + 1) * LANES)
            q = q_ref[rows, cols]
            k = k_ref[rows, cols]
            v = v_ref[rows, cols]
            q_heads = jnp.concatenate([jnp.where(first_head, q, zero), jnp.where(first_head, zero, q)], axis=0)
            inner = _dot_nt(q_heads, k) * dec_ref[p]
            o_heads = _dot(inner.astype(bf16), v)
            o = (jnp.where(first_head, o_heads[:rc], o_heads[rc:])
                 + _dot(q, states[p].astype(bf16)) * qd_ref[p])
            k_dec = (k.astype(f32) * kd_ref[p]).astype(bf16)
            states[p] = states[p] * cd_ref[p] + jnp.where(same_head, _dot_tn(k_dec, v), 0.0)
            raw_ref[rows, cols] = o
    for p in range(N_PAIRS):
        state_ref[p] = states[p]
    o = raw_ref[...]
    ro = o * lax.rsqrt(_group_mean(o * o, W_RET) + RMS_EPS)
    gate = g_ref[...].astype(f32)
    o_ref[...] = (gate * jax.nn.sigmoid(gate) * ro).astype(bf16)


def _retention_tables():
    log_g = jnp.log(1.0 - 2.0 ** (-5.0 - jnp.arange(N_HEADS_RET, dtype=f32)))
    idx = jnp.arange(RET_CHUNK, dtype=f32)
    diff = idx[:, None] - idx[None, :]
    decay = jnp.where(diff >= 0, jnp.exp(log_g[:, None, None] * jnp.maximum(diff, 0.0)), 0.0)
    lane_g = jnp.repeat(log_g, HEAD_DIM).reshape(N_PAIRS, 1, LANES)
    q_decay = jnp.exp(lane_g * (idx + 1.0)[None, :, None])
    k_decay = jnp.exp(lane_g * (RET_CHUNK - 1.0 - idx)[None, :, None])
    chunk_decay = jnp.broadcast_to(jnp.exp(lane_g * RET_CHUNK).reshape(N_PAIRS, LANES, 1),
                                   (N_PAIRS, LANES, LANES))
    pair_decay = decay.reshape(N_PAIRS, 2 * RET_CHUNK, RET_CHUNK)
    return pair_decay, q_decay, k_decay, chunk_decay


def _retention(rq, rk, rv, rg, *, seq, tm):
    t = rq.shape[0]
    assert tm % RET_CHUNK == 0
    decay, q_decay, k_decay, chunk_decay = _retention_tables()
    row = pl.BlockSpec((tm, W_RET), lambda i: (i, 0))
    kern = functools.partial(_retention_kernel, tm=tm, blocks_per_seq=seq // tm)
    return pl.pallas_call(
        kern,
        grid=(t // tm,),
        in_specs=[row, row, row, row, _resident(decay.shape), _resident(q_decay.shape),
                  _resident(k_decay.shape), _resident(chunk_decay.shape)],
        out_specs=row,
        out_shape=jax.ShapeDtypeStruct((t, W_RET), bf16),
        scratch_shapes=[pltpu.VMEM((N_PAIRS, LANES, LANES), f32), pltpu.VMEM((tm, W_RET), f32)],
        compiler_params=_params("arbitrary"),
        name="retention",
    )(rq, rk, rv, rg, decay, q_decay, k_decay, chunk_decay)


SUM_ROWS = 16


def _fox_kernel(cend_ref, thr_ref, q_ref, k_ref, vt_ref, o_ref, s_ref, m_ref, acc_ref, *, tq, tk, n_chunks):
    b = pl.program_id(0)
    pair = pl.program_id(1)
    i = pl.program_id(2)

    def first_needed_block():
        heads = [b * N_HEADS_FOX + 2 * pair + a for a in range(2)]
        q_start = jnp.maximum((tq // CHUNK) * i - 1, 0)
        limit = [cend_ref[h * n_chunks + q_start] + thr_ref[h] for h in heads]

        def needed(j):
            key_end = (tk // CHUNK) * (jnp.maximum(j, 0) + 1) - 1
            flags = [cend_ref[h * n_chunks + key_end] <= lim for h, lim in zip(heads, limit)]
            return jnp.logical_and(j >= 0, jnp.logical_or(flags[0], flags[1]))

        return lax.while_loop(needed, lambda j: j - 1, 2 * i - 1) + 1

    first_block = first_needed_block()
    first_pair = (first_block + 1) // 2
    q = q_ref[...]
    lane = lax.broadcasted_iota(jnp.int32, (tq, LANES), 1)
    first_head = lane < HEAD_DIM
    zero = jnp.zeros((tq, LANES), bf16)
    q_aug = []
    for a in range(2):
        qa = jnp.where(first_head, q, zero) if a == 0 else jnp.where(first_head, zero, q)
        pick = jnp.where((lane >= C_TERMS * a) & (lane < C_TERMS * (a + 1)), -1.0, 0.0).astype(bf16)
        q_aug.append(jnp.concatenate([qa, pick], axis=1))
    m_ref[...] = jnp.full_like(m_ref, MASK_VALUE)
    acc_ref[...] = jnp.zeros_like(acc_ref)
    ones_rows = jnp.ones((SUM_ROWS, tk), bf16)

    def scores(j, slot, lo=0):
        off = pl.multiple_of(j * tk, tk)
        k = k_ref[pl.ds(off, tk), :]
        for a in range(2):
            s_ref[slot, a, :, lo:] = _dot_nt(k, q_aug[a][lo:])

    def consume(j, slot, diag=None, lo=0):
        off = pl.multiple_of(j * tk, tk)
        for a in range(2):
            s = s_ref[slot, a, :, lo:]
            if diag is not None:
                kpos = diag * tk + lax.broadcasted_iota(jnp.int32, (tk, tq - lo), 0)
                qpos = lo + lax.broadcasted_iota(jnp.int32, (tk, tq - lo), 1)
                s = jnp.where(kpos <= qpos, s, MASK_VALUE)
            m_old = m_ref[a, :, lo:]
            m_new = jnp.maximum(m_old, jnp.max(s, axis=0, keepdims=True))
            p = jnp.exp2(s - m_new)
            alpha = jnp.exp2(m_old - m_new)
            vt = jnp.concatenate([vt_ref[a * HEAD_DIM:(a + 1) * HEAD_DIM, pl.ds(off, tk)], ones_rows], axis=0)
            acc_ref[a, :, lo:] = alpha * acc_ref[a, :, lo:] + _dot(vt, p.astype(bf16))
            m_ref[a, :, lo:] = m_new

    @pl.when(first_block % 2 == 1)
    def _():
        scores(first_block, 1)
        scores(2 * first_pair, 0)
        consume(first_block, 1)

    @pl.when(first_block % 2 == 0)
    def _():
        scores(2 * first_pair, 0)

    def body(jj, carry):
        scores(2 * jj + 1, 1)
        consume(2 * jj, 0)
        scores(2 * jj + 2, 0)
        consume(2 * jj + 1, 1)
        return carry

    lax.fori_loop(first_pair, i, body, 0)
    scores(2 * i + 1, 1, lo=tk)
    consume(2 * i, 0, diag=0)
    consume(2 * i + 1, 1, diag=1, lo=tk)
    o_t = jnp.concatenate([acc_ref[a, :HEAD_DIM] / acc_ref[a, HEAD_DIM:HEAD_DIM + 1] for a in range(2)], axis=0)
    o_ref[...] = o_t.T.astype(bf16)


EXP2_UNDERFLOW = 153.0
NORM_SLACK = 1.01


def _fox(fq, fk_aug, fv_t, cend, norm2, *, batch, seq, tq):
    t = fq.shape[0]
    nq = seq // tq
    tk = tq // 2
    n_chunks = seq // CHUNK
    cend_flat = cend.reshape(batch, n_chunks, 8, LANES)[:, :, 0, :N_HEADS_FOX].transpose(0, 2, 1).reshape(-1)
    norm2 = norm2.reshape(batch, 8, LANES)[:, 0, :2 * N_HEADS_FOX].reshape(batch, 2, N_HEADS_FOX)
    thr = (2.0 * NORM_SLACK * jnp.sqrt(norm2[:, 0] * norm2[:, 1]) + EXP2_UNDERFLOW).reshape(-1)
    smem = pl.BlockSpec(memory_space=pltpu.SMEM)
    q_spec = pl.BlockSpec((tq, LANES), lambda b, p, i: (b * nq + i, p))
    return pl.pallas_call(
        functools.partial(_fox_kernel, tq=tq, tk=tk, n_chunks=n_chunks),
        grid=(batch, N_PAIRS, nq),
        in_specs=[smem, smem, q_spec,
                  pl.BlockSpec((seq, 2 * LANES), lambda b, p, i: (b, p)),
                  pl.BlockSpec((LANES, seq), lambda b, p, i: (p, b))],
        out_specs=q_spec,
        out_shape=jax.ShapeDtypeStruct((t, W_FOX), bf16),
        scratch_shapes=[pltpu.VMEM((2, 2, tk, tq), f32), pltpu.VMEM((2, 1, tq), f32),
                        pltpu.VMEM((2, HEAD_DIM + SUM_ROWS, tq), f32)],
        compiler_params=_params("parallel", "parallel", "arbitrary"),
        name="fox_attention",
    )(cend_flat, thr, fq, fk_aug, fv_t)


def _channel_kernel(a_ref, b_ref, c_ref, x_ref, p_ref, wo_ref, gmix_ref,
                    gpre_ref, wg_ref, wu_ref, cw_ref, cb_ref, wd_ref, gpost_ref,
                    gple_ref, wpg_ref, wpp_ref, gplepost_ref, o_ref, halo_ref,
                    *, tm, fc, blocks_per_seq):
    i = pl.program_id(0)

    @pl.when(i % blocks_per_seq == 0)
    def _():
        halo_ref[...] = jnp.zeros_like(halo_ref)

    cat = jnp.concatenate([a_ref[...], b_ref[...], c_ref[...]], axis=1)
    xf = x_ref[...] + _rms(_dot(cat, wo_ref[...]), gmix_ref[...])

    h = _rms(xf, gpre_ref[...]).astype(bf16)
    row8 = lax.broadcasted_iota(jnp.int32, (8, fc), 0)
    acc = jnp.zeros((tm, D_MODEL), f32)
    for c in range(D_FF // fc):
        cols = slice(c * fc, (c + 1) * fc)
        g = _dot(h, wg_ref[:, cols])
        up = _dot(h, wu_ref[:, cols])
        prev = halo_ref[:, cols]
        halo_ref[:, cols] = g[tm - 8:, :]
        g1 = pltpu.roll(g, 1, 0)
        g2 = pltpu.roll(g, 2, 0)
        g1 = jnp.concatenate([jnp.where(row8 < 1, pltpu.roll(prev, 1, 0), g1[:8]), g1[8:]], axis=0)
        g2 = jnp.concatenate([jnp.where(row8 < 2, pltpu.roll(prev, 2, 0), g2[:8]), g2[8:]], axis=0)
        conv = cb_ref[:, cols] + g2 * cw_ref[0:1, cols]
        conv = conv + g1 * cw_ref[1:2, cols]
        conv = conv + g * cw_ref[2:3, cols]
        act = _gelu(conv) * up
        acc = acc + _dot(act.astype(bf16), wd_ref[cols, :])
    xf = xf + _rms(acc, gpost_ref[...])

    gate = jax.nn.sigmoid(_dot(_rms(xf, gple_ref[...]).astype(bf16), wpg_ref[...]))
    e = _dot(p_ref[...].astype(bf16), wpp_ref[...])
    o_ref[...] = xf + _rms(e * gate, gplepost_ref[...])


def _channel(out_a, out_b, out_c, x2d, p2d, w_o, g_mix, g_pre, w_gate, w_up, conv_w, conv_b, w_down, g_post,
             g_ple, w_ple_gate, w_ple_proj, g_ple_post, *, layer, seq, tm, fc):
    t = x2d.shape[0]
    row = lambda width: pl.BlockSpec((tm, width), lambda i: (i, 0))
    gain = _resident((1, D_MODEL))
    kern = functools.partial(_channel_kernel, tm=tm, fc=fc, blocks_per_seq=seq // tm)
    return pl.pallas_call(
        kern,
        grid=(t // tm,),
        in_specs=[row(W_SGU), row(W_RET), row(W_FOX), row(D_MODEL),
                  pl.BlockSpec((tm, D_PLE), lambda i: (layer * (t // tm) + i, 0)),
                  _resident((D_MODEL, D_MODEL), layer), gain,
                  gain, _resident((D_MODEL, D_FF), layer), _resident((D_MODEL, D_FF), layer),
                  _resident(conv_w.shape), _resident((1, D_FF)), _resident((D_FF, D_MODEL), layer), gain,
                  gain, _resident((D_MODEL, D_MODEL), layer), _resident((D_PLE, D_MODEL), layer), gain],
        out_specs=row(D_MODEL),
        out_shape=jax.ShapeDtypeStruct((t, D_MODEL), f32),
        scratch_shapes=[pltpu.VMEM((8, D_FF), f32)],
        compiler_params=_params("arbitrary"),
        name="channel",
    )(out_a, out_b, out_c, x2d, p2d, w_o, g_mix, g_pre, w_gate, w_up, conv_w, conv_b, w_down, g_post,
      g_ple, w_ple_gate, w_ple_proj, g_ple_post)


def _largest_tile(seq, cap):
    tile = cap
    while seq % tile:
        tile //= 2
    return tile


def kernel(x, p, positions, mix_pre_g, w_in, sgu_v_g, sgu_w, sgu_b, fox_b_f, w_o, mix_post_g, ffn_pre_g,
           w_gate, w_up, conv_w, conv_b, w_down, ffn_post_g, ple_pre_g, w_ple_gate, w_ple_proj, ple_post_g):
    batch, seq, _ = x.shape
    depth = w_in.shape[0]
    t = batch * seq
    assert seq % CHUNK == 0
    tm = _largest_tile(seq, 512)
    tm_in = _largest_tile(seq, 1024)
    tq = _largest_tile(seq, 1024)
    fc = 2048

    cos4, sin4 = _rope_tables(positions)
    x2d = x.reshape(t, D_MODEL)
    p2d = p.reshape(depth * t, D_PLE)
    w_pad = jnp.pad(w_in.astype(bf16), ((0, 0), (0, 0), (0, N_IN_PAD - w_in.shape[2])))
    w_o, w_gate, w_up, w_down, w_ple_gate, w_ple_proj = (
        w.astype(bf16) for w in (w_o, w_gate, w_up, w_down, w_ple_gate, w_ple_proj))
    for i in range(depth):
        sgu_b_full = jnp.repeat(sgu_b[i].T, HEAD_DIM, axis=1)
        fox_b = jnp.pad(fox_b_f[i], (0, LANES - N_HEADS_FOX))[None, :]
        out_a, rq, rk, rv, rg, fq, fk_aug, fv_t, cend, norm2 = _proj_in(
            x2d, mix_pre_g[i][None, :], w_pad, cos4, sin4, sgu_w[i], sgu_b_full,
            sgu_v_g[i].reshape(1, W_SGU), fox_b, layer=i, seq=seq, tm=tm_in)
        out_b = _retention(rq, rk, rv, rg, seq=seq, tm=tm_in)
        out_c = _fox(fq, fk_aug, fv_t, cend, norm2, batch=batch, seq=seq, tq=tq)
        x2d = _channel(out_a, out_b, out_c, x2d, p2d, w_o, mix_post_g[i][None, :],
                       ffn_pre_g[i][None, :], w_gate, w_up, conv_w[i], conv_b[i][None, :], w_down,
                       ffn_post_g[i][None, :], ple_pre_g[i][None, :], w_ple_gate, w_ple_proj,
                       ple_post_g[i][None, :], layer=i, seq=seq, tm=tm, fc=fc)
    return x2d.reshape(batch, seq, D_MODEL)
```

```python
import functools

import numpy as np
import jax
import jax.numpy as jnp
from jax import lax
from jax.experimental import pallas as pl
from jax.experimental.pallas import tpu as pltpu

D_MODEL = 1024
D_PLE = 256
HEAD_DIM = 64
N_HEADS_SGU = 4
N_HEADS_RET = 6
N_HEADS_FOX = 6
W_SGU = N_HEADS_SGU * HEAD_DIM
W_RET = N_HEADS_RET * HEAD_DIM
W_FOX = N_HEADS_FOX * HEAD_DIM
CHUNK = 128
D_FF = 4 * D_MODEL
ROPE_BASE = 10000.0
RMS_EPS = 1e-6
LANES = 128
N_PAIRS = W_RET // LANES
N_IN_PAD = 2 * W_SGU + 4 * W_RET + 3 * W_FOX + LANES
MASK_VALUE = -1e30
LOG2E = 1.4426950408889634
C_TERMS = 3
CUMSUM_ROWS = 128
RET_CHUNK = 256
VMEM_LIMIT = 56 * 1024 * 1024

f32 = jnp.float32
bf16 = jnp.bfloat16


def _params(*sem):
    return pltpu.CompilerParams(dimension_semantics=sem, vmem_limit_bytes=VMEM_LIMIT)


def _resident(shape, layer=None):
    zeros = (0,) * len(shape)
    if layer is None:
        return pl.BlockSpec(shape, lambda *_: zeros, pipeline_mode=pl.Buffered(1))
    return pl.BlockSpec((None,) + tuple(shape), lambda *_: (layer,) + zeros, pipeline_mode=pl.Buffered(1))


def _rms(xf, g):
    y = xf * lax.rsqrt(jnp.mean(xf * xf, axis=-1, keepdims=True) + RMS_EPS)
    return y * g


def _dot(a, b):
    return jnp.dot(a, b, preferred_element_type=f32)


def _dot_nt(a, b):
    return lax.dot_general(a, b, (((1,), (1,)), ((), ())), preferred_element_type=f32)


def _dot_tn(a, b):
    return lax.dot_general(a, b, (((0,), (0,)), ((), ())), preferred_element_type=f32)


def _split_bf16(x, terms):
    parts = []
    r = x
    for _ in range(terms):
        p = r.astype(bf16)
        parts.append(p)
        r = r - p.astype(f32)
    return parts


def _group_mean(y, width):
    r = lax.broadcasted_iota(jnp.int32, (width, width), 0) // HEAD_DIM
    c = lax.broadcasted_iota(jnp.int32, (width, width), 1) // HEAD_DIM
    gm = jnp.where(r == c, 1.0 / HEAD_DIM, 0.0).astype(bf16)
    return _dot(y.astype(bf16), gm)


def _gelu(x):
    return jax.nn.gelu(x, approximate=True)


ROPE_HALF = HEAD_DIM // 2
ROPE_GROUPS = LANES // ROPE_HALF


def _rope_kernel(pos_ref, invf_ref, cos_ref, sin_ref):
    ang = pos_ref[...].astype(f32) * invf_ref[...]
    c = jnp.cos(ang)
    s = jnp.sin(ang)
    lane = lax.broadcasted_iota(jnp.int32, ang.shape, 1)
    first_group = lane < ROPE_HALF
    sign = jnp.where((lane % HEAD_DIM) < ROPE_HALF, -1.0, 1.0)

    def spread(v, j):
        v = pltpu.roll(v, LANES - ROPE_HALF * j, 1) if j else v
        v = jnp.where(first_group, v, 0.0)
        v = v + pltpu.roll(v, ROPE_HALF, 1)
        return v + pltpu.roll(v, 2 * ROPE_HALF, 1)

    for j in range(ROPE_GROUPS):
        cos_ref[j] = spread(c, j)
        sin_ref[j] = spread(s, j) * sign


def _rope_tables(positions):
    t = positions.size
    rows = t // ROPE_GROUPS
    pos = jnp.repeat(positions.reshape(ROPE_GROUPS, rows).T, ROPE_HALF, axis=1)
    inv_freq = ROPE_BASE ** (-jnp.arange(ROPE_HALF, dtype=f32) / ROPE_HALF)
    invf = jnp.tile(inv_freq, ROPE_GROUPS)[None, :]
    tr = min(rows, 1024)
    out_spec = pl.BlockSpec((ROPE_GROUPS, tr, LANES), lambda i: (0, i, 0))
    cos, sin = pl.pallas_call(
        _rope_kernel,
        grid=(rows // tr,),
        in_specs=[pl.BlockSpec((tr, LANES), lambda i: (i, 0)), _resident((1, LANES))],
        out_specs=[out_spec, out_spec],
        out_shape=[jax.ShapeDtypeStruct((ROPE_GROUPS, rows, LANES), f32)] * 2,
        compiler_params=_params("parallel"),
        name="rope_tables",
    )(pos, invf)
    return cos.reshape(t, LANES), sin.reshape(t, LANES)


def _proj_in_kernel(x_ref, g_ref, w_ref, cos_ref, sin_ref, sguw_ref, sgub_ref, vg_ref, fb_ref, place_ref,
                    oa_ref, rq_ref, rk_ref, rv_ref, rg_ref, fq_ref, fk_ref, fvt_ref, cend_ref, nrm_ref,
                    carry_ref, fv_ref, *, tm, blocks_per_seq):
    i = pl.program_id(0)
    n_chunks = tm // CHUNK
    h = _rms(x_ref[...], g_ref[...]).astype(bf16)

    def proj(lo, width):
        return _dot(h, w_ref[:, lo:lo + width])

    z_fox_vf = proj(2 * W_SGU + 4 * W_RET + 2 * W_FOX, W_FOX + LANES)

    ff = z_fox_vf[:, W_FOX:] + fb_ref[...]
    logf = -(jnp.maximum(-ff, 0.0) + jnp.log1p(jnp.exp(-jnp.abs(ff))))

    @pl.when(i % blocks_per_seq == 0)
    def _():
        carry_ref[...] = jnp.zeros_like(carry_ref)

    sub = min(tm, CUMSUM_ROWS)
    sub_tril = (lax.broadcasted_iota(jnp.int32, (sub, sub), 0) >= lax.broadcasted_iota(jnp.int32, (sub, sub), 1))
    sub_tril = jnp.where(sub_tril, 1.0, 0.0).astype(bf16)
    carry = carry_ref[...]
    cums = []
    for r in range(tm // sub):
        within = _dot(sub_tril, jnp.concatenate(_split_bf16(logf[r * sub:(r + 1) * sub], 3), axis=1))
        cums.append(within[:, :LANES] + within[:, LANES:2 * LANES] + within[:, 2 * LANES:] + carry)
        carry = cums[-1][sub - 1:sub, :]
    carry_ref[...] = carry
    cum = jnp.concatenate(cums, axis=0) * LOG2E
    for c in range(n_chunks):
        cend_ref[8 * c:8 * (c + 1), :] = jnp.broadcast_to(cum[(c + 1) * CHUNK - 1:(c + 1) * CHUNK, :], (8, LANES))
    placed = _dot(jnp.concatenate(_split_bf16(cum, C_TERMS), axis=1), place_ref[...]).astype(bf16)
    for p in range(N_PAIRS):
        fk_ref[:, (2 * p + 1) * LANES:(2 * p + 2) * LANES] = placed[:, p * LANES:(p + 1) * LANES]

    sq_row = lax.broadcasted_iota(jnp.int32, (CHUNK, CHUNK), 0)
    sq_col = lax.broadcasted_iota(jnp.int32, (CHUNK, CHUNK), 1)
    tril = sq_row >= sq_col
    first_head = sq_col < HEAD_DIM

    z_sgu = _gelu(proj(0, 2 * W_SGU))
    u = z_sgu[:, :W_SGU]
    v = z_sgu[:, W_SGU:]
    v = v * lax.rsqrt(_group_mean(v * v, W_SGU) + RMS_EPS) * vg_ref[...]
    for p in range(W_SGU // LANES):
        cols = slice(p * LANES, (p + 1) * LANES)
        w_pair = jnp.concatenate(
            [jnp.where(tril, sguw_ref[2 * p], 0.0), jnp.where(tril, sguw_ref[2 * p + 1], 0.0)],
            axis=0).astype(bf16)
        for c in range(n_chunks):
            rows = slice(c * CHUNK, (c + 1) * CHUNK)
            r = _dot(w_pair, v[rows, cols].astype(bf16))
            s = jnp.where(first_head, r[:CHUNK], r[CHUNK:]) + sgub_ref[:, cols]
            oa_ref[rows, cols] = (u[rows, cols] * s).astype(bf16)

    cosv = cos_ref[...]
    sinv = sin_ref[...]
    lane = lax.broadcasted_iota(jnp.int32, (tm, LANES), 1)
    first_half = (lane % HEAD_DIM) < (HEAD_DIM // 2)

    def rotary(z):
        outs = []
        for s in range(N_PAIRS):
            t = z[:, s * LANES:(s + 1) * LANES]
            partner = jnp.where(first_half, pltpu.roll(t, LANES - HEAD_DIM // 2, 1),
                                pltpu.roll(t, HEAD_DIM // 2, 1))
            outs.append(t * cosv + partner * sinv)
        return jnp.concatenate(outs, axis=1)

    base = 2 * W_SGU
    z_qk = proj(base, 2 * W_RET)
    rq_ref[...] = rotary(z_qk[:, :W_RET]).astype(bf16)
    rk_ref[...] = (rotary(z_qk[:, W_RET:]) * (HEAD_DIM ** -0.5)).astype(bf16)
    z_vg = proj(base + 2 * W_RET, 2 * W_RET).astype(bf16)
    rv_ref[...] = z_vg[:, :W_RET]
    rg_ref[...] = z_vg[:, W_RET:]

    base += 4 * W_RET
    z_fox_qk = proj(base, 2 * W_FOX)
    fq = (z_fox_qk[:, :W_FOX] * (LOG2E * HEAD_DIM ** -0.5)).astype(bf16)
    fq_ref[...] = fq
    fk = z_fox_qk[:, W_FOX:].astype(bf16)
    for p in range(N_PAIRS):
        fk_ref[:, 2 * p * LANES:(2 * p + 1) * LANES] = fk[:, p * LANES:(p + 1) * LANES]
    fv_ref[...] = z_fox_vf[:, :W_FOX]
    fvt_ref[...] = fv_ref[...].T.astype(bf16)

    gr = lax.broadcasted_iota(jnp.int32, (2 * W_FOX, LANES), 0) // HEAD_DIM
    gc = lax.broadcasted_iota(jnp.int32, (2 * W_FOX, LANES), 1)
    head_sum = jnp.where(gr == gc, 1.0, 0.0).astype(bf16)
    qk = jnp.concatenate([fq, fk], axis=1).astype(f32)
    norm2 = jnp.max(_dot((qk * qk).astype(bf16), head_sum), axis=0, keepdims=True)

    @pl.when(i % blocks_per_seq == 0)
    def _():
        nrm_ref[...] = jnp.zeros_like(nrm_ref)

    nrm_ref[...] = jnp.maximum(nrm_ref[...], norm2)


def _placement_matrix():
    place = np.zeros((C_TERMS * LANES, N_PAIRS * LANES), np.float32)
    for h in range(N_HEADS_FOX):
        for term in range(C_TERMS):
            place[term * LANES + h, (h // 2) * LANES + C_TERMS * (h % 2) + term] = 1.0
    return jnp.asarray(place, bf16)


def _proj_in(x2d, g, w_pad, cos4, sin4, sgu_w, sgu_b_full, sgu_vg, fox_b, *, layer, seq, tm):
    t = x2d.shape[0]
    row = lambda width: pl.BlockSpec((tm, width), lambda i: (i, 0))
    out_widths = [W_SGU] + [W_RET] * 4 + [W_FOX, 2 * W_FOX]
    place = _placement_matrix()
    kern = functools.partial(_proj_in_kernel, tm=tm, blocks_per_seq=seq // tm)
    return pl.pallas_call(
        kern,
        grid=(t // tm,),
        in_specs=[row(D_MODEL), _resident((1, D_MODEL)), _resident((D_MODEL, N_IN_PAD), layer),
                  row(LANES), row(LANES),
                  _resident((N_HEADS_SGU, CHUNK, CHUNK)), _resident((CHUNK, W_SGU)),
                  _resident((1, W_SGU)), _resident((1, LANES)), _resident(place.shape)],
        out_specs=[row(w) for w in out_widths] + [
            pl.BlockSpec((W_FOX, tm), lambda i: (0, i)),
            pl.BlockSpec((8 * tm // CHUNK, LANES), lambda i: (i, 0)),
            pl.BlockSpec((8, LANES), lambda i: (i // (seq // tm), 0))],
        out_shape=[jax.ShapeDtypeStruct((t, w), bf16) for w in out_widths] + [
            jax.ShapeDtypeStruct((W_FOX, t), bf16),
            jax.ShapeDtypeStruct((8 * t // CHUNK, LANES), f32),
            jax.ShapeDtypeStruct((8 * t // seq, LANES), f32)],
        scratch_shapes=[pltpu.VMEM((1, LANES), f32), pltpu.VMEM((tm, W_FOX), f32)],
        compiler_params=_params("arbitrary"),
        name="proj_in",
    )(x2d, g, w_pad, cos4, sin4, sgu_w, sgu_b_full, sgu_vg, fox_b, place)


def _retention_kernel(q_ref, k_ref, v_ref, g_ref, dec_ref, qd_ref, kd_ref, cd_ref, o_ref, state_ref, raw_ref,
                      *, tm, blocks_per_seq):
    i = pl.program_id(0)

    @pl.when(i % blocks_per_seq == 0)
    def _():
        state_ref[...] = jnp.zeros_like(state_ref)

    rc = RET_CHUNK
    first_head = lax.broadcasted_iota(jnp.int32, (rc, LANES), 1) < HEAD_DIM
    same_head = ((lax.broadcasted_iota(jnp.int32, (LANES, LANES), 0) < HEAD_DIM)
                 == (lax.broadcasted_iota(jnp.int32, (LANES, LANES), 1) < HEAD_DIM))
    zero = jnp.zeros((rc, LANES), bf16)
    states = [state_ref[p] for p in range(N_PAIRS)]
    for c in range(tm // rc):
        rows = slice(c * rc, (c + 1) * rc)
        for p in range(N_PAIRS):
            cols = slice(p * LANES, (p + 1) * LANES)
            q = q_ref[rows, cols]
            k = k_ref[rows, cols]
            v = v_ref[rows, cols]
            q_heads = jnp.concatenate([jnp.where(first_head, q, zero), jnp.where(first_head, zero, q)], axis=0)
            inner = _dot_nt(q_heads, k) * dec_ref[p]
            o_heads = _dot(inner.astype(bf16), v)
            o = (jnp.where(first_head, o_heads[:rc], o_heads[rc:])
                 + _dot(q, states[p].astype(bf16)) * qd_ref[p])
            k_dec = (k.astype(f32) * kd_ref[p]).astype(bf16)
            states[p] = states[p] * cd_ref[p] + jnp.where(same_head, _dot_tn(k_dec, v), 0.0)
            raw_ref[rows, cols] = o
    for p in range(N_PAIRS):
        state_ref[p] = states[p]
    o = raw_ref[...]
    ro = o * lax.rsqrt(_group_mean(o * o, W_RET) + RMS_EPS)
    gate = g_ref[...].astype(f32)
    o_ref[...] = (gate * jax.nn.sigmoid(gate) * ro).astype(bf16)


def _retention_tables():
    log_g = jnp.log(1.0 - 2.0 ** (-5.0 - jnp.arange(N_HEADS_RET, dtype=f32)))
    idx = jnp.arange(RET_CHUNK, dtype=f32)
    diff = idx[:, None] - idx[None, :]
    decay = jnp.where(diff >= 0, jnp.exp(log_g[:, None, None] * jnp.maximum(diff, 0.0)), 0.0)
    lane_g = jnp.repeat(log_g, HEAD_DIM).reshape(N_PAIRS, 1, LANES)
    q_decay = jnp.exp(lane_g * (idx + 1.0)[None, :, None])
    k_decay = jnp.exp(lane_g * (RET_CHUNK - 1.0 - idx)[None, :, None])
    chunk_decay = jnp.broadcast_to(jnp.exp(lane_g * RET_CHUNK).reshape(N_PAIRS, LANES, 1),
                                   (N_PAIRS, LANES, LANES))
    pair_decay = decay.reshape(N_PAIRS, 2 * RET_CHUNK, RET_CHUNK)
    return pair_decay, q_decay, k_decay, chunk_decay


def _retention(rq, rk, rv, rg, *, seq, tm):
    t = rq.shape[0]
    assert tm % RET_CHUNK == 0
    decay, q_decay, k_decay, chunk_decay = _retention_tables()
    row = pl.BlockSpec((tm, W_RET), lambda i: (i, 0))
    kern = functools.partial(_retention_kernel, tm=tm, blocks_per_seq=seq // tm)
    return pl.pallas_call(
        kern,
        grid=(t // tm,),
        in_specs=[row, row, row, row, _resident(decay.shape), _resident(q_decay.shape),
                  _resident(k_decay.shape), _resident(chunk_decay.shape)],
        out_specs=row,
        out_shape=jax.ShapeDtypeStruct((t, W_RET), bf16),
        scratch_shapes=[pltpu.VMEM((N_PAIRS, LANES, LANES), f32), pltpu.VMEM((tm, W_RET), f32)],
        compiler_params=_params("arbitrary"),
        name="retention",
    )(rq, rk, rv, rg, decay, q_decay, k_decay, chunk_decay)


SUM_ROWS = 16


def _fox_kernel(cend_ref, thr_ref, q_ref, k_ref, vt_ref, o_ref, s_ref, m_ref, acc_ref, *, tq, tk, n_chunks):
    b = pl.program_id(0)
    pair = pl.program_id(1)
    i = pl.program_id(2)

    def first_needed_block():
        heads = [b * N_HEADS_FOX + 2 * pair + a for a in range(2)]
        q_start = jnp.maximum((tq // CHUNK) * i - 1, 0)
        limit = [cend_ref[h * n_chunks + q_start] + thr_ref[h] for h in heads]

        def needed(j):
            key_end = (tk // CHUNK) * (jnp.maximum(j, 0) + 1) - 1
            flags = [cend_ref[h * n_chunks + key_end] <= lim for h, lim in zip(heads, limit)]
            return jnp.logical_and(j >= 0, jnp.logical_or(flags[0], flags[1]))

        return lax.while_loop(needed, lambda j: j - 1, 2 * i - 1) + 1

    first_block = first_needed_block()
    first_pair = (first_block + 1) // 2
    q = q_ref[...]
    lane = lax.broadcasted_iota(jnp.int32, (tq, LANES), 1)
    first_head = lane < HEAD_DIM
    zero = jnp.zeros((tq, LANES), bf16)
    q_aug = []
    for a in range(2):
        qa = jnp.where(first_head, q, zero) if a == 0 else jnp.where(first_head, zero, q)
        pick = jnp.where((lane >= C_TERMS * a) & (lane < C_TERMS * (a + 1)), -1.0, 0.0).astype(bf16)
        q_aug.append(jnp.concatenate([qa, pick], axis=1))
    m_ref[...] = jnp.full_like(m_ref, MASK_VALUE)
    acc_ref[...] = jnp.zeros_like(acc_ref)
    ones_rows = jnp.ones((SUM_ROWS, tk), bf16)

    def scores(j, slot, lo=0):
        off = pl.multiple_of(j * tk, tk)
        k = k_ref[pl.ds(off, tk), :]
        for a in range(2):
            s_ref[slot, a, :, lo:] = _dot_nt(k, q_aug[a][lo:])

    def consume(j, slot, diag=None, lo=0):
        off = pl.multiple_of(j * tk, tk)
        for a in range(2):
            s = s_ref[slot, a, :, lo:]
            if diag is not None:
                kpos = diag * tk + lax.broadcasted_iota(jnp.int32, (tk, tq - lo), 0)
                qpos = lo + lax.broadcasted_iota(jnp.int32, (tk, tq - lo), 1)
                s = jnp.where(kpos <= qpos, s, MASK_VALUE)
            m_old = m_ref[a, :, lo:]
            m_new = jnp.maximum(m_old, jnp.max(s, axis=0, keepdims=True))
            p = jnp.exp2(s - m_new)
            alpha = jnp.exp2(m_old - m_new)
            vt = jnp.concatenate([vt_ref[a * HEAD_DIM:(a + 1) * HEAD_DIM, pl.ds(off, tk)], ones_rows], axis=0)
            acc_ref[a, :, lo:] = alpha * acc_ref[a, :, lo:] + _dot(vt, p.astype(bf16))
            m_ref[a, :, lo:] = m_new

    @pl.when(first_block % 2 == 1)
    def _():
        scores(first_block, 1)
        scores(2 * first_pair, 0)
        consume(first_block, 1)

    @pl.when(first_block % 2 == 0)
    def _():
        scores(2 * first_pair, 0)

    def body(jj, carry):
        scores(2 * jj + 1, 1)
        consume(2 * jj, 0)
        scores(2 * jj + 2, 0)
        consume(2 * jj + 1, 1)
        return carry

    lax.fori_loop(first_pair, i, body, 0)
    scores(2 * i + 1, 1, lo=tk)
    consume(2 * i, 0, diag=0)
    consume(2 * i + 1, 1, diag=1, lo=tk)
    o_t = jnp.concatenate([acc_ref[a, :HEAD_DIM] / acc_ref[a, HEAD_DIM:HEAD_DIM + 1] for a in range(2)], axis=0)
    o_ref[...] = o_t.T.astype(bf16)


EXP2_UNDERFLOW = 153.0
NORM_SLACK = 1.01


def _fox(fq, fk_aug, fv_t, cend, norm2, *, batch, seq, tq):
    t = fq.shape[0]
    nq = seq // tq
    tk = tq // 2
    n_chunks = seq // CHUNK
    cend_flat = cend.reshape(batch, n_chunks, 8, LANES)[:, :, 0, :N_HEADS_FOX].transpose(0, 2, 1).reshape(-1)
    norm2 = norm2.reshape(batch, 8, LANES)[:, 0, :2 * N_HEADS_FOX].reshape(batch, 2, N_HEADS_FOX)
    thr = (2.0 * NORM_SLACK * jnp.sqrt(norm2[:, 0] * norm2[:, 1]) + EXP2_UNDERFLOW).reshape(-1)
    smem = pl.BlockSpec(memory_space=pltpu.SMEM)
    q_spec = pl.BlockSpec((tq, LANES), lambda b, p, i: (b * nq + i, p))
    return pl.pallas_call(
        functools.partial(_fox_kernel, tq=tq, tk=tk, n_chunks=n_chunks),
        grid=(batch, N_PAIRS, nq),
        in_specs=[smem, smem, q_spec,
                  pl.BlockSpec((seq, 2 * LANES), lambda b, p, i: (b, p)),
                  pl.BlockSpec((LANES, seq), lambda b, p, i: (p, b))],
        out_specs=q_spec,
        out_shape=jax.ShapeDtypeStruct((t, W_FOX), bf16),
        scratch_shapes=[pltpu.VMEM((2, 2, tk, tq), f32), pltpu.VMEM((2, 1, tq), f32),
                        pltpu.VMEM((2, HEAD_DIM + SUM_ROWS, tq), f32)],
        compiler_params=_params("parallel", "parallel", "arbitrary"),
        name="fox_attention",
    )(cend_flat, thr, fq, fk_aug, fv_t)


def _channel_kernel(a_ref, b_ref, c_ref, x_ref, p_ref, wo_ref, gmix_ref,
                    gpre_ref, wg_ref, wu_ref, cw_ref, cb_ref, wd_ref, gpost_ref,
                    gple_ref, wpg_ref, wpp_ref, gplepost_ref, o_ref, halo_ref,
                    *, tm, fc, blocks_per_seq):
    i = pl.program_id(0)

    @pl.when(i % blocks_per_seq == 0)
    def _():
        halo_ref[...] = jnp.zeros_like(halo_ref)

    cat = jnp.concatenate([a_ref[...], b_ref[...], c_ref[...]], axis=1)
    xf = x_ref[...] + _rms(_dot(cat, wo_ref[...]), gmix_ref[...])

    h = _rms(xf, gpre_ref[...]).astype(bf16)
    row8 = lax.broadcasted_iota(jnp.int32, (8, fc), 0)
    acc = jnp.zeros((tm, D_MODEL), f32)
    for c in range(D_FF // fc):
        cols = slice(c * fc, (c + 1) * fc)
        g = _dot(h, wg_ref[:, cols])
        up = _dot(h, wu_ref[:, cols])
        prev = halo_ref[:, cols]
        halo_ref[:, cols] = g[tm - 8:, :]
        g1 = pltpu.roll(g, 1, 0)
        g2 = pltpu.roll(g, 2, 0)
        g1 = jnp.concatenate([jnp.where(row8 < 1, pltpu.roll(prev, 1, 0), g1[:8]), g1[8:]], axis=0)
        g2 = jnp.concatenate([jnp.where(row8 < 2, pltpu.roll(prev, 2, 0), g2[:8]), g2[8:]], axis=0)
        conv = cb_ref[:, cols] + g2 * cw_ref[0:1, cols]
        conv = conv + g1 * cw_ref[1:2, cols]
        conv = conv + g * cw_ref[2:3, cols]
        act = _gelu(conv) * up
        acc = acc + _dot(act.astype(bf16), wd_ref[cols, :])
    xf = xf + _rms(acc, gpost_ref[...])

    gate = jax.nn.sigmoid(_dot(_rms(xf, gple_ref[...]).astype(bf16), wpg_ref[...]))
    e = _dot(p_ref[...].astype(bf16), wpp_ref[...])
    o_ref[...] = xf + _rms(e * gate, gplepost_ref[...])


def _channel(out_a, out_b, out_c, x2d, p2d, w_o, g_mix, g_pre, w_gate, w_up, conv_w, conv_b, w_down, g_post,
             g_ple, w_ple_gate, w_ple_proj, g_ple_post, *, layer, seq, tm, fc):
    t = x2d.shape[0]
    row = lambda width: pl.BlockSpec((tm, width), lambda i: (i, 0))
    gain = _resident((1, D_MODEL))
    kern = functools.partial(_channel_kernel, tm=tm, fc=fc, blocks_per_seq=seq // tm)
    return pl.pallas_call(
        kern,
        grid=(t // tm,),
        in_specs=[row(W_SGU), row(W_RET), row(W_FOX), row(D_MODEL),
                  pl.BlockSpec((tm, D_PLE), lambda i: (layer * (t // tm) + i, 0)),
                  _resident((D_MODEL, D_MODEL), layer), gain,
                  gain, _resident((D_MODEL, D_FF), layer), _resident((D_MODEL, D_FF), layer),
                  _resident(conv_w.shape), _resident((1, D_FF)), _resident((D_FF, D_MODEL), layer), gain,
                  gain, _resident((D_MODEL, D_MODEL), layer), _resident((D_PLE, D_MODEL), layer), gain],
        out_specs=row(D_MODEL),
        out_shape=jax.ShapeDtypeStruct((t, D_MODEL), f32),
        scratch_shapes=[pltpu.VMEM((8, D_FF), f32)],
        compiler_params=_params("arbitrary"),
        name="channel",
    )(out_a, out_b, out_c, x2d, p2d, w_o, g_mix, g_pre, w_gate, w_up, conv_w, conv_b, w_down, g_post,
      g_ple, w_ple_gate, w_ple_proj, g_ple_post)


def _largest_tile(seq, cap):
    tile = cap
    while seq % tile:
        tile //= 2
    return tile


def kernel(x, p, positions, mix_pre_g, w_in, sgu_v_g, sgu_w, sgu_b, fox_b_f, w_o, mix_post_g, ffn_pre_g,
           w_gate, w_up, conv_w, conv_b, w_down, ffn_post_g, ple_pre_g, w_ple_gate, w_ple_proj, ple_post_g):
    batch, seq, _ = x.shape
    depth = w_in.shape[0]
    t = batch * seq
    assert seq % CHUNK == 0
    tm = _largest_tile(seq, 512)
    tm_in = _largest_tile(seq, 1024)
    tq = _largest_tile(seq, 1024)
    fc = 2048

    cos4, sin4 = _rope_tables(positions)
    x2d = x.reshape(t, D_MODEL)
    p2d = p.reshape(depth * t, D_PLE)
    w_pad = jnp.pad(w_in.astype(bf16), ((0, 0), (0, 0), (0, N_IN_PAD - w_in.shape[2])))
    w_o, w_gate, w_up, w_down, w_ple_gate, w_ple_proj = (
        w.astype(bf16) for w in (w_o, w_gate, w_up, w_down, w_ple_gate, w_ple_proj))
    for i in range(depth):
        sgu_b_full = jnp.repeat(sgu_b[i].T, HEAD_DIM, axis=1)
        fox_b = jnp.pad(fox_b_f[i], (0, LANES - N_HEADS_FOX))[None, :]
        out_a, rq, rk, rv, rg, fq, fk_aug, fv_t, cend, norm2 = _proj_in(
            x2d, mix_pre_g[i][None, :], w_pad, cos4, sin4, sgu_w[i], sgu_b_full,
            sgu_v_g[i].reshape(1, W_SGU), fox_b, layer=i, seq=seq, tm=tm_in)
        out_b = _retention(rq, rk, rv, rg, seq=seq, tm=tm_in)
        out_c = _fox(fq, fk_aug, fv_t, cend, norm2, batch=batch, seq=seq, tq=tq)
        x2d = _channel(out_a, out_b, out_c, x2d, p2d, w_o, mix_post_g[i][None, :],
                       ffn_pre_g[i][None, :], w_gate, w_up, conv_w[i], conv_b[i][None, :], w_down,
                       ffn_post_g[i][None, :], ple_pre_g[i][None, :], w_ple_gate, w_ple_proj,
                       ple_post_g[i][None, :], layer=i, seq=seq, tm=tm, fc=fc)
    return x2d.reshape(batch, seq, D_MODEL)
```

```python
import functools

import numpy as np
import jax
import jax.numpy as jnp
from jax import lax
from jax.experimental import pallas as pl
from jax.experimental.pallas import tpu as pltpu

D_MODEL = 1024
D_PLE = 256
HEAD_DIM = 64
N_HEADS_SGU = 4
N_HEADS_RET = 6
N_HEADS_FOX = 6
W_SGU = N_HEADS_SGU * HEAD_DIM
W_RET = N_HEADS_RET * HEAD_DIM
W_FOX = N_HEADS_FOX * HEAD_DIM
CHUNK = 128
D_FF = 4 * D_MODEL
ROPE_BASE = 10000.0
RMS_EPS = 1e-6
LANES = 128
N_PAIRS = W_RET // LANES
N_IN_PAD = 2 * W_SGU + 4 * W_RET + 3 * W_FOX + LANES
MASK_VALUE = -1e30
LOG2E = 1.4426950408889634
C_TERMS = 3
CUMSUM_ROWS = 128
RET_CHUNK = 256
VMEM_LIMIT = 56 * 1024 * 1024

f32 = jnp.float32
bf16 = jnp.bfloat16


def _params(*sem):
    return pltpu.CompilerParams(dimension_semantics=sem, vmem_limit_bytes=VMEM_LIMIT)


def _resident(shape, layer=None):
    zeros = (0,) * len(shape)
    if layer is None:
        return pl.BlockSpec(shape, lambda *_: zeros, pipeline_mode=pl.Buffered(1))
    return pl.BlockSpec((None,) + tuple(shape), lambda *_: (layer,) + zeros, pipeline_mode=pl.Buffered(1))


def _rms(xf, g):
    y = xf * lax.rsqrt(jnp.mean(xf * xf, axis=-1, keepdims=True) + RMS_EPS)
    return y * g


def _dot(a, b):
    return jnp.dot(a, b, preferred_element_type=f32)


def _dot_nt(a, b):
    return lax.dot_general(a, b, (((1,), (1,)), ((), ())), preferred_element_type=f32)


def _dot_tn(a, b):
    return lax.dot_general(a, b, (((0,), (0,)), ((), ())), preferred_element_type=f32)


def _split_bf16(x, terms):
    parts = []
    r = x
    for _ in range(terms):
        p = r.astype(bf16)
        parts.append(p)
        r = r - p.astype(f32)
    return parts


def _group_mean(y, width):
    r = lax.broadcasted_iota(jnp.int32, (width, width), 0) // HEAD_DIM
    c = lax.broadcasted_iota(jnp.int32, (width, width), 1) // HEAD_DIM
    gm = jnp.where(r == c, 1.0 / HEAD_DIM, 0.0).astype(bf16)
    return _dot(y.astype(bf16), gm)


def _gelu(x):
    return jax.nn.gelu(x, approximate=True)


ROPE_HALF = HEAD_DIM // 2
ROPE_GROUPS = LANES // ROPE_HALF


def _rope_kernel(pos_ref, invf_ref, cos_ref, sin_ref):
    ang = pos_ref[...].astype(f32) * invf_ref[...]
    c = jnp.cos(ang)
    s = jnp.sin(ang)
    lane = lax.broadcasted_iota(jnp.int32, ang.shape, 1)
    first_group = lane < ROPE_HALF
    sign = jnp.where((lane % HEAD_DIM) < ROPE_HALF, -1.0, 1.0)

    def spread(v, j):
        v = pltpu.roll(v, LANES - ROPE_HALF * j, 1) if j else v
        v = jnp.where(first_group, v, 0.0)
        v = v + pltpu.roll(v, ROPE_HALF, 1)
        return v + pltpu.roll(v, 2 * ROPE_HALF, 1)

    for j in range(ROPE_GROUPS):
        cos_ref[j] = spread(c, j)
        sin_ref[j] = spread(s, j) * sign


def _rope_tables(positions):
    t = positions.size
    rows = t // ROPE_GROUPS
    pos = jnp.repeat(positions.reshape(ROPE_GROUPS, rows).T, ROPE_HALF, axis=1)
    inv_freq = ROPE_BASE ** (-jnp.arange(ROPE_HALF, dtype=f32) / ROPE_HALF)
    invf = jnp.tile(inv_freq, ROPE_GROUPS)[None, :]
    tr = min(rows, 1024)
    out_spec = pl.BlockSpec((ROPE_GROUPS, tr, LANES), lambda i: (0, i, 0))
    cos, sin = pl.pallas_call(
        _rope_kernel,
        grid=(rows // tr,),
        in_specs=[pl.BlockSpec((tr, LANES), lambda i: (i, 0)), _resident((1, LANES))],
        out_specs=[out_spec, out_spec],
        out_shape=[jax.ShapeDtypeStruct((ROPE_GROUPS, rows, LANES), f32)] * 2,
        compiler_params=_params("parallel"),
        name="rope_tables",
    )(pos, invf)
    return cos.reshape(t, LANES), sin.reshape(t, LANES)


def _proj_in_kernel(x_ref, g_ref, w_ref, cos_ref, sin_ref, sguw_ref, sgub_ref, vg_ref, fb_ref, place_ref,
                    oa_ref, rq_ref, rk_ref, rv_ref, rg_ref, fq_ref, fk_ref, fvt_ref, cend_ref, nrm_ref,
                    carry_ref, fv_ref, *, tm, blocks_per_seq):
    i = pl.program_id(0)
    n_chunks = tm // CHUNK
    h = _rms(x_ref[...], g_ref[...]).astype(bf16)

    def proj(lo, width):
        return _dot(h, w_ref[:, lo:lo + width])

    z_fox_vf = proj(2 * W_SGU + 4 * W_RET + 2 * W_FOX, W_FOX + LANES)

    ff = z_fox_vf[:, W_FOX:] + fb_ref[...]
    logf = -(jnp.maximum(-ff, 0.0) + jnp.log1p(jnp.exp(-jnp.abs(ff))))

    @pl.when(i % blocks_per_seq == 0)
    def _():
        carry_ref[...] = jnp.zeros_like(carry_ref)

    sub = min(tm, CUMSUM_ROWS)
    sub_tril = (lax.broadcasted_iota(jnp.int32, (sub, sub), 0) >= lax.broadcasted_iota(jnp.int32, (sub, sub), 1))
    sub_tril = jnp.where(sub_tril, 1.0, 0.0).astype(bf16)
    carry = carry_ref[...]
    cums = []
    for r in range(tm // sub):
        within = _dot(sub_tril, jnp.concatenate(_split_bf16(logf[r * sub:(r + 1) * sub], 3), axis=1))
        cums.append(within[:, :LANES] + within[:, LANES:2 * LANES] + within[:, 2 * LANES:] + carry)
        carry = cums[-1][sub - 1:sub, :]
    carry_ref[...] = carry
    cum = jnp.concatenate(cums, axis=0) * LOG2E
    for c in range(n_chunks):
        cend_ref[8 * c:8 * (c + 1), :] = jnp.broadcast_to(cum[(c + 1) * CHUNK - 1:(c + 1) * CHUNK, :], (8, LANES))
    placed = _dot(jnp.concatenate(_split_bf16(cum, C_TERMS), axis=1), place_ref[...]).astype(bf16)
    for p in range(N_PAIRS):
        fk_ref[:, (2 * p + 1) * LANES:(2 * p + 2) * LANES] = placed[:, p * LANES:(p + 1) * LANES]

    sq_row = lax.broadcasted_iota(jnp.int32, (CHUNK, CHUNK), 0)
    sq_col = lax.broadcasted_iota(jnp.int32, (CHUNK, CHUNK), 1)
    tril = sq_row >= sq_col
    first_head = sq_col < HEAD_DIM

    z_sgu = _gelu(proj(0, 2 * W_SGU))
    u = z_sgu[:, :W_SGU]
    v = z_sgu[:, W_SGU:]
    v = v * lax.rsqrt(_group_mean(v * v, W_SGU) + RMS_EPS) * vg_ref[...]
    for p in range(W_SGU // LANES):
        cols = slice(p * LANES, (p + 1) * LANES)
        w_pair = jnp.concatenate(
            [jnp.where(tril, sguw_ref[2 * p], 0.0), jnp.where(tril, sguw_ref[2 * p + 1], 0.0)],
            axis=0).astype(bf16)
        for c in range(n_chunks):
            rows = slice(c * CHUNK, (c + 1) * CHUNK)
            r = _dot(w_pair, v[rows, cols].astype(bf16))
            s = jnp.where(first_head, r[:CHUNK], r[CHUNK:]) + sgub_ref[:, cols]
            oa_ref[rows, cols] = (u[rows, cols] * s).astype(bf16)

    cosv = cos_ref[...]
    sinv = sin_ref[...]
    lane = lax.broadcasted_iota(jnp.int32, (tm, LANES), 1)
    first_half = (lane % HEAD_DIM) < (HEAD_DIM // 2)

    def rotary(z):
        outs = []
        for s in range(N_PAIRS):
            t = z[:, s * LANES:(s + 1) * LANES]
            partner = jnp.where(first_half, pltpu.roll(t, LANES - HEAD_DIM // 2, 1),
                                pltpu.roll(t, HEAD_DIM // 2, 1))
            outs.append(t * cosv + partner * sinv)
        return jnp.concatenate(outs, axis=1)

    base = 2 * W_SGU
    z_qk = proj(base, 2 * W_RET)
    rq_ref[...] = rotary(z_qk[:, :W_RET]).astype(bf16)
    rk_ref[...] = (rotary(z_qk[:, W_RET:]) * (HEAD_DIM ** -0.5)).astype(bf16)
    z_vg = proj(base + 2 * W_RET, 2 * W_RET).astype(bf16)
    rv_ref[...] = z_vg[:, :W_RET]
    rg_ref[...] = z_vg[:, W_RET:]

    base += 4 * W_RET
    z_fox_qk = proj(base, 2 * W_FOX)
    fq = (z_fox_qk[:, :W_FOX] * (LOG2E * HEAD_DIM ** -0.5)).astype(bf16)
    fq_ref[...] = fq
    fk = z_fox_qk[:, W_FOX:].astype(bf16)
    for p in range(N_PAIRS):
        fk_ref[:, 2 * p * LANES:(2 * p + 1) * LANES] = fk[:, p * LANES:(p + 1) * LANES]
    fv_ref[...] = z_fox_vf[:, :W_FOX]
    fvt_ref[...] = fv_ref[...].T.astype(bf16)

    gr = lax.broadcasted_iota(jnp.int32, (2 * W_FOX, LANES), 0) // HEAD_DIM
    gc = lax.broadcasted_iota(jnp.int32, (2 * W_FOX, LANES), 1)
    head_sum = jnp.where(gr == gc, 1.0, 0.0).astype(bf16)
    qk = jnp.concatenate([fq, fk], axis=1).astype(f32)
    norm2 = jnp.max(_dot((qk * qk).astype(bf16), head_sum), axis=0, keepdims=True)

    @pl.when(i % blocks_per_seq == 0)
    def _():
        nrm_ref[...] = jnp.zeros_like(nrm_ref)

    nrm_ref[...] = jnp.maximum(nrm_ref[...], norm2)


def _placement_matrix():
    place = np.zeros((C_TERMS * LANES, N_PAIRS * LANES), np.float32)
    for h in range(N_HEADS_FOX):
        for term in range(C_TERMS):
            place[term * LANES + h, (h // 2) * LANES + C_TERMS * (h % 2) + term] = 1.0
    return jnp.asarray(place, bf16)


def _proj_in(x2d, g, w_pad, cos4, sin4, sgu_w, sgu_b_full, sgu_vg, fox_b, *, layer, seq, tm):
    t = x2d.shape[0]
    row = lambda width: pl.BlockSpec((tm, width), lambda i: (i, 0))
    out_widths = [W_SGU] + [W_RET] * 4 + [W_FOX, 2 * W_FOX]
    place = _placement_matrix()
    kern = functools.partial(_proj_in_kernel, tm=tm, blocks_per_seq=seq // tm)
    return pl.pallas_call(
        kern,
        grid=(t // tm,),
        in_specs=[row(D_MODEL), _resident((1, D_MODEL)), _resident((D_MODEL, N_IN_PAD), layer),
                  row(LANES), row(LANES),
                  _resident((N_HEADS_SGU, CHUNK, CHUNK)), _resident((CHUNK, W_SGU)),
                  _resident((1, W_SGU)), _resident((1, LANES)), _resident(place.shape)],
        out_specs=[row(w) for w in out_widths] + [
            pl.BlockSpec((W_FOX, tm), lambda i: (0, i)),
            pl.BlockSpec((8 * tm // CHUNK, LANES), lambda i: (i, 0)),
            pl.BlockSpec((8, LANES), lambda i: (i // (seq // tm), 0))],
        out_shape=[jax.ShapeDtypeStruct((t, w), bf16) for w in out_widths] + [
            jax.ShapeDtypeStruct((W_FOX, t), bf16),
            jax.ShapeDtypeStruct((8 * t // CHUNK, LANES), f32),
            jax.ShapeDtypeStruct((8 * t // seq, LANES), f32)],
        scratch_shapes=[pltpu.VMEM((1, LANES), f32), pltpu.VMEM((tm, W_FOX), f32)],
        compiler_params=_params("arbitrary"),
        name="proj_in",
    )(x2d, g, w_pad, cos4, sin4, sgu_w, sgu_b_full, sgu_vg, fox_b, place)


def _retention_kernel(q_ref, k_ref, v_ref, g_ref, dec_ref, qd_ref, kd_ref, cd_ref, o_ref, state_ref, raw_ref,
                      *, tm, blocks_per_seq):
    i = pl.program_id(0)

    @pl.when(i % blocks_per_seq == 0)
    def _():
        state_ref[...] = jnp.zeros_like(state_ref)

    rc = RET_CHUNK
    first_head = lax.broadcasted_iota(jnp.int32, (rc, LANES), 1) < HEAD_DIM
    same_head = ((lax.broadcasted_iota(jnp.int32, (LANES, LANES), 0) < HEAD_DIM)
                 == (lax.broadcasted_iota(jnp.int32, (LANES, LANES), 1) < HEAD_DIM))
    zero = jnp.zeros((rc, LANES), bf16)
    states = [state_ref[p] for p in range(N_PAIRS)]
    for c in range(tm // rc):
        rows = slice(c * rc, (c + 1) * rc)
        for p in range(N_PAIRS):
            cols = slice(p * LANES, (p + 1) * LANES)
            q = q_ref[rows, cols]
            k = k_ref[rows, cols]
            v = v_ref[rows, cols]
            q_heads = jnp.concatenate([jnp.where(first_head, q, zero), jnp.where(first_head, zero, q)], axis=0)
            inner = _dot_nt(q_heads, k) * dec_ref[p]
            o_heads = _dot(inner.astype(bf16), v)
            o = (jnp.where(first_head, o_heads[:rc], o_heads[rc:])
                 + _dot(q, states[p].astype(bf16)) * qd_ref[p])
            k_dec = (k.astype(f32) * kd_ref[p]).astype(bf16)
            states[p] = states[p] * cd_ref[p] + jnp.where(same_head, _dot_tn(k_dec, v), 0.0)
            raw_ref[rows, cols] = o
    for p in range(N_PAIRS):
        state_ref[p] = states[p]
    o = raw_ref[...]
    ro = o * lax.rsqrt(_group_mean(o * o, W_RET) + RMS_EPS)
    gate = g_ref[...].astype(f32)
    o_ref[...] = (gate * jax.nn.sigmoid(gate) * ro).astype(bf16)


def _retention_tables():
    log_g = jnp.log(1.0 - 2.0 ** (-5.0 - jnp.arange(N_HEADS_RET, dtype=f32)))
    idx = jnp.arange(RET_CHUNK, dtype=f32)
    diff = idx[:, None] - idx[None, :]
    decay = jnp.where(diff >= 0, jnp.exp(log_g[:, None, None] * jnp.maximum(diff, 0.0)), 0.0)
    lane_g = jnp.repeat(log_g, HEAD_DIM).reshape(N_PAIRS, 1, LANES)
    q_decay = jnp.exp(lane_g * (idx + 1.0)[None, :, None])
    k_decay = jnp.exp(lane_g * (RET_CHUNK - 1.0 - idx)[None, :, None])
    chunk_decay = jnp.broadcast_to(jnp.exp(lane_g * RET_CHUNK).reshape(N_PAIRS, LANES, 1),
                                   (N_PAIRS, LANES, LANES))
    pair_decay = decay.reshape(N_PAIRS, 2 * RET_CHUNK, RET_CHUNK)
    return pair_decay, q_decay, k_decay, chunk_decay


def _retention(rq, rk, rv, rg, *, seq, tm):
    t = rq.shape[0]
    assert tm % RET_CHUNK == 0
    decay, q_decay, k_decay, chunk_decay = _retention_tables()
    row = pl.BlockSpec((tm, W_RET), lambda i: (i, 0))
    kern = functools.partial(_retention_kernel, tm=tm, blocks_per_seq=seq // tm)
    return pl.pallas_call(
        kern,
        grid=(t // tm,),
        in_specs=[row, row, row, row, _resident(decay.shape), _resident(q_decay.shape),
                  _resident(k_decay.shape), _resident(chunk_decay.shape)],
        out_specs=row,
        out_shape=jax.ShapeDtypeStruct((t, W_RET), bf16),
        scratch_shapes=[pltpu.VMEM((N_PAIRS, LANES, LANES), f32), pltpu.VMEM((tm, W_RET), f32)],
        compiler_params=_params("arbitrary"),
        name="retention",
    )(rq, rk, rv, rg, decay, q_decay, k_decay, chunk_decay)


SUM_ROWS = 16


def _fox_kernel(cend_ref, thr_ref, q_ref, k_ref, vt_ref, o_ref, s_ref, m_ref, acc_ref, *, tq, tk, n_chunks):
    b = pl.program_id(0)
    pair = pl.program_id(1)
    i = pl.program_id(2)

    def first_needed_block():
        heads = [b * N_HEADS_FOX + 2 * pair + a for a in range(2)]
        q_start = jnp.maximum((tq // CHUNK) * i - 1, 0)
        limit = [cend_ref[h * n_chunks + q_start] + thr_ref[h] for h in heads]

        def needed(j):
            key_end = (tk // CHUNK) * (jnp.maximum(j, 0) + 1) - 1
            flags = [cend_ref[h * n_chunks + key_end] <= lim for h, lim in zip(heads, limit)]
            return jnp.logical_and(j >= 0, jnp.logical_or(flags[0], flags[1]))

        return lax.while_loop(needed, lambda j: j - 1, 2 * i - 1) + 1

    first_block = first_needed_block()
    first_pair = (first_block + 1) // 2
    q = q_ref[...]
    lane = lax.broadcasted_iota(jnp.int32, (tq, LANES), 1)
    first_head = lane < HEAD_DIM
    zero = jnp.zeros((tq, LANES), bf16)
    q_aug = []
    for a in range(2):
        qa = jnp.where(first_head, q, zero) if a == 0 else jnp.where(first_head, zero, q)
        pick = jnp.where((lane >= C_TERMS * a) & (lane < C_TERMS * (a + 1)), -1.0, 0.0).astype(bf16)
        q_aug.append(jnp.concatenate([qa, pick], axis=1))
    m_ref[...] = jnp.full_like(m_ref, MASK_VALUE)
    acc_ref[...] = jnp.zeros_like(acc_ref)
    ones_rows = jnp.ones((SUM_ROWS, tk), bf16)

    def scores(j, slot, lo=0):
        off = pl.multiple_of(j * tk, tk)
        k = k_ref[pl.ds(off, tk), :]
        for a in range(2):
            s_ref[slot, a, :, lo:] = _dot_nt(k, q_aug[a][lo:])

    def consume(j, slot, diag=None, lo=0):
        off = pl.multiple_of(j * tk, tk)
        for a in range(2):
            s = s_ref[slot, a, :, lo:]
            if diag is not None:
                kpos = diag * tk + lax.broadcasted_iota(jnp.int32, (tk, tq - lo), 0)
                qpos = lo + lax.broadcasted_iota(jnp.int32, (tk, tq - lo), 1)
                s = jnp.where(kpos <= qpos, s, MASK_VALUE)
            m_old = m_ref[a, :, lo:]
            m_new = jnp.maximum(m_old, jnp.max(s, axis=0, keepdims=True))
            p = jnp.exp2(s - m_new)
            alpha = jnp.exp2(m_old - m_new)
            vt = jnp.concatenate([vt_ref[a * HEAD_DIM:(a + 1) * HEAD_DIM, pl.ds(off, tk)], ones_rows], axis=0)
            acc_ref[a, :, lo:] = alpha * acc_ref[a, :, lo:] + _dot(vt, p.astype(bf16))
            m_ref[a, :, lo:] = m_new

    @pl.when(first_block % 2 == 1)
    def _():
        scores(first_block, 1)
        scores(2 * first_pair, 0)
        consume(first_block, 1)

    @pl.when(first_block % 2 == 0)
    def _():
        scores(2 * first_pair, 0)

    def body(jj, carry):
        scores(2 * jj + 1, 1)
        consume(2 * jj, 0)
        scores(2 * jj + 2, 0)
        consume(2 * jj + 1, 1)
        return carry

    lax.fori_loop(first_pair, i, body, 0)
    scores(2 * i + 1, 1, lo=tk)
    consume(2 * i, 0, diag=0)
    consume(2 * i + 1, 1, diag=1, lo=tk)
    o_t = jnp.concatenate([acc_ref[a, :HEAD_DIM] / acc_ref[a, HEAD_DIM:HEAD_DIM + 1] for a in range(2)], axis=0)
    o_ref[...] = o_t.T.astype(bf16)


EXP2_UNDERFLOW = 153.0
NORM_SLACK = 1.01


def _fox(fq, fk_aug, fv_t, cend, norm2, *, batch, seq, tq):
    t = fq.shape[0]
    nq = seq // tq
    tk = tq // 2
    n_chunks = seq // CHUNK
    cend_flat = cend.reshape(batch, n_chunks, 8, LANES)[:, :, 0, :N_HEADS_FOX].transpose(0, 2, 1).reshape(-1)
    norm2 = norm2.reshape(batch, 8, LANES)[:, 0, :2 * N_HEADS_FOX].reshape(batch, 2, N_HEADS_FOX)
    thr = (2.0 * NORM_SLACK * jnp.sqrt(norm2[:, 0] * norm2[:, 1]) + EXP2_UNDERFLOW).reshape(-1)
    smem = pl.BlockSpec(memory_space=pltpu.SMEM)
    q_spec = pl.BlockSpec((tq, LANES), lambda b, p, i: (b * nq + i, p))
    return pl.pallas_call(
        functools.partial(_fox_kernel, tq=tq, tk=tk, n_chunks=n_chunks),
        grid=(batch, N_PAIRS, nq),
        in_specs=[smem, smem, q_spec,
                  pl.BlockSpec((seq, 2 * LANES), lambda b, p, i: (b, p)),
                  pl.BlockSpec((LANES, seq), lambda b, p, i: (p, b))],
        out_specs=q_spec,
        out_shape=jax.ShapeDtypeStruct((t, W_FOX), bf16),
        scratch_shapes=[pltpu.VMEM((2, 2, tk, tq), f32), pltpu.VMEM((2, 1, tq), f32),
                        pltpu.VMEM((2, HEAD_DIM + SUM_ROWS, tq), f32)],
        compiler_params=_params("parallel", "parallel", "arbitrary"),
        name="fox_attention",
    )(cend_flat, thr, fq, fk_aug, fv_t)


def _channel_kernel(a_ref, b_ref, c_ref, x_ref, p_ref, wo_ref, gmix_ref,
                    gpre_ref, wg_ref, wu_ref, cw_ref, cb_ref, wd_ref, gpost_ref,
                    gple_ref, wpg_ref, wpp_ref, gplepost_ref, o_ref, halo_ref, h_ref, acc_ref,
                    *, tm, fc, blocks_per_seq):
    i = pl.program_id(0)

    @pl.when(i % blocks_per_seq == 0)
    def _():
        halo_ref[...] = jnp.zeros_like(halo_ref)

    cat = jnp.concatenate([a_ref[...], b_ref[...], c_ref[...]], axis=1)
    o_ref[...] = x_ref[...] + _rms(_dot(cat, wo_ref[...]), gmix_ref[...])

    h_ref[...] = _rms(o_ref[...], gpre_ref[...]).astype(bf16)
    row8 = lax.broadcasted_iota(jnp.int32, (8, fc), 0)
    for c in range(D_FF // fc):
        cols = slice(c * fc, (c + 1) * fc)
        h = h_ref[...]
        g = _dot(h, wg_ref[:, cols])
        up = _dot(h, wu_ref[:, cols])
        prev = halo_ref[:, cols]
        halo_ref[:, cols] = g[tm - 8:, :]
        g1 = pltpu.roll(g, 1, 0)
        g2 = pltpu.roll(g, 2, 0)
        g1 = jnp.concatenate([jnp.where(row8 < 1, pltpu.roll(prev, 1, 0), g1[:8]), g1[8:]], axis=0)
        g2 = jnp.concatenate([jnp.where(row8 < 2, pltpu.roll(prev, 2, 0), g2[:8]), g2[8:]], axis=0)
        conv = cb_ref[:, cols] + g2 * cw_ref[0:1, cols]
        conv = conv + g1 * cw_ref[1:2, cols]
        conv = conv + g * cw_ref[2:3, cols]
        act = _gelu(conv) * up
        down = _dot(act.astype(bf16), wd_ref[cols, :])
        acc_ref[...] = down if c == 0 else acc_ref[...] + down
    o_ref[...] = o_ref[...] + _rms(acc_ref[...], gpost_ref[...])

    gate = jax.nn.sigmoid(_dot(_rms(o_ref[...], gple_ref[...]).astype(bf16), wpg_ref[...]))
    e = _dot(p_ref[...].astype(bf16), wpp_ref[...])
    o_ref[...] = o_ref[...] + _rms(e * gate, gplepost_ref[...])


def _channel(out_a, out_b, out_c, x2d, p2d, w_o, g_mix, g_pre, w_gate, w_up, conv_w, conv_b, w_down, g_post,
             g_ple, w_ple_gate, w_ple_proj, g_ple_post, *, layer, seq, tm, fc):
    t = x2d.shape[0]
    row = lambda width: pl.BlockSpec((tm, width), lambda i: (i, 0))
    gain = _resident((1, D_MODEL))
    kern = functools.partial(_channel_kernel, tm=tm, fc=fc, blocks_per_seq=seq // tm)
    return pl.pallas_call(
        kern,
        grid=(t // tm,),
        in_specs=[row(W_SGU), row(W_RET), row(W_FOX), row(D_MODEL),
                  pl.BlockSpec((tm, D_PLE), lambda i: (layer * (t // tm) + i, 0)),
                  _resident((D_MODEL, D_MODEL), layer), gain,
                  gain, _resident((D_MODEL, D_FF), layer), _resident((D_MODEL, D_FF), layer),
                  _resident(conv_w.shape), _resident((1, D_FF)), _resident((D_FF, D_MODEL), layer), gain,
                  gain, _resident((D_MODEL, D_MODEL), layer), _resident((D_PLE, D_MODEL), layer), gain],
        out_specs=row(D_MODEL),
        out_shape=jax.ShapeDtypeStruct((t, D_MODEL), f32),
        scratch_shapes=[pltpu.VMEM((8, D_FF), f32), pltpu.VMEM((tm, D_MODEL), bf16),
                        pltpu.VMEM((tm, D_MODEL), f32)],
        compiler_params=_params("arbitrary"),
        name="channel",
    )(out_a, out_b, out_c, x2d, p2d, w_o, g_mix, g_pre, w_gate, w_up, conv_w, conv_b, w_down, g_post,
      g_ple, w_ple_gate, w_ple_proj, g_ple_post)


def _largest_tile(seq, cap):
    tile = cap
    while seq % tile:
        tile //= 2
    return tile


def kernel(x, p, positions, mix_pre_g, w_in, sgu_v_g, sgu_w, sgu_b, fox_b_f, w_o, mix_post_g, ffn_pre_g,
           w_gate, w_up, conv_w, conv_b, w_down, ffn_post_g, ple_pre_g, w_ple_gate, w_ple_proj, ple_post_g):
    batch, seq, _ = x.shape
    depth = w_in.shape[0]
    t = batch * seq
    assert seq % CHUNK == 0
    tm = _largest_tile(seq, 512)
    tm_in = _largest_tile(seq, 1024)
    tq = _largest_tile(seq, 1024)
    fc = 2048

    cos4, sin4 = _rope_tables(positions)
    x2d = x.reshape(t, D_MODEL)
    p2d = p.reshape(depth * t, D_PLE)
    w_pad = jnp.pad(w_in.astype(bf16), ((0, 0), (0, 0), (0, N_IN_PAD - w_in.shape[2])))
    w_o, w_gate, w_up, w_down, w_ple_gate, w_ple_proj = (
        w.astype(bf16) for w in (w_o, w_gate, w_up, w_down, w_ple_gate, w_ple_proj))
    for i in range(depth):
        sgu_b_full = jnp.repeat(sgu_b[i].T, HEAD_DIM, axis=1)
        fox_b = jnp.pad(fox_b_f[i], (0, LANES - N_HEADS_FOX))[None, :]
        out_a, rq, rk, rv, rg, fq, fk_aug, fv_t, cend, norm2 = _proj_in(
            x2d, mix_pre_g[i][None, :], w_pad, cos4, sin4, sgu_w[i], sgu_b_full,
            sgu_v_g[i].reshape(1, W_SGU), fox_b, layer=i, seq=seq, tm=tm_in)
        out_b = _retention(rq, rk, rv, rg, seq=seq, tm=tm_in)
        out_c = _fox(fq, fk_aug, fv_t, cend, norm2, batch=batch, seq=seq, tq=tq)
        x2d = _channel(out_a, out_b, out_c, x2d, p2d, w_o, mix_post_g[i][None, :],
                       ffn_pre_g[i][None, :], w_gate, w_up, conv_w[i], conv_b[i][None, :], w_down,
                       ffn_post_g[i][None, :], ple_pre_g[i][None, :], w_ple_gate, w_ple_proj,
                       ple_post_g[i][None, :], layer=i, seq=seq, tm=tm, fc=fc)
    return x2d.reshape(batch, seq, D_MODEL)
```
